```python
import jax, jax.numpy as jnp
from jax import lax
import numpy as np

D_MODEL = 4096
BATCH = 1
SEQ = 16384
DEPTH = 2
DEC_BATCH = 2
DEC_SEQ = 4096
PAST_LEN = 128

GRID_W = 64
HEAD_DIM = 128
N_Q_HEADS = 16
N_KV_HEADS = 4
Q_PER_KV = N_Q_HEADS // N_KV_HEADS
ATTN_WIDTH = N_Q_HEADS * HEAD_DIM
KV_WIDTH = N_KV_HEADS * HEAD_DIM
CONV_WIDTH = D_MODEL // 4
CONV_K = 3
POOL_WIDTH = D_MODEL // 4
POOL_WINDOWS = (2, 4, 8, 16)
POOL_GROUP = POOL_WIDTH // len(POOL_WINDOWS)
MIX_WIDTH = ATTN_WIDTH + CONV_WIDTH + POOL_WIDTH
IN_WIDTH = ATTN_WIDTH + 2 * KV_WIDTH + 3 * CONV_WIDTH + POOL_WIDTH
D_FF = -(-8 * D_MODEL // (3 * 256)) * 256
ROPE_HALF = HEAD_DIM // 2
ROPE_THETA = 10000.0
Q_BLOCK = 128
EPS = 1e-6

kernel_name = "hybrid_parallel_bidir_encoder"


def rms_norm(x, g):
    x32 = x.astype(jnp.float32)
    y = x32 * lax.rsqrt(jnp.mean(x32 * x32, axis=-1, keepdims=True) + EPS)
    return (y * g.astype(jnp.float32)).astype(x.dtype)


def axial_angles(n):
    rows = n // GRID_W
    r = jnp.repeat(jnp.arange(rows, dtype=jnp.float32), GRID_W)
    c = jnp.tile(jnp.arange(GRID_W, dtype=jnp.float32), rows)
    inv = 1.0 / (ROPE_THETA ** (jnp.arange(0, ROPE_HALF, 2, dtype=jnp.float32) / ROPE_HALF))
    return r[:, None] * inv, c[:, None] * inv


def rotate_half_block(x, ang):
    f = ang.shape[1]
    shape = (1, ang.shape[0]) + (1,) * (x.ndim - 3) + (f,)
    cos = jnp.cos(ang).reshape(shape).astype(x.dtype)
    sin = jnp.sin(ang).reshape(shape).astype(x.dtype)
    x1, x2 = x[..., :f], x[..., f:]
    return jnp.concatenate([x1 * cos - x2 * sin, x2 * cos + x1 * sin], axis=-1)


def axial_rope(x, ang_r, ang_c):
    return jnp.concatenate([rotate_half_block(x[..., :ROPE_HALF], ang_r),
                            rotate_half_block(x[..., ROPE_HALF:], ang_c)], axis=-1)


def attention_mixer(q, k, v, q_gain, k_gain):
    b, s, _ = q.shape
    q = rms_norm(q.reshape(b, s, N_KV_HEADS, Q_PER_KV, HEAD_DIM), q_gain)
    k = rms_norm(k.reshape(b, s, N_KV_HEADS, HEAD_DIM), k_gain)
    v = v.reshape(b, s, N_KV_HEADS, HEAD_DIM)
    ang_r, ang_c = axial_angles(s)
    q = axial_rope(q, ang_r, ang_c)
    k = axial_rope(k, ang_r, ang_c)
    scale = HEAD_DIM ** -0.5
    nb = s // Q_BLOCK
    qb = q.reshape(b, nb, Q_BLOCK, N_KV_HEADS, Q_PER_KV, HEAD_DIM).transpose(1, 0, 2, 3, 4, 5)

    def one_block(qi):
        sc = jnp.einsum('bqkgd,bskd->bkgqs', qi, k, preferred_element_type=jnp.float32) * scale
        p = jax.nn.softmax(sc, axis=-1).astype(v.dtype)
        return jnp.einsum('bkgqs,bskd->bqkgd', p, v, preferred_element_type=jnp.float32).astype(v.dtype)

    o = lax.map(one_block, qb)
    return o.transpose(1, 0, 2, 3, 4, 5).reshape(b, s, ATTN_WIDTH)


def conv_mixer(h, gate_b, gate_c, conv_w):
    u = gate_c * h
    up = jnp.pad(u, ((0, 0), (1, 1), (0, 0)))
    y = up[:, :-2] * conv_w[0] + up[:, 1:-1] * conv_w[1] + up[:, 2:] * conv_w[2]
    return gate_b * y


def pool_mixer(u, pool_w, pool_scale):
    b, s, _ = u.shape
    u32 = u.astype(jnp.float32)
    cs = jnp.concatenate([jnp.zeros((b, 1, POOL_WIDTH), jnp.float32), lax.cumsum(u32, axis=1)], axis=1)
    t = jnp.arange(s)
    outs = []
    for gi, w in enumerate(POOL_WINDOWS):
        lo = jnp.clip(t - w // 2, 0, s)
        hi = jnp.clip(t + w // 2, 0, s)
        sl = slice(gi * POOL_GROUP, (gi + 1) * POOL_GROUP)
        csg = cs[..., sl]
        tot = jnp.take(csg, hi, axis=1) - jnp.take(csg, lo, axis=1)
        cnt = (hi - lo).astype(jnp.float32)[None, :, None]
        outs.append(tot / cnt - u32[..., sl])
    m = jnp.stack(outs, axis=2).astype(u.dtype)
    y = jnp.einsum('bsgc,gcd->bsgd', m, pool_w).reshape(b, s, POOL_WIDTH)
    return y * pool_scale


def encoder_layer(x, ln_mix_pre, ln_mix_post, ln_ffn_pre, ln_ffn_post, q_norm, k_norm,
                  w_in, conv_w, pool_w, pool_scale, w_out, w_gate, w_up, w_down):
    h = rms_norm(x, ln_mix_pre)
    proj = h @ w_in
    o = 0
    q = proj[..., o:o + ATTN_WIDTH]; o += ATTN_WIDTH
    k = proj[..., o:o + KV_WIDTH]; o += KV_WIDTH
    v = proj[..., o:o + KV_WIDTH]; o += KV_WIDTH
    ch = proj[..., o:o + CONV_WIDTH]; o += CONV_WIDTH
    cb = proj[..., o:o + CONV_WIDTH]; o += CONV_WIDTH
    cc = proj[..., o:o + CONV_WIDTH]; o += CONV_WIDTH
    pu = proj[..., o:o + POOL_WIDTH]
    a_out = attention_mixer(q, k, v, q_norm, k_norm)
    c_out = conv_mixer(ch, cb, cc, conv_w)
    p_out = pool_mixer(pu, pool_w, pool_scale)
    mix = jnp.concatenate([a_out, c_out, p_out], axis=-1) @ w_out
    x = x + rms_norm(mix, ln_mix_post)
    h = rms_norm(x, ln_ffn_pre)
    f = (jax.nn.silu(h @ w_gate) * (h @ w_up)) @ w_down
    return x + rms_norm(f, ln_ffn_post)


def trunk(x, ln_mix_pre, ln_mix_post, ln_ffn_pre, ln_ffn_post, q_norm, k_norm,
          w_in, conv_w, pool_w, pool_scale, w_out, w_gate, w_up, w_down):
    for l in range(DEPTH):
        x = encoder_layer(x, ln_mix_pre[l], ln_mix_post[l], ln_ffn_pre[l], ln_ffn_post[l],
                          q_norm[l], k_norm[l], w_in[l], conv_w[l], pool_w[l], pool_scale[l],
                          w_out[l], w_gate[l], w_up[l], w_down[l])
    return x


def setup_inputs(seed: int = 0) -> dict:
    key = jax.random.key(seed)
    ks = jax.random.split(key, 18)
    f32 = jnp.float32

    def gain(k, n):
        return 1.0 + 0.05 * jax.random.normal(k, (DEPTH, n), f32)

    return {
        "x_prompt": jax.random.normal(ks[0], (BATCH, SEQ, D_MODEL), f32),
        "x_sample": jax.random.normal(ks[1], (DEC_BATCH, DEC_SEQ, D_MODEL), f32),
        "ln_mix_pre": gain(ks[2], D_MODEL),
        "ln_mix_post": gain(ks[3], D_MODEL),
        "ln_ffn_pre": gain(ks[4], D_MODEL),
        "ln_ffn_post": gain(ks[5], D_MODEL),
        "q_norm": gain(ks[6], HEAD_DIM),
        "k_norm": gain(ks[7], HEAD_DIM),
        "w_in": jax.random.normal(ks[8], (DEPTH, D_MODEL, IN_WIDTH), f32) * D_MODEL ** -0.5,
        "conv_w": jax.random.normal(ks[9], (DEPTH, CONV_K, CONV_WIDTH), f32) * CONV_K ** -0.5,
        "pool_w": jax.random.normal(ks[10], (DEPTH, len(POOL_WINDOWS), POOL_GROUP, POOL_GROUP), f32) * POOL_GROUP ** -0.5,
        "pool_scale": 1.0 + 0.1 * jax.random.normal(ks[11], (DEPTH, POOL_WIDTH), f32),
        "w_out": jax.random.normal(ks[12], (DEPTH, MIX_WIDTH, D_MODEL), f32) * MIX_WIDTH ** -0.5,
        "w_gate": jax.random.normal(ks[13], (DEPTH, D_MODEL, D_FF), f32) * D_MODEL ** -0.5,
        "w_up": jax.random.normal(ks[14], (DEPTH, D_MODEL, D_FF), f32) * D_MODEL ** -0.5,
        "w_down": jax.random.normal(ks[15], (DEPTH, D_FF, D_MODEL), f32) * D_FF ** -0.5,
    }


def reference(x_prompt, x_sample, ln_mix_pre, ln_mix_post, ln_ffn_pre, ln_ffn_post, q_norm, k_norm,
              w_in, conv_w, pool_w, pool_scale, w_out, w_gate, w_up, w_down):
    y_prompt = trunk(x_prompt, ln_mix_pre, ln_mix_post, ln_ffn_pre, ln_ffn_post, q_norm, k_norm,
                     w_in, conv_w, pool_w, pool_scale, w_out, w_gate, w_up, w_down)
    y_sample = trunk(x_sample, ln_mix_pre, ln_mix_post, ln_ffn_pre, ln_ffn_post, q_norm, k_norm,
                     w_in, conv_w, pool_w, pool_scale, w_out, w_gate, w_up, w_down)
    return (y_prompt, y_sample)
```

```python
import functools

import jax
import jax.numpy as jnp
from jax import lax
from jax.experimental import pallas as pl
from jax.experimental.pallas import tpu as pltpu

D_MODEL = 4096
DEPTH = 2
GRID_W = 64
HEAD_DIM = 128
N_Q_HEADS = 16
N_KV_HEADS = 4
Q_PER_KV = N_Q_HEADS // N_KV_HEADS
ATTN_WIDTH = N_Q_HEADS * HEAD_DIM
KV_WIDTH = N_KV_HEADS * HEAD_DIM
CONV_WIDTH = D_MODEL // 4
POOL_WIDTH = D_MODEL // 4
POOL_WINDOWS = (2, 4, 8, 16)
POOL_GROUP = POOL_WIDTH // len(POOL_WINDOWS)
IN_WIDTH = ATTN_WIDTH + 2 * KV_WIDTH + 3 * CONV_WIDTH + POOL_WIDTH
D_FF = 11008
D_FF_PAD = 11264
ROPE_HALF = HEAD_DIM // 2
ROPE_THETA = 10000.0
EPS = 1e-6

Q_OFF = 0
K_OFF = ATTN_WIDTH
V_OFF = K_OFF + KV_WIDTH
CH_OFF = V_OFF + KV_WIDTH
CB_OFF = CH_OFF + CONV_WIDTH
CC_OFF = CB_OFF + CONV_WIDTH
PU_OFF = CC_OFF + CONV_WIDTH

HALO = 16
V7X_VMEM_BYTES = 64 * 1024 * 1024

F32 = jnp.float32
BF16 = jnp.bfloat16


def _params(semantics, vmem_mb):
    assert vmem_mb * 1024 * 1024 < V7X_VMEM_BYTES
    return pltpu.CompilerParams(dimension_semantics=semantics,
                                vmem_limit_bytes=vmem_mb * 1024 * 1024)


def _rms(x):
    return x * lax.rsqrt(jnp.mean(x * x, axis=-1, keepdims=True) + EPS)


def _norm_cast_kernel(x_ref, g_ref, h_ref):
    h_ref[...] = (_rms(x_ref[...]) * g_ref[...]).astype(h_ref.dtype)


def norm_cast(x, g, rows=256):
    t, d = x.shape
    return pl.pallas_call(
        _norm_cast_kernel,
        grid=(t // rows,),
        in_specs=[pl.BlockSpec((rows, d), lambda i: (i, 0)),
                  pl.BlockSpec((1, d), lambda i: (0, 0))],
        out_specs=pl.BlockSpec((rows, d), lambda i: (i, 0)),
        out_shape=jax.ShapeDtypeStruct((t, d), BF16),
        compiler_params=_params(("parallel",), 32),
        name="norm_cast",
    )(x, g.reshape(1, d))


def _resnorm_kernel(f_ref, x_ref, g_post_ref, g_next_ref, xo_ref, h_ref):
    xn = x_ref[...] + _rms(f_ref[...]) * g_post_ref[...]
    xo_ref[...] = xn
    h_ref[...] = (_rms(xn) * g_next_ref[...]).astype(h_ref.dtype)


def _resnorm_last_kernel(f_ref, x_ref, g_post_ref, xo_ref):
    xo_ref[...] = x_ref[...] + _rms(f_ref[...]) * g_post_ref[...]


def resnorm(f, x, g_post, g_next, rows=256):
    t, d = x.shape
    row_spec = pl.BlockSpec((rows, d), lambda i: (i, 0))
    g_spec = pl.BlockSpec((1, d), lambda i: (0, 0))
    if g_next is None:
        return pl.pallas_call(
            _resnorm_last_kernel,
            grid=(t // rows,),
            in_specs=[row_spec, row_spec, g_spec],
            out_specs=row_spec,
            out_shape=jax.ShapeDtypeStruct((t, d), F32),
            compiler_params=_params(("parallel",), 40),
            name="resnorm_last",
        )(f, x, g_post.reshape(1, d)), None
    return pl.pallas_call(
        _resnorm_kernel,
        grid=(t // rows,),
        in_specs=[row_spec, row_spec, g_spec, g_spec],
        out_specs=[row_spec, row_spec],
        out_shape=[jax.ShapeDtypeStruct((t, d), F32), jax.ShapeDtypeStruct((t, d), BF16)],
        compiler_params=_params(("parallel",), 40),
        name="resnorm",
    )(f, x, g_post.reshape(1, d), g_next.reshape(1, d))


def _mm_kernel(a_ref, b_ref, o_ref):
    o_ref[...] = jnp.dot(a_ref[...], b_ref[...], preferred_element_type=F32).astype(o_ref.dtype)


def matmul(a, b, out_dtype, bm=1024, bn=1024):
    m, k = a.shape
    _, n = b.shape
    bm = min(bm, m)
    return pl.pallas_call(
        _mm_kernel,
        grid=(m // bm, n // bn),
        in_specs=[pl.BlockSpec((bm, k), lambda i, j: (i, 0)),
                  pl.BlockSpec((k, bn), lambda i, j: (0, j))],
        out_specs=pl.BlockSpec((bm, bn), lambda i, j: (i, j)),
        out_shape=jax.ShapeDtypeStruct((m, n), out_dtype),
        compiler_params=_params(("parallel", "parallel"), 48),
        name="matmul_full_k",
    )(a, b)


def _mm2_kernel(a1_ref, a2_ref, b1_ref, b2_ref, o_ref):
    o_ref[...] = (jnp.dot(a1_ref[...], b1_ref[...], preferred_element_type=F32)
                  + jnp.dot(a2_ref[...], b2_ref[...], preferred_element_type=F32))


def matmul_two_lhs(a1, a2, b, bm=1024, bn=1024):
    m, k1 = a1.shape
    _, k2 = a2.shape
    assert k1 == k2 and b.shape[0] == k1 + k2
    n = b.shape[1]
    bm = min(bm, m)
    return pl.pallas_call(
        _mm2_kernel,
        grid=(m // bm, n // bn),
        in_specs=[pl.BlockSpec((bm, k1), lambda i, j: (i, 0)),
                  pl.BlockSpec((bm, k2), lambda i, j: (i, 0)),
                  pl.BlockSpec((k1, bn), lambda i, j: (0, j)),
                  pl.BlockSpec((k2, bn), lambda i, j: (1, j))],
        out_specs=pl.BlockSpec((bm, bn), lambda i, j: (i, j)),
        out_shape=jax.ShapeDtypeStruct((m, n), F32),
        compiler_params=_params(("parallel", "parallel"), 48),
        name="matmul_two_lhs",
    )(a1, a2, b, b)


def _mm_acc_kernel(a_ref, b_ref, o_ref):
    part = jnp.dot(a_ref[...], b_ref[...], preferred_element_type=F32)

    @pl.when(pl.program_id(2) == 0)
    def _():
        o_ref[...] = part

    @pl.when(pl.program_id(2) != 0)
    def _():
        o_ref[...] += part


def matmul_split_k(a, b, bk, bm=1024, bn=1024):
    m, k = a.shape
    _, n = b.shape
    bm = min(bm, m)
    return pl.pallas_call(
        _mm_acc_kernel,
        grid=(m // bm, n // bn, k // bk),
        in_specs=[pl.BlockSpec((bm, bk), lambda i, j, kk: (i, kk)),
                  pl.BlockSpec((bk, bn), lambda i, j, kk: (kk, j))],
        out_specs=pl.BlockSpec((bm, bn), lambda i, j, kk: (i, j)),
        out_shape=jax.ShapeDtypeStruct((m, n), F32),
        compiler_params=_params(("parallel", "parallel", "arbitrary"), 48),
        name="matmul_split_k",
    )(a, b)


def _gateup_kernel(h_ref, wg_ref, wu_ref, o_ref):
    h = h_ref[...]
    g = jnp.dot(h, wg_ref[...], preferred_element_type=F32)
    u = jnp.dot(h, wu_ref[...], preferred_element_type=F32)
    o_ref[...] = (g * jax.nn.sigmoid(g) * u).astype(o_ref.dtype)


def gate_up(h, wg, wu, bm=1024, bn=512):
    m, k = h.shape
    _, n = wg.shape
    bm = min(bm, m)
    return pl.pallas_call(
        _gateup_kernel,
        grid=(m // bm, n // bn),
        in_specs=[pl.BlockSpec((bm, k), lambda i, j: (i, 0)),
                  pl.BlockSpec((k, bn), lambda i, j: (0, j)),
                  pl.BlockSpec((k, bn), lambda i, j: (0, j))],
        out_specs=pl.BlockSpec((bm, bn), lambda i, j: (i, j)),
        out_shape=jax.ShapeDtypeStruct((m, n), BF16),
        compiler_params=_params(("parallel", "parallel"), 48),
        name="gate_up",
    )(h, wg, wu)


def _rope_tables(segments):
    pos = jnp.concatenate([jnp.arange(n, dtype=jnp.int32) for _, n in segments])
    r = (pos // GRID_W).astype(F32)
    c = (pos % GRID_W).astype(F32)
    inv = 1.0 / (ROPE_THETA ** (jnp.arange(0, ROPE_HALF, 2, dtype=F32) / ROPE_HALF))
    ang_r = r[:, None] * inv
    ang_c = c[:, None] * inv
    cos = jnp.concatenate([jnp.cos(ang_r), jnp.cos(ang_r), jnp.cos(ang_c), jnp.cos(ang_c)], axis=-1)
    sin = jnp.concatenate([-jnp.sin(ang_r), jnp.sin(ang_r), -jnp.sin(ang_c), jnp.sin(ang_c)], axis=-1)
    return cos, sin


def _norm_rope(x, gain, cos, sin, upper):
    xn = _rms(x) * gain
    quarter = ROPE_HALF // 2
    partner = jnp.where(upper, pltpu.roll(xn, quarter, axis=1), pltpu.roll(xn, HEAD_DIM - quarter, axis=1))
    return xn * cos + partner * sin


def _qk_prep_kernel(q_ref, k_ref, cos_ref, sin_ref, qg_ref, kg_ref, qo_ref, ko_ref):
    cos = cos_ref[...]
    sin = sin_ref[...]
    lane = lax.broadcasted_iota(jnp.int32, cos.shape, 1)
    upper = (lane & (ROPE_HALF // 2)) != 0
    scale = HEAD_DIM ** -0.5
    qg = qg_ref[...]
    kg = kg_ref[...]
    for h in range(N_Q_HEADS):
        sl = slice(h * HEAD_DIM, (h + 1) * HEAD_DIM)
        y = _norm_rope(q_ref[:, sl].astype(F32), qg, cos, sin, upper)
        qo_ref[:, sl] = (y * scale).astype(qo_ref.dtype)
    for h in range(N_KV_HEADS):
        sl = slice(h * HEAD_DIM, (h + 1) * HEAD_DIM)
        y = _norm_rope(k_ref[:, sl].astype(F32), kg, cos, sin, upper)
        ko_ref[:, sl] = y.astype(ko_ref.dtype)


def qk_prep(proj, cos, sin, q_gain, k_gain, rows=256):
    t = proj.shape[0]
    rows = min(rows, t)
    return pl.pallas_call(
        _qk_prep_kernel,
        grid=(t // rows,),
        in_specs=[pl.BlockSpec((rows, ATTN_WIDTH), lambda i: (i, Q_OFF // ATTN_WIDTH)),
                  pl.BlockSpec((rows, KV_WIDTH), lambda i: (i, K_OFF // KV_WIDTH)),
                  pl.BlockSpec((rows, HEAD_DIM), lambda i: (i, 0)),
                  pl.BlockSpec((rows, HEAD_DIM), lambda i: (i, 0)),
                  pl.BlockSpec((1, HEAD_DIM), lambda i: (0, 0)),
                  pl.BlockSpec((1, HEAD_DIM), lambda i: (0, 0))],
        out_specs=[pl.BlockSpec((rows, ATTN_WIDTH), lambda i: (i, 0)),
                   pl.BlockSpec((rows, KV_WIDTH), lambda i: (i, 0))],
        out_shape=[jax.ShapeDtypeStruct((t, ATTN_WIDTH), BF16),
                   jax.ShapeDtypeStruct((t, KV_WIDTH), BF16)],
        compiler_params=_params(("parallel",), 32),
        name="qk_prep",
    )(proj, proj, cos, sin, q_gain.reshape(1, HEAD_DIM), k_gain.reshape(1, HEAD_DIM))


def _attn_kernel(q_ref, k_ref, v_ref, o_ref, m_scr, l_scr, acc_scr, *, bk, seq):
    m_scr[...] = jnp.full(m_scr.shape, -jnp.inf, F32)
    l_scr[...] = jnp.zeros(l_scr.shape, F32)
    acc_scr[...] = jnp.zeros(acc_scr.shape, F32)

    def body(j, carry):
        start = pl.multiple_of(j * bk, bk)
        kb = k_ref[pl.ds(start, bk), :]
        vb = v_ref[pl.ds(start, bk), :]
        for h in range(Q_PER_KV):
            q = q_ref[:, h * HEAD_DIM:(h + 1) * HEAD_DIM]
            s = lax.dot_general(q, kb, (((1,), (1,)), ((), ())), preferred_element_type=F32)
            m_prev = m_scr[h]
            m_new = jnp.maximum(m_prev, jnp.max(s, axis=1, keepdims=True))
            alpha = jnp.exp(m_prev - m_new)
            p = jnp.exp(s - pltpu.repeat(m_new, bk // HEAD_DIM, axis=1))
            l_scr[h] = alpha * l_scr[h] + jnp.sum(p, axis=1, keepdims=True)
            acc_scr[h] = alpha * acc_scr[h] + jnp.dot(p.astype(vb.dtype), vb, preferred_element_type=F32)
            m_scr[h] = m_new
        return carry

    lax.fori_loop(0, seq // bk, body, 0)
    for h in range(Q_PER_KV):
        o_ref[:, h * HEAD_DIM:(h + 1) * HEAD_DIM] = (acc_scr[h] / l_scr[h]).astype(o_ref.dtype)


def attention(q_rot, k_rot, proj, prev_out, row0, batch, seq, bq=512, bk=512):
    t = q_rot.shape[0]
    bq = min(bq, seq)
    bk = min(bk, seq)
    assert row0 % seq == 0 and seq % bq == 0 and seq % bk == 0
    group_w = Q_PER_KV * HEAD_DIM
    nq = seq // bq
    in_specs = [
        pl.BlockSpec((bq, group_w), lambda b, h, i: (row0 // bq + b * nq + i, h)),
        pl.BlockSpec((seq, HEAD_DIM), lambda b, h, i: (row0 // seq + b, h)),
        pl.BlockSpec((seq, HEAD_DIM), lambda b, h, i: (row0 // seq + b, V_OFF // HEAD_DIM + h)),
    ]
    args = [q_rot, k_rot, proj]
    aliases = {}
    kernel = functools.partial(_attn_kernel, bk=bk, seq=seq)
    if prev_out is not None:
        in_specs.append(pl.BlockSpec(memory_space=pl.ANY))
        args.append(prev_out)
        aliases = {3: 0}
        kernel = functools.partial(_attn_alias_kernel, bk=bk, seq=seq)
    return pl.pallas_call(
        kernel,
        grid=(batch, N_KV_HEADS, nq),
        in_specs=in_specs,
        out_specs=pl.BlockSpec((bq, group_w), lambda b, h, i: (row0 // bq + b * nq + i, h)),
        out_shape=jax.ShapeDtypeStruct((t, ATTN_WIDTH), BF16),
        scratch_shapes=[pltpu.VMEM((Q_PER_KV, bq, HEAD_DIM), F32),
                        pltpu.VMEM((Q_PER_KV, bq, HEAD_DIM), F32),
                        pltpu.VMEM((Q_PER_KV, bq, HEAD_DIM), F32)],
        input_output_aliases=aliases,
        compiler_params=_params(("parallel", "parallel", "parallel"), 48),
        name="attention",
    )(*args)


def _attn_alias_kernel(q_ref, k_ref, v_ref, prev_ref, o_ref, m_scr, l_scr, acc_scr, *, bk, seq):
    del prev_ref
    _attn_kernel(q_ref, k_ref, v_ref, o_ref, m_scr, l_scr, acc_scr, bk=bk, seq=seq)


def _convpool_kernel(ch_ref, cb_ref, cc_ref, pu_ref,
                     chp_ref, ccp_ref, pup_ref, chn_ref, ccn_ref, pun_ref,
                     cw_ref, pw_ref, ps_ref, o_ref, *, rows, segments):
    i = pl.program_id(0)
    start = i * rows
    pos0 = start
    seq_len = jnp.int32(0)
    has_prev = jnp.bool_(True)
    has_next = jnp.bool_(True)
    for seg_start, seg_len in segments:
        inside = (start >= seg_start) & (start < seg_start + seg_len)
        pos0 = jnp.where(inside, start - seg_start, pos0)
        seq_len = jnp.where(inside, seg_len, seq_len)
        has_prev = has_prev & (start != seg_start)
        has_next = has_next & (start + rows != seg_start + seg_len)
    prev_on = has_prev.astype(F32)
    next_on = has_next.astype(F32)

    def extended(cur_ref, prev_ref, next_ref):
        return jnp.concatenate([prev_ref[...].astype(F32) * prev_on,
                                cur_ref[...].astype(F32),
                                next_ref[...].astype(F32) * next_on], axis=0)

    ext_rows = rows + 2 * HALO

    def shifted(x, d):
        return pltpu.roll(x, d % ext_rows, axis=0)

    u = extended(cc_ref, ccp_ref, ccn_ref) * extended(ch_ref, chp_ref, chn_ref)
    cw = cw_ref[...]
    y = shifted(u, 1) * cw[0:1, :] + u * cw[1:2, :] + shifted(u, -1) * cw[2:3, :]
    o_ref[:, :CONV_WIDTH] = (cb_ref[...].astype(F32) * y[HALO:HALO + rows, :]).astype(o_ref.dtype)

    x = extended(pu_ref, pup_ref, pun_ref)
    pos = (pos0 + lax.broadcasted_iota(jnp.int32, (rows, 1), 0))
    for gi, w in enumerate(POOL_WINDOWS):
        sl = slice(gi * POOL_GROUP, (gi + 1) * POOL_GROUP)
        xg = x[:, sl]
        tot = xg + shifted(xg, 1)
        half = 1
        while 2 * half < w:
            tot = shifted(tot, half) + shifted(tot, -half)
            half *= 2
        hi = jnp.minimum(pos + w // 2, seq_len)
        lo = jnp.maximum(pos - w // 2, 0)
        cnt = (hi - lo).astype(F32)
        m = tot[HALO:HALO + rows, :] / cnt - xg[HALO:HALO + rows, :]
        yg = jnp.dot(m.astype(BF16), pw_ref[gi], preferred_element_type=F32)
        o_ref[:, CONV_WIDTH + gi * POOL_GROUP:CONV_WIDTH + (gi + 1) * POOL_GROUP] = (
            yg * ps_ref[:, sl]).astype(o_ref.dtype)


def conv_pool(proj, conv_w, pool_w, pool_scale, segments, rows=512):
    t = proj.shape[0]
    rows = min(rows, min(n for _, n in segments))
    assert all(s % rows == 0 and n % rows == 0 for s, n in segments)
    w = CONV_WIDTH
    per = rows // HALO
    last = t // HALO - 1

    def cur(off):
        return pl.BlockSpec((rows, w), lambda i: (i, off // w))

    def prev(off):
        return pl.BlockSpec((HALO, w), lambda i: (jnp.maximum(i * per - 1, 0), off // w))

    def nxt(off):
        return pl.BlockSpec((HALO, w), lambda i: (jnp.minimum((i + 1) * per, last), off // w))

    kernel = functools.partial(_convpool_kernel, rows=rows, segments=segments)
    return pl.pallas_call(
        kernel,
        grid=(t // rows,),
        in_specs=[cur(CH_OFF), cur(CB_OFF), cur(CC_OFF), cur(PU_OFF),
                  prev(CH_OFF), prev(CC_OFF), prev(PU_OFF),
                  nxt(CH_OFF), nxt(CC_OFF), nxt(PU_OFF),
                  pl.BlockSpec((3, w), lambda i: (0, 0)),
                  pl.BlockSpec((len(POOL_WINDOWS), POOL_GROUP, POOL_GROUP), lambda i: (0, 0, 0)),
                  pl.BlockSpec((1, POOL_WIDTH), lambda i: (0, 0))],
        out_specs=pl.BlockSpec((rows, CONV_WIDTH + POOL_WIDTH), lambda i: (i, 0)),
        out_shape=jax.ShapeDtypeStruct((t, CONV_WIDTH + POOL_WIDTH), BF16),
        compiler_params=_params(("parallel",), 48),
        name="conv_pool",
    )(proj, proj, proj, proj, proj, proj, proj, proj, proj, proj,
      conv_w, pool_w, pool_scale.reshape(1, POOL_WIDTH))


def _trunk(x, segments, attn_calls, ln_mix_pre, ln_mix_post, ln_ffn_pre, ln_ffn_post, q_norm, k_norm,
           w_in, conv_w, pool_w, pool_scale, w_out, w_gate, w_up, w_down):
    cos, sin = _rope_tables(segments)
    pad = D_FF_PAD - D_FF
    h = norm_cast(x, ln_mix_pre[0])
    for l in range(DEPTH):
        proj = matmul(h, w_in[l].astype(BF16), BF16)
        q_rot, k_rot = qk_prep(proj, cos, sin, q_norm[l], k_norm[l])
        a_out = None
        for row0, batch, seq in attn_calls:
            a_out = attention(q_rot, k_rot, proj, a_out, row0, batch, seq)
        cp_out = conv_pool(proj, conv_w[l], pool_w[l].astype(BF16), pool_scale[l], segments)
        mix = matmul_two_lhs(a_out, cp_out, w_out[l].astype(BF16))
        x, h = resnorm(mix, x, ln_mix_post[l], ln_ffn_pre[l])
        wg = jnp.pad(w_gate[l].astype(BF16), ((0, 0), (0, pad)))
        wu = jnp.pad(w_up[l].astype(BF16), ((0, 0), (0, pad)))
        wd = jnp.pad(w_down[l].astype(BF16), ((0, pad), (0, 0)))
        act = gate_up(h, wg, wu)
        f = matmul_split_k(act, wd, bk=D_FF_PAD // 4)
        x, h = resnorm(f, x, ln_ffn_post[l], ln_mix_pre[l + 1] if l + 1 < DEPTH else None)
    return x


def kernel(x_prompt, x_sample, ln_mix_pre, ln_mix_post, ln_ffn_pre, ln_ffn_post, q_norm, k_norm,
           w_in, conv_w, pool_w, pool_scale, w_out, w_gate, w_up, w_down):
    pb, ps, d = x_prompt.shape
    sb, ss, _ = x_sample.shape
    x = jnp.concatenate([x_prompt.reshape(pb * ps, d), x_sample.reshape(sb * ss, d)], axis=0)
    segments = tuple((b * ps, ps) for b in range(pb)) + tuple((pb * ps + b * ss, ss) for b in range(sb))
    attn_calls = ((0, pb, ps), (pb * ps, sb, ss))
    y = _trunk(x, segments, attn_calls, ln_mix_pre, ln_mix_post, ln_ffn_pre, ln_ffn_post, q_norm, k_norm,
               w_in, conv_w, pool_w, pool_scale, w_out, w_gate, w_up, w_down)
    return (y[:pb * ps].reshape(pb, ps, d), y[pb * ps:].reshape(sb, ss, d))
```

```python
import functools

import jax
import jax.numpy as jnp
from jax import lax
from jax.experimental import pallas as pl
from jax.experimental.pallas import tpu as pltpu

D_MODEL = 4096
DEPTH = 2
GRID_W = 64
HEAD_DIM = 128
N_Q_HEADS = 16
N_KV_HEADS = 4
Q_PER_KV = N_Q_HEADS // N_KV_HEADS
ATTN_WIDTH = N_Q_HEADS * HEAD_DIM
KV_WIDTH = N_KV_HEADS * HEAD_DIM
CONV_WIDTH = D_MODEL // 4
POOL_WIDTH = D_MODEL // 4
POOL_WINDOWS = (2, 4, 8, 16)
POOL_GROUP = POOL_WIDTH // len(POOL_WINDOWS)
IN_WIDTH = ATTN_WIDTH + 2 * KV_WIDTH + 3 * CONV_WIDTH + POOL_WIDTH
D_FF = 11008
D_FF_PAD = 11264
ROPE_HALF = HEAD_DIM // 2
ROPE_THETA = 10000.0
EPS = 1e-6

Q_OFF = 0
K_OFF = ATTN_WIDTH
V_OFF = K_OFF + KV_WIDTH
CH_OFF = V_OFF + KV_WIDTH
CB_OFF = CH_OFF + CONV_WIDTH
CC_OFF = CB_OFF + CONV_WIDTH
PU_OFF = CC_OFF + CONV_WIDTH

HALO = 16
V7X_VMEM_BYTES = 64 * 1024 * 1024

F32 = jnp.float32
BF16 = jnp.bfloat16


def _params(semantics, vmem_mb):
    assert vmem_mb * 1024 * 1024 < V7X_VMEM_BYTES
    return pltpu.CompilerParams(dimension_semantics=semantics,
                                vmem_limit_bytes=vmem_mb * 1024 * 1024)


def _rms(x):
    return x * lax.rsqrt(jnp.mean(x * x, axis=-1, keepdims=True) + EPS)


def _norm_cast_kernel(x_ref, g_ref, h_ref):
    h_ref[...] = (_rms(x_ref[...]) * g_ref[...]).astype(h_ref.dtype)


def norm_cast(x, g, rows=256):
    t, d = x.shape
    return pl.pallas_call(
        _norm_cast_kernel,
        grid=(t // rows,),
        in_specs=[pl.BlockSpec((rows, d), lambda i: (i, 0)),
                  pl.BlockSpec((1, d), lambda i: (0, 0))],
        out_specs=pl.BlockSpec((rows, d), lambda i: (i, 0)),
        out_shape=jax.ShapeDtypeStruct((t, d), BF16),
        compiler_params=_params(("parallel",), 32),
        name="norm_cast",
    )(x, g.reshape(1, d))


def _resnorm_kernel(f_ref, x_ref, g_post_ref, g_next_ref, xo_ref, h_ref):
    xn = x_ref[...] + _rms(f_ref[...]) * g_post_ref[...]
    xo_ref[...] = xn
    h_ref[...] = (_rms(xn) * g_next_ref[...]).astype(h_ref.dtype)


def _resnorm_last_kernel(f_ref, x_ref, g_post_ref, xo_ref):
    xo_ref[...] = x_ref[...] + _rms(f_ref[...]) * g_post_ref[...]


def resnorm(f, x, g_post, g_next, rows=256):
    t, d = x.shape
    row_spec = pl.BlockSpec((rows, d), lambda i: (i, 0))
    g_spec = pl.BlockSpec((1, d), lambda i: (0, 0))
    if g_next is None:
        return pl.pallas_call(
            _resnorm_last_kernel,
            grid=(t // rows,),
            in_specs=[row_spec, row_spec, g_spec],
            out_specs=row_spec,
            out_shape=jax.ShapeDtypeStruct((t, d), F32),
            compiler_params=_params(("parallel",), 40),
            name="resnorm_last",
        )(f, x, g_post.reshape(1, d)), None
    return pl.pallas_call(
        _resnorm_kernel,
        grid=(t // rows,),
        in_specs=[row_spec, row_spec, g_spec, g_spec],
        out_specs=[row_spec, row_spec],
        out_shape=[jax.ShapeDtypeStruct((t, d), F32), jax.ShapeDtypeStruct((t, d), BF16)],
        compiler_params=_params(("parallel",), 40),
        name="resnorm",
    )(f, x, g_post.reshape(1, d), g_next.reshape(1, d))


def _mm_kernel(a_ref, b_ref, o_ref):
    o_ref[...] = jnp.dot(a_ref[...], b_ref[...], preferred_element_type=F32).astype(o_ref.dtype)


def matmul(a, b, out_dtype, bm=1024, bn=1024):
    m, k = a.shape
    _, n = b.shape
    bm = min(bm, m)
    return pl.pallas_call(
        _mm_kernel,
        grid=(m // bm, n // bn),
        in_specs=[pl.BlockSpec((bm, k), lambda i, j: (i, 0)),
                  pl.BlockSpec((k, bn), lambda i, j: (0, j))],
        out_specs=pl.BlockSpec((bm, bn), lambda i, j: (i, j)),
        out_shape=jax.ShapeDtypeStruct((m, n), out_dtype),
        compiler_params=_params(("parallel", "parallel"), 48),
        name="matmul_full_k",
    )(a, b)


def _mm2_kernel(a1_ref, a2_ref, b1_ref, b2_ref, o_ref):
    o_ref[...] = (jnp.dot(a1_ref[...], b1_ref[...], preferred_element_type=F32)
                  + jnp.dot(a2_ref[...], b2_ref[...], preferred_element_type=F32))


def matmul_two_lhs(a1, a2, b, bm=1024, bn=1024):
    m, k1 = a1.shape
    _, k2 = a2.shape
    assert k1 == k2 and b.shape[0] == k1 + k2
    n = b.shape[1]
    bm = min(bm, m)
    return pl.pallas_call(
        _mm2_kernel,
        grid=(m // bm, n // bn),
        in_specs=[pl.BlockSpec((bm, k1), lambda i, j: (i, 0)),
                  pl.BlockSpec((bm, k2), lambda i, j: (i, 0)),
                  pl.BlockSpec((k1, bn), lambda i, j: (0, j)),
                  pl.BlockSpec((k2, bn), lambda i, j: (1, j))],
        out_specs=pl.BlockSpec((bm, bn), lambda i, j: (i, j)),
        out_shape=jax.ShapeDtypeStruct((m, n), F32),
        compiler_params=_params(("parallel", "parallel"), 48),
        name="matmul_two_lhs",
    )(a1, a2, b, b)


def _mm_acc_kernel(a_ref, b_ref, o_ref):
    part = jnp.dot(a_ref[...], b_ref[...], preferred_element_type=F32)

    @pl.when(pl.program_id(2) == 0)
    def _():
        o_ref[...] = part

    @pl.when(pl.program_id(2) != 0)
    def _():
        o_ref[...] += part


def matmul_split_k(a, b, bk, bm=1024, bn=1024):
    m, k = a.shape
    _, n = b.shape
    bm = min(bm, m)
    return pl.pallas_call(
        _mm_acc_kernel,
        grid=(m // bm, n // bn, k // bk),
        in_specs=[pl.BlockSpec((bm, bk), lambda i, j, kk: (i, kk)),
                  pl.BlockSpec((bk, bn), lambda i, j, kk: (kk, j))],
        out_specs=pl.BlockSpec((bm, bn), lambda i, j, kk: (i, j)),
        out_shape=jax.ShapeDtypeStruct((m, n), F32),
        compiler_params=_params(("parallel", "parallel", "arbitrary"), 48),
        name="matmul_split_k",
    )(a, b)


def _gateup_kernel(h_ref, wg_ref, wu_ref, o_ref):
    h = h_ref[...]
    g = jnp.dot(h, wg_ref[...], preferred_element_type=F32)
    u = jnp.dot(h, wu_ref[...], preferred_element_type=F32)
    o_ref[...] = (g * jax.nn.sigmoid(g) * u).astype(o_ref.dtype)


def gate_up(h, wg, wu, bm=1024, bn=512):
    m, k = h.shape
    _, n = wg.shape
    bm = min(bm, m)
    return pl.pallas_call(
        _gateup_kernel,
        grid=(m // bm, n // bn),
        in_specs=[pl.BlockSpec((bm, k), lambda i, j: (i, 0)),
                  pl.BlockSpec((k, bn), lambda i, j: (0, j)),
                  pl.BlockSpec((k, bn), lambda i, j: (0, j))],
        out_specs=pl.BlockSpec((bm, bn), lambda i, j: (i, j)),
        out_shape=jax.ShapeDtypeStruct((m, n), BF16),
        compiler_params=_params(("parallel", "parallel"), 48),
        name="gate_up",
    )(h, wg, wu)


V_ROWS = HEAD_DIM + 16
LOG2E = 1.4426950408889634


def _rope_tables(segments):
    pos = jnp.concatenate([jnp.arange(n, dtype=jnp.int32) for _, n in segments])
    r = (pos // GRID_W).astype(F32)
    c = (pos % GRID_W).astype(F32)
    inv = 1.0 / (ROPE_THETA ** (jnp.arange(0, ROPE_HALF, 2, dtype=F32) / ROPE_HALF))
    ang_r = r[:, None] * inv
    ang_c = c[:, None] * inv
    cos = jnp.concatenate([jnp.cos(ang_r), jnp.cos(ang_r), jnp.cos(ang_c), jnp.cos(ang_c)], axis=-1)
    sin = jnp.concatenate([-jnp.sin(ang_r), jnp.sin(ang_r), -jnp.sin(ang_c), jnp.sin(ang_c)], axis=-1)
    return cos, sin


def _norm_rope(x, gain, cos, sin, upper):
    xn = _rms(x) * gain
    quarter = ROPE_HALF // 2
    partner = jnp.where(upper, pltpu.roll(xn, quarter, axis=1), pltpu.roll(xn, HEAD_DIM - quarter, axis=1))
    return xn * cos + partner * sin


def _qkv_prep_kernel(q_ref, k_ref, v_ref, cos_ref, sin_ref, qg_ref, kg_ref, qt_ref, ko_ref, vt_ref):
    cos = cos_ref[...]
    sin = sin_ref[...]
    rows = cos.shape[0]
    lane = lax.broadcasted_iota(jnp.int32, cos.shape, 1)
    upper = (lane & (ROPE_HALF // 2)) != 0
    q_scale = HEAD_DIM ** -0.5 * LOG2E
    qg = qg_ref[...]
    kg = kg_ref[...]
    for h in range(N_Q_HEADS):
        sl = slice(h * HEAD_DIM, (h + 1) * HEAD_DIM)
        y = _norm_rope(q_ref[:, sl].astype(F32), qg, cos, sin, upper) * q_scale
        qt_ref[sl, :] = y.T.astype(qt_ref.dtype)
    for h in range(N_KV_HEADS):
        sl = slice(h * HEAD_DIM, (h + 1) * HEAD_DIM)
        y = _norm_rope(k_ref[:, sl].astype(F32), kg, cos, sin, upper)
        ko_ref[:, sl] = y.astype(ko_ref.dtype)
        vt_ref[h, 0, :HEAD_DIM, :] = v_ref[:, sl].astype(F32).T.astype(vt_ref.dtype)
        vt_ref[h, 0, HEAD_DIM:, :] = jnp.ones((V_ROWS - HEAD_DIM, rows), vt_ref.dtype)


def qkv_prep(proj, cos, sin, q_gain, k_gain, rows):
    t = proj.shape[0]
    return pl.pallas_call(
        _qkv_prep_kernel,
        grid=(t // rows,),
        in_specs=[pl.BlockSpec((rows, ATTN_WIDTH), lambda i: (i, Q_OFF // ATTN_WIDTH)),
                  pl.BlockSpec((rows, KV_WIDTH), lambda i: (i, K_OFF // KV_WIDTH)),
                  pl.BlockSpec((rows, KV_WIDTH), lambda i: (i, V_OFF // KV_WIDTH)),
                  pl.BlockSpec((rows, HEAD_DIM), lambda i: (i, 0)),
                  pl.BlockSpec((rows, HEAD_DIM), lambda i: (i, 0)),
                  pl.BlockSpec((1, HEAD_DIM), lambda i: (0, 0)),
                  pl.BlockSpec((1, HEAD_DIM), lambda i: (0, 0))],
        out_specs=[pl.BlockSpec((ATTN_WIDTH, rows), lambda i: (0, i)),
                   pl.BlockSpec((rows, KV_WIDTH), lambda i: (i, 0)),
                   pl.BlockSpec((N_KV_HEADS, 1, V_ROWS, rows), lambda i: (0, i, 0, 0))],
        out_shape=[jax.ShapeDtypeStruct((ATTN_WIDTH, t), BF16),
                   jax.ShapeDtypeStruct((t, KV_WIDTH), BF16),
                   jax.ShapeDtypeStruct((N_KV_HEADS, t // rows, V_ROWS, rows), BF16)],
        compiler_params=_params(("parallel",), 40),
        name="qkv_prep",
    )(proj, proj, proj, cos, sin, q_gain.reshape(1, HEAD_DIM), k_gain.reshape(1, HEAD_DIM))


def _attn_kernel(qt_ref, k_ref, vt_ref, *rest, bk, nk):
    o_ref, m_scr, acc_scr, s_scr, smax_scr = rest[-5:]
    m_scr[...] = jnp.full(m_scr.shape, -jnp.inf, F32)
    acc_scr[...] = jnp.zeros(acc_scr.shape, F32)

    def scores(kb, h, slot):
        s = jnp.dot(kb, qt_ref[h * HEAD_DIM:(h + 1) * HEAD_DIM, :], preferred_element_type=F32)
        s_scr[slot] = s
        smax_scr[slot] = jnp.max(s, axis=0, keepdims=True)

    def key_block(j):
        return k_ref[pl.ds(pl.multiple_of(j * bk, bk), bk), :]

    scores(key_block(0), 0, 0)

    def body(j, carry):
        kb = key_block(j)
        kb_next = key_block(jnp.minimum(j + 1, nk - 1))
        vb = vt_ref[j]
        for h in range(Q_PER_KV):
            slot = h % 2
            if h + 1 < Q_PER_KV:
                scores(kb, h + 1, 1 - slot)
            else:
                scores(kb_next, 0, 1 - slot)
            m_prev = m_scr[h]
            m_new = jnp.maximum(m_prev, smax_scr[slot])
            alpha = jnp.exp2(m_prev - m_new)
            p = jnp.exp2(s_scr[slot] - m_new).astype(vb.dtype)
            acc_scr[h] = acc_scr[h] * alpha + jnp.dot(vb, p, preferred_element_type=F32)
            m_scr[h] = m_new
        return carry

    lax.fori_loop(0, nk, body, 0, unroll=2)
    for h in range(Q_PER_KV):
        acc = acc_scr[h]
        o = acc[:HEAD_DIM, :] / acc[HEAD_DIM:HEAD_DIM + 1, :]
        o_ref[:, h * HEAD_DIM:(h + 1) * HEAD_DIM] = o.T.astype(o_ref.dtype)


def attention(qt, k_rot, vt, prev_out, row0, batch, seq, bq=512):
    t = k_rot.shape[0]
    bk = vt.shape[-1]
    bq = min(bq, seq)
    assert row0 % seq == 0 and seq % bq == 0 and seq % bk == 0
    group_w = Q_PER_KV * HEAD_DIM
    nq = seq // bq
    nk = seq // bk
    in_specs = [
        pl.BlockSpec((group_w, bq), lambda b, h, i: (h, row0 // bq + b * nq + i)),
        pl.BlockSpec((seq, HEAD_DIM), lambda b, h, i: (row0 // seq + b, h)),
        pl.BlockSpec((None, nk, V_ROWS, bk), lambda b, h, i: (h, row0 // seq + b, 0, 0)),
    ]
    args = [qt, k_rot, vt]
    aliases = {}
    if prev_out is not None:
        in_specs.append(pl.BlockSpec(memory_space=pl.ANY))
        args.append(prev_out)
        aliases = {3: 0}
    return pl.pallas_call(
        functools.partial(_attn_kernel, bk=bk, nk=nk),
        grid=(batch, N_KV_HEADS, nq),
        in_specs=in_specs,
        out_specs=pl.BlockSpec((bq, group_w), lambda b, h, i: (row0 // bq + b * nq + i, h)),
        out_shape=jax.ShapeDtypeStruct((t, ATTN_WIDTH), BF16),
        scratch_shapes=[pltpu.VMEM((Q_PER_KV, 1, bq), F32),
                        pltpu.VMEM((Q_PER_KV, V_ROWS, bq), F32),
                        pltpu.VMEM((2, bk, bq), F32),
                        pltpu.VMEM((2, 1, bq), F32)],
        input_output_aliases=aliases,
        compiler_params=_params(("parallel", "parallel", "parallel"), 48),
        name="attention",
    )(*args)


def _convpool_kernel(ch_ref, cb_ref, cc_ref, pu_ref,
                     chp_ref, ccp_ref, pup_ref, chn_ref, ccn_ref, pun_ref,
                     cw_ref, pw_ref, ps_ref, o_ref, *, rows, segments):
    i = pl.program_id(0)
    start = i * rows
    pos0 = start
    seq_len = jnp.int32(0)
    has_prev = jnp.bool_(True)
    has_next = jnp.bool_(True)
    for seg_start, seg_len in segments:
        inside = (start >= seg_start) & (start < seg_start + seg_len)
        pos0 = jnp.where(inside, start - seg_start, pos0)
        seq_len = jnp.where(inside, seg_len, seq_len)
        has_prev = has_prev & (start != seg_start)
        has_next = has_next & (start + rows != seg_start + seg_len)
    prev_on = has_prev.astype(F32)
    next_on = has_next.astype(F32)

    def extended(cur_ref, prev_ref, next_ref):
        return jnp.concatenate([prev_ref[...].astype(F32) * prev_on,
                                cur_ref[...].astype(F32),
                                next_ref[...].astype(F32) * next_on], axis=0)

    ext_rows = rows + 2 * HALO

    def shifted(x, d):
        return pltpu.roll(x, d % ext_rows, axis=0)

    u = extended(cc_ref, ccp_ref, ccn_ref) * extended(ch_ref, chp_ref, chn_ref)
    cw = cw_ref[...]
    y = shifted(u, 1) * cw[0:1, :] + u * cw[1:2, :] + shifted(u, -1) * cw[2:3, :]
    o_ref[:, :CONV_WIDTH] = (cb_ref[...].astype(F32) * y[HALO:HALO + rows, :]).astype(o_ref.dtype)

    x = extended(pu_ref, pup_ref, pun_ref)
    pos = (pos0 + lax.broadcasted_iota(jnp.int32, (rows, 1), 0))
    for gi, w in enumerate(POOL_WINDOWS):
        sl = slice(gi * POOL_GROUP, (gi + 1) * POOL_GROUP)
        xg = x[:, sl]
        tot = xg + shifted(xg, 1)
        half = 1
        while 2 * half < w:
            tot = shifted(tot, half) + shifted(tot, -half)
            half *= 2
        hi = jnp.minimum(pos + w // 2, seq_len)
        lo = jnp.maximum(pos - w // 2, 0)
        cnt = (hi - lo).astype(F32)
        m = tot[HALO:HALO + rows, :] / cnt - xg[HALO:HALO + rows, :]
        yg = jnp.dot(m.astype(BF16), pw_ref[gi], preferred_element_type=F32)
        o_ref[:, CONV_WIDTH + gi * POOL_GROUP:CONV_WIDTH + (gi + 1) * POOL_GROUP] = (
            yg * ps_ref[:, sl]).astype(o_ref.dtype)


def conv_pool(proj, conv_w, pool_w, pool_scale, segments, rows=512):
    t = proj.shape[0]
    rows = min(rows, min(n for _, n in segments))
    assert all(s % rows == 0 and n % rows == 0 for s, n in segments)
    w = CONV_WIDTH
    per = rows // HALO
    last = t // HALO - 1

    def cur(off):
        return pl.BlockSpec((rows, w), lambda i: (i, off // w))

    def prev(off):
        return pl.BlockSpec((HALO, w), lambda i: (jnp.maximum(i * per - 1, 0), off // w))

    def nxt(off):
        return pl.BlockSpec((HALO, w), lambda i: (jnp.minimum((i + 1) * per, last), off // w))

    kernel = functools.partial(_convpool_kernel, rows=rows, segments=segments)
    return pl.pallas_call(
        kernel,
        grid=(t // rows,),
        in_specs=[cur(CH_OFF), cur(CB_OFF), cur(CC_OFF), cur(PU_OFF),
                  prev(CH_OFF), prev(CC_OFF), prev(PU_OFF),
                  nxt(CH_OFF), nxt(CC_OFF), nxt(PU_OFF),
                  pl.BlockSpec((3, w), lambda i: (0, 0)),
                  pl.BlockSpec((len(POOL_WINDOWS), POOL_GROUP, POOL_GROUP), lambda i: (0, 0, 0)),
                  pl.BlockSpec((1, POOL_WIDTH), lambda i: (0, 0))],
        out_specs=pl.BlockSpec((rows, CONV_WIDTH + POOL_WIDTH), lambda i: (i, 0)),
        out_shape=jax.ShapeDtypeStruct((t, CONV_WIDTH + POOL_WIDTH), BF16),
        compiler_params=_params(("parallel",), 48),
        name="conv_pool",
    )(proj, proj, proj, proj, proj, proj, proj, proj, proj, proj,
      conv_w, pool_w, pool_scale.reshape(1, POOL_WIDTH))


def _trunk(x, segments, attn_calls, ln_mix_pre, ln_mix_post, ln_ffn_pre, ln_ffn_post, q_norm, k_norm,
           w_in, conv_w, pool_w, pool_scale, w_out, w_gate, w_up, w_down):
    cos, sin = _rope_tables(segments)
    pad = D_FF_PAD - D_FF
    key_block = min(512, min(n for _, n in segments))
    h = norm_cast(x, ln_mix_pre[0])
    for l in range(DEPTH):
        proj = matmul(h, w_in[l].astype(BF16), BF16)
        qt, k_rot, vt = qkv_prep(proj, cos, sin, q_norm[l], k_norm[l], rows=key_block)
        a_out = None
        for row0, batch, seq in attn_calls:
            a_out = attention(qt, k_rot, vt, a_out, row0, batch, seq)
        cp_out = conv_pool(proj, conv_w[l], pool_w[l].astype(BF16), pool_scale[l], segments)
        mix = matmul_two_lhs(a_out, cp_out, w_out[l].astype(BF16))
        x, h = resnorm(mix, x, ln_mix_post[l], ln_ffn_pre[l])
        wg = jnp.pad(w_gate[l].astype(BF16), ((0, 0), (0, pad)))
        wu = jnp.pad(w_up[l].astype(BF16), ((0, 0), (0, pad)))
        wd = jnp.pad(w_down[l].astype(BF16), ((0, pad), (0, 0)))
        act = gate_up(h, wg, wu)
        f = matmul_split_k(act, wd, bk=D_FF_PAD // 4)
        x, h = resnorm(f, x, ln_ffn_post[l], ln_mix_pre[l + 1] if l + 1 < DEPTH else None)
    return x


def kernel(x_prompt, x_sample, ln_mix_pre, ln_mix_post, ln_ffn_pre, ln_ffn_post, q_norm, k_norm,
           w_in, conv_w, pool_w, pool_scale, w_out, w_gate, w_up, w_down):
    pb, ps, d = x_prompt.shape
    sb, ss, _ = x_sample.shape
    x = jnp.concatenate([x_prompt.reshape(pb * ps, d), x_sample.reshape(sb * ss, d)], axis=0)
    segments = tuple((b * ps, ps) for b in range(pb)) + tuple((pb * ps + b * ss, ss) for b in range(sb))
    attn_calls = ((0, pb, ps), (pb * ps, sb, ss))
    y = _trunk(x, segments, attn_calls, ln_mix_pre, ln_mix_post, ln_ffn_pre, ln_ffn_post, q_norm, k_norm,
               w_in, conv_w, pool_w, pool_scale, w_out, w_gate, w_up, w_down)
    return (y[:pb * ps].reshape(pb, ps, d), y[pb * ps:].reshape(sb, ss, d))
```

```python
import functools

import jax
import jax.numpy as jnp
from jax import lax
from jax.experimental import pallas as pl
from jax.experimental.pallas import tpu as pltpu

D_MODEL = 4096
DEPTH = 2
GRID_W = 64
HEAD_DIM = 128
N_Q_HEADS = 16
N_KV_HEADS = 4
Q_PER_KV = N_Q_HEADS // N_KV_HEADS
ATTN_WIDTH = N_Q_HEADS * HEAD_DIM
KV_WIDTH = N_KV_HEADS * HEAD_DIM
CONV_WIDTH = D_MODEL // 4
POOL_WIDTH = D_MODEL // 4
POOL_WINDOWS = (2, 4, 8, 16)
POOL_GROUP = POOL_WIDTH // len(POOL_WINDOWS)
IN_WIDTH = ATTN_WIDTH + 2 * KV_WIDTH + 3 * CONV_WIDTH + POOL_WIDTH
D_FF = 11008
D_FF_PAD = 11264
ROPE_HALF = HEAD_DIM // 2
ROPE_THETA = 10000.0
EPS = 1e-6

Q_OFF = 0
K_OFF = ATTN_WIDTH
V_OFF = K_OFF + KV_WIDTH
CH_OFF = V_OFF + KV_WIDTH
CB_OFF = CH_OFF + CONV_WIDTH
CC_OFF = CB_OFF + CONV_WIDTH
PU_OFF = CC_OFF + CONV_WIDTH

HALO = 16
V7X_VMEM_BYTES = 64 * 1024 * 1024

F32 = jnp.float32
BF16 = jnp.bfloat16


def _params(semantics, vmem_mb):
    assert vmem_mb * 1024 * 1024 < V7X_VMEM_BYTES
    return pltpu.CompilerParams(dimension_semantics=semantics,
                                vmem_limit_bytes=vmem_mb * 1024 * 1024)


def _rms(x):
    return x * lax.rsqrt(jnp.mean(x * x, axis=-1, keepdims=True) + EPS)


def _rows_spec(rows, d, row0):
    assert row0 % rows == 0
    return pl.BlockSpec((rows, d), lambda i: (row0 // rows + i, 0))


def _rowwise_call(body, name, n_rows, ins, gains, outs, prev_outs, rows=256):
    d = ins[0][0].shape[1]
    in_specs = [_rows_spec(rows, d, r0) for _, r0 in ins]
    in_specs += [pl.BlockSpec((1, d), lambda i: (0, 0)) for _ in gains]
    args = [a for a, _ in ins] + [g.reshape(1, d) for g in gains]
    aliases = {}
    kernel = body
    if prev_outs is not None:
        n_in = len(args)
        in_specs += [pl.BlockSpec(memory_space=pl.ANY) for _ in prev_outs]
        args += list(prev_outs)
        aliases = {n_in + k: k for k in range(len(prev_outs))}

        def kernel(*refs):
            body(*refs[:n_in], *refs[n_in + len(prev_outs):])

    return pl.pallas_call(
        kernel,
        grid=(n_rows // rows,),
        in_specs=in_specs,
        out_specs=[_rows_spec(rows, d, r0) for _, r0, _ in outs],
        out_shape=[jax.ShapeDtypeStruct((total, d), dt) for total, _, dt in outs],
        input_output_aliases=aliases,
        compiler_params=_params(("parallel",), 40),
        name=name,
    )(*args)


def _norm_cast_kernel(x_ref, g_ref, h_ref):
    h_ref[...] = (_rms(x_ref[...]) * g_ref[...]).astype(h_ref.dtype)


def _resnorm_kernel(f_ref, x_ref, g_post_ref, g_next_ref, xo_ref, h_ref):
    xn = x_ref[...] + _rms(f_ref[...]) * g_post_ref[...]
    xo_ref[...] = xn
    h_ref[...] = (_rms(xn) * g_next_ref[...]).astype(h_ref.dtype)


def _resnorm_last_kernel(f_ref, x_ref, g_post_ref, xo_ref):
    xo_ref[...] = x_ref[...] + _rms(f_ref[...]) * g_post_ref[...]


def norm_cast_parts(parts, g, total):
    h = None
    for x, row0, n in parts:
        h, = _rowwise_call(_norm_cast_kernel, "norm_cast", n, [(x, 0)], [g], [(total, row0, BF16)],
                           None if h is None else [h])
    return h


def resnorm_from_parts(f, parts, g_post, g_next, total):
    outs = None
    for x, row0, n in parts:
        outs = _rowwise_call(_resnorm_kernel, "resnorm", n, [(f, row0), (x, 0)], [g_post, g_next],
                             [(total, row0, F32), (total, row0, BF16)], outs)
    return outs


def resnorm(f, x, g_post, g_next):
    total = x.shape[0]
    return _rowwise_call(_resnorm_kernel, "resnorm", total, [(f, 0), (x, 0)], [g_post, g_next],
                         [(total, 0, F32), (total, 0, BF16)], None)


def resnorm_last_to_parts(f, x, g_post, parts):
    return [_rowwise_call(_resnorm_last_kernel, "resnorm_last", n, [(f, row0), (x, row0)], [g_post],
                          [(n, 0, F32)], None)[0] for row0, n in parts]


def _cast_weight_kernel(w_ref, o_ref, *, valid_rows, valid_cols):
    br, bc = o_ref.shape
    r = pl.program_id(0) * br + lax.broadcasted_iota(jnp.int32, (br, bc), 0)
    c = pl.program_id(1) * bc + lax.broadcasted_iota(jnp.int32, (br, bc), 1)
    o_ref[...] = jnp.where((r < valid_rows) & (c < valid_cols), w_ref[...], 0.0).astype(o_ref.dtype)


def cast_weight(w, layer, out_rows=None, out_cols=None, br=512, bc=1024):
    _, rows, cols = w.shape
    out_rows = out_rows or rows
    out_cols = out_cols or cols
    return pl.pallas_call(
        functools.partial(_cast_weight_kernel, valid_rows=rows, valid_cols=cols),
        grid=(out_rows // br, out_cols // bc),
        in_specs=[pl.BlockSpec((None, br, bc), lambda i, j: (layer, i, j))],
        out_specs=pl.BlockSpec((br, bc), lambda i, j: (i, j)),
        out_shape=jax.ShapeDtypeStruct((out_rows, out_cols), BF16),
        compiler_params=_params(("parallel", "parallel"), 32),
        name="cast_weight",
    )(w)


def _mm_kernel(a_ref, b_ref, o_ref):
    o_ref[...] = jnp.dot(a_ref[...], b_ref[...], preferred_element_type=F32).astype(o_ref.dtype)


def matmul(a, b, out_dtype, bm=1024, bn=1024):
    m, k = a.shape
    _, n = b.shape
    bm = min(bm, m)
    return pl.pallas_call(
        _mm_kernel,
        grid=(m // bm, n // bn),
        in_specs=[pl.BlockSpec((bm, k), lambda i, j: (i, 0)),
                  pl.BlockSpec((k, bn), lambda i, j: (0, j))],
        out_specs=pl.BlockSpec((bm, bn), lambda i, j: (i, j)),
        out_shape=jax.ShapeDtypeStruct((m, n), out_dtype),
        compiler_params=_params(("parallel", "parallel"), 48),
        name="matmul_full_k",
    )(a, b)


def _mm2_kernel(a1_ref, a2_ref, b1_ref, b2_ref, o_ref):
    o_ref[...] = (jnp.dot(a1_ref[...], b1_ref[...], preferred_element_type=F32)
                  + jnp.dot(a2_ref[...], b2_ref[...], preferred_element_type=F32))


def matmul_two_lhs(a1, a2, b, bm=1024, bn=1024):
    m, k1 = a1.shape
    _, k2 = a2.shape
    assert k1 == k2 and b.shape[0] == k1 + k2
    n = b.shape[1]
    bm = min(bm, m)
    return pl.pallas_call(
        _mm2_kernel,
        grid=(m // bm, n // bn),
        in_specs=[pl.BlockSpec((bm, k1), lambda i, j: (i, 0)),
                  pl.BlockSpec((bm, k2), lambda i, j: (i, 0)),
                  pl.BlockSpec((k1, bn), lambda i, j: (0, j)),
                  pl.BlockSpec((k2, bn), lambda i, j: (1, j))],
        out_specs=pl.BlockSpec((bm, bn), lambda i, j: (i, j)),
        out_shape=jax.ShapeDtypeStruct((m, n), F32),
        compiler_params=_params(("parallel", "parallel"), 48),
        name="matmul_two_lhs",
    )(a1, a2, b, b)


def _mm_acc_kernel(a_ref, b_ref, o_ref):
    part = jnp.dot(a_ref[...], b_ref[...], preferred_element_type=F32)

    @pl.when(pl.program_id(2) == 0)
    def _():
        o_ref[...] = part

    @pl.when(pl.program_id(2) != 0)
    def _():
        o_ref[...] += part


def matmul_split_k(a, b, bk, bm=1024, bn=1024):
    m, k = a.shape
    _, n = b.shape
    bm = min(bm, m)
    return pl.pallas_call(
        _mm_acc_kernel,
        grid=(m // bm, n // bn, k // bk),
        in_specs=[pl.BlockSpec((bm, bk), lambda i, j, kk: (i, kk)),
                  pl.BlockSpec((bk, bn), lambda i, j, kk: (kk, j))],
        out_specs=pl.BlockSpec((bm, bn), lambda i, j, kk: (i, j)),
        out_shape=jax.ShapeDtypeStruct((m, n), F32),
        compiler_params=_params(("parallel", "parallel", "arbitrary"), 48),
        name="matmul_split_k",
    )(a, b)


def _gateup_kernel(h_ref, wg_ref, wu_ref, o_ref):
    h = h_ref[...]
    g = jnp.dot(h, wg_ref[...], preferred_element_type=F32)
    u = jnp.dot(h, wu_ref[...], preferred_element_type=F32)
    o_ref[...] = (g * jax.nn.sigmoid(g) * u).astype(o_ref.dtype)


def gate_up(h, wg, wu, bm=1024, bn=512):
    m, k = h.shape
    _, n = wg.shape
    bm = min(bm, m)
    return pl.pallas_call(
        _gateup_kernel,
        grid=(m // bm, n // bn),
        in_specs=[pl.BlockSpec((bm, k), lambda i, j: (i, 0)),
                  pl.BlockSpec((k, bn), lambda i, j: (0, j)),
                  pl.BlockSpec((k, bn), lambda i, j: (0, j))],
        out_specs=pl.BlockSpec((bm, bn), lambda i, j: (i, j)),
        out_shape=jax.ShapeDtypeStruct((m, n), BF16),
        compiler_params=_params(("parallel", "parallel"), 48),
        name="gate_up",
    )(h, wg, wu)


V_ROWS = HEAD_DIM + 16
LOG2E = 1.4426950408889634


def _rope_tables(segments):
    pos = jnp.concatenate([jnp.arange(n, dtype=jnp.int32) for _, n in segments])
    r = (pos // GRID_W).astype(F32)
    c = (pos % GRID_W).astype(F32)
    inv = 1.0 / (ROPE_THETA ** (jnp.arange(0, ROPE_HALF, 2, dtype=F32) / ROPE_HALF))
    ang_r = r[:, None] * inv
    ang_c = c[:, None] * inv
    cos = jnp.concatenate([jnp.cos(ang_r), jnp.cos(ang_r), jnp.cos(ang_c), jnp.cos(ang_c)], axis=-1)
    sin = jnp.concatenate([-jnp.sin(ang_r), jnp.sin(ang_r), -jnp.sin(ang_c), jnp.sin(ang_c)], axis=-1)
    return cos, sin


def _norm_rope(x, gain, cos, sin, upper):
    xn = _rms(x) * gain
    quarter = ROPE_HALF // 2
    partner = jnp.where(upper, pltpu.roll(xn, quarter, axis=1), pltpu.roll(xn, HEAD_DIM - quarter, axis=1))
    return xn * cos + partner * sin


def _qkv_prep_kernel(q_ref, k_ref, v_ref, cos_ref, sin_ref, qg_ref, kg_ref, qt_ref, ko_ref, vt_ref):
    cos = cos_ref[...]
    sin = sin_ref[...]
    rows = cos.shape[0]
    lane = lax.broadcasted_iota(jnp.int32, cos.shape, 1)
    upper = (lane & (ROPE_HALF // 2)) != 0
    q_scale = HEAD_DIM ** -0.5 * LOG2E
    qg = qg_ref[...]
    kg = kg_ref[...]
    for h in range(N_Q_HEADS):
        sl = slice(h * HEAD_DIM, (h + 1) * HEAD_DIM)
        y = _norm_rope(q_ref[:, sl].astype(F32), qg, cos, sin, upper) * q_scale
        qt_ref[sl, :] = y.T.astype(qt_ref.dtype)
    for h in range(N_KV_HEADS):
        sl = slice(h * HEAD_DIM, (h + 1) * HEAD_DIM)
        y = _norm_rope(k_ref[:, sl].astype(F32), kg, cos, sin, upper)
        ko_ref[:, sl] = y.astype(ko_ref.dtype)
        vt_ref[h, 0, :HEAD_DIM, :] = v_ref[:, sl].astype(F32).T.astype(vt_ref.dtype)
        vt_ref[h, 0, HEAD_DIM:, :] = jnp.ones((V_ROWS - HEAD_DIM, rows), vt_ref.dtype)


def qkv_prep(proj, cos, sin, q_gain, k_gain, rows):
    t = proj.shape[0]
    return pl.pallas_call(
        _qkv_prep_kernel,
        grid=(t // rows,),
        in_specs=[pl.BlockSpec((rows, ATTN_WIDTH), lambda i: (i, Q_OFF // ATTN_WIDTH)),
                  pl.BlockSpec((rows, KV_WIDTH), lambda i: (i, K_OFF // KV_WIDTH)),
                  pl.BlockSpec((rows, KV_WIDTH), lambda i: (i, V_OFF // KV_WIDTH)),
                  pl.BlockSpec((rows, HEAD_DIM), lambda i: (i, 0)),
                  pl.BlockSpec((rows, HEAD_DIM), lambda i: (i, 0)),
                  pl.BlockSpec((1, HEAD_DIM), lambda i: (0, 0)),
                  pl.BlockSpec((1, HEAD_DIM), lambda i: (0, 0))],
        out_specs=[pl.BlockSpec((ATTN_WIDTH, rows), lambda i: (0, i)),
                   pl.BlockSpec((rows, KV_WIDTH), lambda i: (i, 0)),
                   pl.BlockSpec((N_KV_HEADS, 1, V_ROWS, rows), lambda i: (0, i, 0, 0))],
        out_shape=[jax.ShapeDtypeStruct((ATTN_WIDTH, t), BF16),
                   jax.ShapeDtypeStruct((t, KV_WIDTH), BF16),
                   jax.ShapeDtypeStruct((N_KV_HEADS, t // rows, V_ROWS, rows), BF16)],
        compiler_params=_params(("parallel",), 40),
        name="qkv_prep",
    )(proj, proj, proj, cos, sin, q_gain.reshape(1, HEAD_DIM), k_gain.reshape(1, HEAD_DIM))


def _attn_kernel(qt_ref, k_ref, vt_ref, *rest, bk, nk):
    o_ref, m_scr, acc_scr, s_scr, smax_scr = rest[-5:]
    m_scr[...] = jnp.full(m_scr.shape, -jnp.inf, F32)
    acc_scr[...] = jnp.zeros(acc_scr.shape, F32)

    def scores(kb, h, slot):
        s = jnp.dot(kb, qt_ref[h * HEAD_DIM:(h + 1) * HEAD_DIM, :], preferred_element_type=F32)
        s_scr[slot] = s
        smax_scr[slot] = jnp.max(s, axis=0, keepdims=True)

    def key_block(j):
        return k_ref[pl.ds(pl.multiple_of(j * bk, bk), bk), :]

    scores(key_block(0), 0, 0)

    def body(j, carry):
        kb = key_block(j)
        kb_next = key_block(jnp.minimum(j + 1, nk - 1))
        vb = vt_ref[j]
        for h in range(Q_PER_KV):
            slot = h % 2
            if h + 1 < Q_PER_KV:
                scores(kb, h + 1, 1 - slot)
            else:
                scores(kb_next, 0, 1 - slot)
            m_prev = m_scr[h]
            m_new = jnp.maximum(m_prev, smax_scr[slot])
            alpha = jnp.exp2(m_prev - m_new)
            p = jnp.exp2(s_scr[slot] - m_new).astype(vb.dtype)
            acc_scr[h] = acc_scr[h] * alpha + jnp.dot(vb, p, preferred_element_type=F32)
            m_scr[h] = m_new
        return carry

    lax.fori_loop(0, nk, body, 0, unroll=4)
    for h in range(Q_PER_KV):
        acc = acc_scr[h]
        o = acc[:HEAD_DIM, :] / acc[HEAD_DIM:HEAD_DIM + 1, :]
        o_ref[:, h * HEAD_DIM:(h + 1) * HEAD_DIM] = o.T.astype(o_ref.dtype)


def attention(qt, k_rot, vt, prev_out, row0, batch, seq, bq=512):
    t = k_rot.shape[0]
    bk = vt.shape[-1]
    bq = min(bq, seq)
    assert row0 % seq == 0 and seq % bq == 0 and seq % bk == 0
    group_w = Q_PER_KV * HEAD_DIM
    nq = seq // bq
    nk = seq // bk
    in_specs = [
        pl.BlockSpec((group_w, bq), lambda b, h, i: (h, row0 // bq + b * nq + i)),
        pl.BlockSpec((seq, HEAD_DIM), lambda b, h, i: (row0 // seq + b, h)),
        pl.BlockSpec((None, nk, V_ROWS, bk), lambda b, h, i: (h, row0 // seq + b, 0, 0)),
    ]
    args = [qt, k_rot, vt]
    aliases = {}
    if prev_out is not None:
        in_specs.append(pl.BlockSpec(memory_space=pl.ANY))
        args.append(prev_out)
        aliases = {3: 0}
    return pl.pallas_call(
        functools.partial(_attn_kernel, bk=bk, nk=nk),
        grid=(batch, N_KV_HEADS, nq),
        in_specs=in_specs,
        out_specs=pl.BlockSpec((bq, group_w), lambda b, h, i: (row0 // bq + b * nq + i, h)),
        out_shape=jax.ShapeDtypeStruct((t, ATTN_WIDTH), BF16),
        scratch_shapes=[pltpu.VMEM((Q_PER_KV, 1, bq), F32),
                        pltpu.VMEM((Q_PER_KV, V_ROWS, bq), F32),
                        pltpu.VMEM((2, bk, bq), F32),
                        pltpu.VMEM((2, 1, bq), F32)],
        input_output_aliases=aliases,
        compiler_params=_params(("parallel", "parallel", "parallel"), 48),
        name="attention",
    )(*args)


def _convpool_kernel(ch_ref, cb_ref, cc_ref, pu_ref,
                     chp_ref, ccp_ref, pup_ref, chn_ref, ccn_ref, pun_ref,
                     cw_ref, pw_ref, ps_ref, o_ref, *, rows, segments):
    i = pl.program_id(0)
    start = i * rows
    pos0 = start
    seq_len = jnp.int32(0)
    has_prev = jnp.bool_(True)
    has_next = jnp.bool_(True)
    for seg_start, seg_len in segments:
        inside = (start >= seg_start) & (start < seg_start + seg_len)
        pos0 = jnp.where(inside, start - seg_start, pos0)
        seq_len = jnp.where(inside, seg_len, seq_len)
        has_prev = has_prev & (start != seg_start)
        has_next = has_next & (start + rows != seg_start + seg_len)
    prev_on = has_prev.astype(F32)
    next_on = has_next.astype(F32)

    def extended(cur_ref, prev_ref, next_ref):
        return jnp.concatenate([prev_ref[...].astype(F32) * prev_on,
                                cur_ref[...].astype(F32),
                                next_ref[...].astype(F32) * next_on], axis=0)

    ext_rows = rows + 2 * HALO

    def shifted(x, d):
        return pltpu.roll(x, d % ext_rows, axis=0)

    u = extended(cc_ref, ccp_ref, ccn_ref) * extended(ch_ref, chp_ref, chn_ref)
    cw = cw_ref[...]
    y = shifted(u, 1) * cw[0:1, :] + u * cw[1:2, :] + shifted(u, -1) * cw[2:3, :]
    o_ref[:, :CONV_WIDTH] = (cb_ref[...].astype(F32) * y[HALO:HALO + rows, :]).astype(o_ref.dtype)

    x = extended(pu_ref, pup_ref, pun_ref)
    pos = (pos0 + lax.broadcasted_iota(jnp.int32, (rows, 1), 0))
    for gi, w in enumerate(POOL_WINDOWS):
        sl = slice(gi * POOL_GROUP, (gi + 1) * POOL_GROUP)
        xg = x[:, sl]
        tot = xg + shifted(xg, 1)
        half = 1
        while 2 * half < w:
            tot = shifted(tot, half) + shifted(tot, -half)
            half *= 2
        hi = jnp.minimum(pos + w // 2, seq_len)
        lo = jnp.maximum(pos - w // 2, 0)
        cnt = (hi - lo).astype(F32)
        m = tot[HALO:HALO + rows, :] / cnt - xg[HALO:HALO + rows, :]
        yg = jnp.dot(m.astype(BF16), pw_ref[gi], preferred_element_type=F32)
        o_ref[:, CONV_WIDTH + gi * POOL_GROUP:CONV_WIDTH + (gi + 1) * POOL_GROUP] = (
            yg * ps_ref[:, sl]).astype(o_ref.dtype)


def conv_pool(proj, conv_w, pool_w, pool_scale, segments, rows=512):
    t = proj.shape[0]
    rows = min(rows, min(n for _, n in segments))
    assert all(s % rows == 0 and n % rows == 0 for s, n in segments)
    w = CONV_WIDTH
    per = rows // HALO
    last = t // HALO - 1

    def cur(off):
        return pl.BlockSpec((rows, w), lambda i: (i, off // w))

    def prev(off):
        return pl.BlockSpec((HALO, w), lambda i: (jnp.maximum(i * per - 1, 0), off // w))

    def nxt(off):
        return pl.BlockSpec((HALO, w), lambda i: (jnp.minimum((i + 1) * per, last), off // w))

    kernel = functools.partial(_convpool_kernel, rows=rows, segments=segments)
    return pl.pallas_call(
        kernel,
        grid=(t // rows,),
        in_specs=[cur(CH_OFF), cur(CB_OFF), cur(CC_OFF), cur(PU_OFF),
                  prev(CH_OFF), prev(CC_OFF), prev(PU_OFF),
                  nxt(CH_OFF), nxt(CC_OFF), nxt(PU_OFF),
                  pl.BlockSpec((3, w), lambda i: (0, 0)),
                  pl.BlockSpec((len(POOL_WINDOWS), POOL_GROUP, POOL_GROUP), lambda i: (0, 0, 0)),
                  pl.BlockSpec((1, POOL_WIDTH), lambda i: (0, 0))],
        out_specs=pl.BlockSpec((rows, CONV_WIDTH + POOL_WIDTH), lambda i: (i, 0)),
        out_shape=jax.ShapeDtypeStruct((t, CONV_WIDTH + POOL_WIDTH), BF16),
        compiler_params=_params(("parallel",), 48),
        name="conv_pool",
    )(proj, proj, proj, proj, proj, proj, proj, proj, proj, proj,
      conv_w, pool_w, pool_scale.reshape(1, POOL_WIDTH))


def _trunk(x_parts, segments, attn_calls, ln_mix_pre, ln_mix_post, ln_ffn_pre, ln_ffn_post, q_norm, k_norm,
           w_in, conv_w, pool_w, pool_scale, w_out, w_gate, w_up, w_down):
    total = sum(n for _, _, n in x_parts)
    cos, sin = _rope_tables(segments)
    key_block = min(512, min(n for _, n in segments))
    h = norm_cast_parts(x_parts, ln_mix_pre[0], total)
    x = None
    for l in range(DEPTH):
        proj = matmul(h, cast_weight(w_in, l), BF16)
        qt, k_rot, vt = qkv_prep(proj, cos, sin, q_norm[l], k_norm[l], rows=key_block)
        a_out = None
        for row0, batch, seq in attn_calls:
            a_out = attention(qt, k_rot, vt, a_out, row0, batch, seq)
        cp_out = conv_pool(proj, conv_w[l], pool_w[l].astype(BF16), pool_scale[l], segments)
        mix = matmul_two_lhs(a_out, cp_out, cast_weight(w_out, l))
        if x is None:
            x, h = resnorm_from_parts(mix, x_parts, ln_mix_post[l], ln_ffn_pre[l], total)
        else:
            x, h = resnorm(mix, x, ln_mix_post[l], ln_ffn_pre[l])
        act = gate_up(h, cast_weight(w_gate, l, out_cols=D_FF_PAD), cast_weight(w_up, l, out_cols=D_FF_PAD))
        f = matmul_split_k(act, cast_weight(w_down, l, out_rows=D_FF_PAD), bk=D_FF_PAD // 4)
        if l + 1 < DEPTH:
            x, h = resnorm(f, x, ln_ffn_post[l], ln_mix_pre[l + 1])
    return resnorm_last_to_parts(f, x, ln_ffn_post[DEPTH - 1], [(row0, n) for _, row0, n in x_parts])


def kernel(x_prompt, x_sample, ln_mix_pre, ln_mix_post, ln_ffn_pre, ln_ffn_post, q_norm, k_norm,
           w_in, conv_w, pool_w, pool_scale, w_out, w_gate, w_up, w_down):
    pb, ps, d = x_prompt.shape
    sb, ss, _ = x_sample.shape
    x_parts = [(x_prompt.reshape(pb * ps, d), 0, pb * ps), (x_sample.reshape(sb * ss, d), pb * ps, sb * ss)]
    segments = tuple((b * ps, ps) for b in range(pb)) + tuple((pb * ps + b * ss, ss) for b in range(sb))
    attn_calls = ((0, pb, ps), (pb * ps, sb, ss))
    y_prompt, y_sample = _trunk(x_parts, segments, attn_calls, ln_mix_pre, ln_mix_post, ln_ffn_pre, ln_ffn_post,
                                q_norm, k_norm, w_in, conv_w, pool_w, pool_scale, w_out, w_gate, w_up, w_down)
    return (y_prompt.reshape(pb, ps, d), y_sample.reshape(sb, ss, d))
```

```python
import functools

import jax
import jax.numpy as jnp
from jax import lax
from jax.experimental import pallas as pl
from jax.experimental.pallas import tpu as pltpu

D_MODEL = 4096
DEPTH = 2
GRID_W = 64
HEAD_DIM = 128
N_Q_HEADS = 16
N_KV_HEADS = 4
Q_PER_KV = N_Q_HEADS // N_KV_HEADS
ATTN_WIDTH = N_Q_HEADS * HEAD_DIM
KV_WIDTH = N_KV_HEADS * HEAD_DIM
CONV_WIDTH = D_MODEL // 4
POOL_WIDTH = D_MODEL // 4
POOL_WINDOWS = (2, 4, 8, 16)
POOL_GROUP = POOL_WIDTH // len(POOL_WINDOWS)
IN_WIDTH = ATTN_WIDTH + 2 * KV_WIDTH + 3 * CONV_WIDTH + POOL_WIDTH
D_FF = 11008
D_FF_PAD = 11264
ROPE_HALF = HEAD_DIM // 2
ROPE_THETA = 10000.0
EPS = 1e-6

Q_OFF = 0
K_OFF = ATTN_WIDTH
V_OFF = K_OFF + KV_WIDTH
CH_OFF = V_OFF + KV_WIDTH
CB_OFF = CH_OFF + CONV_WIDTH
CC_OFF = CB_OFF + CONV_WIDTH
PU_OFF = CC_OFF + CONV_WIDTH

HALO = 16
V7X_VMEM_BYTES = 64 * 1024 * 1024

F32 = jnp.float32
BF16 = jnp.bfloat16


def _params(semantics, vmem_mb):
    assert vmem_mb * 1024 * 1024 < V7X_VMEM_BYTES
    return pltpu.CompilerParams(dimension_semantics=semantics,
                                vmem_limit_bytes=vmem_mb * 1024 * 1024)


def _rms(x):
    return x * lax.rsqrt(jnp.mean(x * x, axis=-1, keepdims=True) + EPS)


def _rows_spec(rows, d, row0):
    assert row0 % rows == 0
    return pl.BlockSpec((rows, d), lambda i: (row0 // rows + i, 0))


def _rowwise_call(body, name, n_rows, ins, gains, outs, prev_outs, rows=256):
    d = ins[0][0].shape[1]
    in_specs = [_rows_spec(rows, d, r0) for _, r0 in ins]
    in_specs += [pl.BlockSpec((1, d), lambda i: (0, 0)) for _ in gains]
    args = [a for a, _ in ins] + [g.reshape(1, d) for g in gains]
    aliases = {}
    kernel = body
    if prev_outs is not None:
        n_in = len(args)
        in_specs += [pl.BlockSpec(memory_space=pl.ANY) for _ in prev_outs]
        args += list(prev_outs)
        aliases = {n_in + k: k for k in range(len(prev_outs))}

        def kernel(*refs):
            body(*refs[:n_in], *refs[n_in + len(prev_outs):])

    return pl.pallas_call(
        kernel,
        grid=(n_rows // rows,),
        in_specs=in_specs,
        out_specs=[_rows_spec(rows, d, r0) for _, r0, _ in outs],
        out_shape=[jax.ShapeDtypeStruct((total, d), dt) for total, _, dt in outs],
        input_output_aliases=aliases,
        compiler_params=_params(("parallel",), 40),
        name=name,
    )(*args)


def _norm_cast_kernel(x_ref, g_ref, h_ref):
    h_ref[...] = (_rms(x_ref[...]) * g_ref[...]).astype(h_ref.dtype)


def _resnorm_kernel(f_ref, x_ref, g_post_ref, g_next_ref, xo_ref, h_ref):
    xn = x_ref[...] + _rms(f_ref[...]) * g_post_ref[...]
    xo_ref[...] = xn
    h_ref[...] = (_rms(xn) * g_next_ref[...]).astype(h_ref.dtype)


def _resnorm_last_kernel(f_ref, x_ref, g_post_ref, xo_ref):
    xo_ref[...] = x_ref[...] + _rms(f_ref[...]) * g_post_ref[...]


def norm_cast_parts(parts, g, total):
    h = None
    for x, row0, n in parts:
        h, = _rowwise_call(_norm_cast_kernel, "norm_cast", n, [(x, 0)], [g], [(total, row0, BF16)],
                           None if h is None else [h])
    return h


def resnorm_from_parts(f, parts, g_post, g_next, total):
    outs = None
    for x, row0, n in parts:
        outs = _rowwise_call(_resnorm_kernel, "resnorm", n, [(f, row0), (x, 0)], [g_post, g_next],
                             [(total, row0, F32), (total, row0, BF16)], outs)
    return outs


def resnorm(f, x, g_post, g_next):
    total = x.shape[0]
    return _rowwise_call(_resnorm_kernel, "resnorm", total, [(f, 0), (x, 0)], [g_post, g_next],
                         [(total, 0, F32), (total, 0, BF16)], None)


def resnorm_last_to_parts(f, x, g_post, parts):
    return [_rowwise_call(_resnorm_last_kernel, "resnorm_last", n, [(f, row0), (x, row0)], [g_post],
                          [(n, 0, F32)], None)[0] for row0, n in parts]


def _cast_weight_kernel(w_ref, o_ref, *, valid_rows, valid_cols):
    br, bc = o_ref.shape
    r = pl.program_id(0) * br + lax.broadcasted_iota(jnp.int32, (br, bc), 0)
    c = pl.program_id(1) * bc + lax.broadcasted_iota(jnp.int32, (br, bc), 1)
    o_ref[...] = jnp.where((r < valid_rows) & (c < valid_cols), w_ref[...], 0.0).astype(o_ref.dtype)


def cast_weight(w, layer, out_rows=None, out_cols=None, br=512, bc=1024):
    _, rows, cols = w.shape
    out_rows = out_rows or rows
    out_cols = out_cols or cols
    return pl.pallas_call(
        functools.partial(_cast_weight_kernel, valid_rows=rows, valid_cols=cols),
        grid=(out_rows // br, out_cols // bc),
        in_specs=[pl.BlockSpec((None, br, bc), lambda i, j: (layer, i, j))],
        out_specs=pl.BlockSpec((br, bc), lambda i, j: (i, j)),
        out_shape=jax.ShapeDtypeStruct((out_rows, out_cols), BF16),
        compiler_params=_params(("parallel", "parallel"), 32),
        name="cast_weight",
    )(w)


def _mm_kernel(a_ref, b_ref, o_ref):
    o_ref[...] = jnp.dot(a_ref[...], b_ref[...], preferred_element_type=F32).astype(o_ref.dtype)


def matmul(a, b, out_dtype, bm=1024, bn=1024):
    m, k = a.shape
    _, n = b.shape
    bm = min(bm, m)
    return pl.pallas_call(
        _mm_kernel,
        grid=(m // bm, n // bn),
        in_specs=[pl.BlockSpec((bm, k), lambda i, j: (i, 0)),
                  pl.BlockSpec((k, bn), lambda i, j: (0, j))],
        out_specs=pl.BlockSpec((bm, bn), lambda i, j: (i, j)),
        out_shape=jax.ShapeDtypeStruct((m, n), out_dtype),
        compiler_params=_params(("parallel", "parallel"), 48),
        name="matmul_full_k",
    )(a, b)


def _mm2_kernel(a1_ref, a2_ref, b1_ref, b2_ref, o_ref):
    o_ref[...] = (jnp.dot(a1_ref[...], b1_ref[...], preferred_element_type=F32)
                  + jnp.dot(a2_ref[...], b2_ref[...], preferred_element_type=F32))


def matmul_two_lhs(a1, a2, b, bm=1024, bn=1024):
    m, k1 = a1.shape
    _, k2 = a2.shape
    assert k1 == k2 and b.shape[0] == k1 + k2
    n = b.shape[1]
    bm = min(bm, m)
    return pl.pallas_call(
        _mm2_kernel,
        grid=(m // bm, n // bn),
        in_specs=[pl.BlockSpec((bm, k1), lambda i, j: (i, 0)),
                  pl.BlockSpec((bm, k2), lambda i, j: (i, 0)),
                  pl.BlockSpec((k1, bn), lambda i, j: (0, j)),
                  pl.BlockSpec((k2, bn), lambda i, j: (1, j))],
        out_specs=pl.BlockSpec((bm, bn), lambda i, j: (i, j)),
        out_shape=jax.ShapeDtypeStruct((m, n), F32),
        compiler_params=_params(("parallel", "parallel"), 48),
        name="matmul_two_lhs",
    )(a1, a2, b, b)


def _mm_acc_kernel(a_ref, b_ref, o_ref):
    part = jnp.dot(a_ref[...], b_ref[...], preferred_element_type=F32)

    @pl.when(pl.program_id(2) == 0)
    def _():
        o_ref[...] = part

    @pl.when(pl.program_id(2) != 0)
    def _():
        o_ref[...] += part


def matmul_split_k(a, b, bk, bm=1024, bn=1024):
    m, k = a.shape
    _, n = b.shape
    bm = min(bm, m)
    return pl.pallas_call(
        _mm_acc_kernel,
        grid=(m // bm, n // bn, k // bk),
        in_specs=[pl.BlockSpec((bm, bk), lambda i, j, kk: (i, kk)),
                  pl.BlockSpec((bk, bn), lambda i, j, kk: (kk, j))],
        out_specs=pl.BlockSpec((bm, bn), lambda i, j, kk: (i, j)),
        out_shape=jax.ShapeDtypeStruct((m, n), F32),
        compiler_params=_params(("parallel", "parallel", "arbitrary"), 48),
        name="matmul_split_k",
    )(a, b)


def _gateup_kernel(h_ref, wg_ref, wu_ref, o_ref):
    h = h_ref[...]
    g = jnp.dot(h, wg_ref[...], preferred_element_type=F32)
    u = jnp.dot(h, wu_ref[...], preferred_element_type=F32)
    o_ref[...] = (g * jax.nn.sigmoid(g) * u).astype(o_ref.dtype)


def gate_up(h, wg, wu, bm=1024, bn=512):
    m, k = h.shape
    _, n = wg.shape
    bm = min(bm, m)
    return pl.pallas_call(
        _gateup_kernel,
        grid=(m // bm, n // bn),
        in_specs=[pl.BlockSpec((bm, k), lambda i, j: (i, 0)),
                  pl.BlockSpec((k, bn), lambda i, j: (0, j)),
                  pl.BlockSpec((k, bn), lambda i, j: (0, j))],
        out_specs=pl.BlockSpec((bm, bn), lambda i, j: (i, j)),
        out_shape=jax.ShapeDtypeStruct((m, n), BF16),
        compiler_params=_params(("parallel", "parallel"), 48),
        name="gate_up",
    )(h, wg, wu)


V_ROWS = HEAD_DIM + 16
LOG2E = 1.4426950408889634


def _rope_tables(segments):
    pos = jnp.concatenate([jnp.arange(n, dtype=jnp.int32) for _, n in segments])
    r = (pos // GRID_W).astype(F32)
    c = (pos % GRID_W).astype(F32)
    inv = 1.0 / (ROPE_THETA ** (jnp.arange(0, ROPE_HALF, 2, dtype=F32) / ROPE_HALF))
    ang_r = r[:, None] * inv
    ang_c = c[:, None] * inv
    cos = jnp.concatenate([jnp.cos(ang_r), jnp.cos(ang_r), jnp.cos(ang_c), jnp.cos(ang_c)], axis=-1)
    sin = jnp.concatenate([-jnp.sin(ang_r), jnp.sin(ang_r), -jnp.sin(ang_c), jnp.sin(ang_c)], axis=-1)
    return cos, sin


def _norm_rope(x, gain, cos, sin, upper):
    xn = _rms(x) * gain
    quarter = ROPE_HALF // 2
    partner = jnp.where(upper, pltpu.roll(xn, quarter, axis=1), pltpu.roll(xn, HEAD_DIM - quarter, axis=1))
    return xn * cos + partner * sin


def _qkv_prep_kernel(q_ref, k_ref, v_ref, cos_ref, sin_ref, qg_ref, kg_ref, qt_ref, ko_ref, vt_ref):
    cos = cos_ref[...]
    sin = sin_ref[...]
    rows = cos.shape[0]
    lane = lax.broadcasted_iota(jnp.int32, cos.shape, 1)
    upper = (lane & (ROPE_HALF // 2)) != 0
    q_scale = HEAD_DIM ** -0.5 * LOG2E
    qg = qg_ref[...]
    kg = kg_ref[...]
    for h in range(N_Q_HEADS):
        sl = slice(h * HEAD_DIM, (h + 1) * HEAD_DIM)
        y = _norm_rope(q_ref[:, sl].astype(F32), qg, cos, sin, upper) * q_scale
        qt_ref[sl, :] = y.T.astype(qt_ref.dtype)
    for h in range(N_KV_HEADS):
        sl = slice(h * HEAD_DIM, (h + 1) * HEAD_DIM)
        y = _norm_rope(k_ref[:, sl].astype(F32), kg, cos, sin, upper)
        ko_ref[:, sl] = y.astype(ko_ref.dtype)
        vt_ref[h, 0, :HEAD_DIM, :] = v_ref[:, sl].astype(F32).T.astype(vt_ref.dtype)
        vt_ref[h, 0, HEAD_DIM:, :] = jnp.ones((V_ROWS - HEAD_DIM, rows), vt_ref.dtype)


def qkv_prep(proj, cos, sin, q_gain, k_gain, rows):
    t = proj.shape[0]
    return pl.pallas_call(
        _qkv_prep_kernel,
        grid=(t // rows,),
        in_specs=[pl.BlockSpec((rows, ATTN_WIDTH), lambda i: (i, Q_OFF // ATTN_WIDTH)),
                  pl.BlockSpec((rows, KV_WIDTH), lambda i: (i, K_OFF // KV_WIDTH)),
                  pl.BlockSpec((rows, KV_WIDTH), lambda i: (i, V_OFF // KV_WIDTH)),
                  pl.BlockSpec((rows, HEAD_DIM), lambda i: (i, 0)),
                  pl.BlockSpec((rows, HEAD_DIM), lambda i: (i, 0)),
                  pl.BlockSpec((1, HEAD_DIM), lambda i: (0, 0)),
                  pl.BlockSpec((1, HEAD_DIM), lambda i: (0, 0))],
        out_specs=[pl.BlockSpec((ATTN_WIDTH, rows), lambda i: (0, i)),
                   pl.BlockSpec((rows, KV_WIDTH), lambda i: (i, 0)),
                   pl.BlockSpec((N_KV_HEADS, 1, V_ROWS, rows), lambda i: (0, i, 0, 0))],
        out_shape=[jax.ShapeDtypeStruct((ATTN_WIDTH, t), BF16),
                   jax.ShapeDtypeStruct((t, KV_WIDTH), BF16),
                   jax.ShapeDtypeStruct((N_KV_HEADS, t // rows, V_ROWS, rows), BF16)],
        compiler_params=_params(("parallel",), 40),
        name="qkv_prep",
    )(proj, proj, proj, cos, sin, q_gain.reshape(1, HEAD_DIM), k_gain.reshape(1, HEAD_DIM))


def _attn_kernel(qt_ref, k_ref, vt_ref, *rest, bk, nk):
    o_ref, m_scr, acc_scr, s_scr, smax_scr = rest[-5:]
    m_scr[...] = jnp.full(m_scr.shape, -jnp.inf, F32)
    acc_scr[...] = jnp.zeros(acc_scr.shape, F32)

    def scores(kb, h, slot):
        s = jnp.dot(kb, qt_ref[h * HEAD_DIM:(h + 1) * HEAD_DIM, :], preferred_element_type=F32)
        s_scr[slot] = s
        smax_scr[slot] = jnp.max(s, axis=0, keepdims=True)

    def key_block(j):
        return k_ref[pl.ds(pl.multiple_of(j * bk, bk), bk), :]

    scores(key_block(0), 0, 0)

    def body(j, carry):
        kb = key_block(j)
        kb_next = key_block(jnp.minimum(j + 1, nk - 1))
        vb = vt_ref[j]
        for h in range(Q_PER_KV):
            slot = h % 2
            if h + 1 < Q_PER_KV:
                scores(kb, h + 1, 1 - slot)
            else:
                scores(kb_next, 0, 1 - slot)
            m_prev = m_scr[h]
            m_new = jnp.maximum(m_prev, smax_scr[slot])
            alpha = jnp.exp2(m_prev - m_new)
            p = jnp.exp2(s_scr[slot] - m_new).astype(vb.dtype)
            acc_scr[h] = acc_scr[h] * alpha + jnp.dot(vb, p, preferred_element_type=F32)
            m_scr[h] = m_new
        return carry

    lax.fori_loop(0, nk, body, 0, unroll=4)
    for h in range(Q_PER_KV):
        acc = acc_scr[h]
        o = acc[:HEAD_DIM, :] / acc[HEAD_DIM:HEAD_DIM + 1, :]
        o_ref[:, h * HEAD_DIM:(h + 1) * HEAD_DIM] = o.T.astype(o_ref.dtype)


def _attn_shift_kernel(shift_ref, qt_ref, k_ref, vt_ref, *rest, bk, nk, unroll):
    o_ref, acc_scr, s_scr = rest[-3:]
    shift = shift_ref[0]
    acc_scr[...] = jnp.zeros(acc_scr.shape, F32)

    def scores(kb, h):
        return jnp.dot(kb, qt_ref[h * HEAD_DIM:(h + 1) * HEAD_DIM, :], preferred_element_type=F32)

    def key_block(j):
        return k_ref[pl.ds(pl.multiple_of(j * bk, bk), bk), :]

    s_scr[...] = scores(key_block(0), 0)

    def body(i, carry):
        s = s_scr[...]
        for u in range(unroll):
            j = i * unroll + u
            kb = key_block(j)
            vb = vt_ref[j]
            for h in range(Q_PER_KV):
                if h + 1 < Q_PER_KV:
                    s_next = scores(kb, h + 1)
                else:
                    s_next = scores(key_block(jnp.minimum(j + 1, nk - 1)), 0)
                p = jnp.exp2(s - shift).astype(vb.dtype)
                acc_scr[h] += jnp.dot(vb, p, preferred_element_type=F32)
                s = s_next
        s_scr[...] = s
        return carry

    lax.fori_loop(0, nk // unroll, body, 0)
    for h in range(Q_PER_KV):
        acc = acc_scr[h]
        o = acc[:HEAD_DIM, :] / acc[HEAD_DIM:HEAD_DIM + 1, :]
        o_ref[:, h * HEAD_DIM:(h + 1) * HEAD_DIM] = o.T.astype(o_ref.dtype)


def attention(qt, k_rot, vt, shift, prev_out, row0, batch, seq, use_shift, bq=512):
    t = k_rot.shape[0]
    bk = vt.shape[-1]
    bq = min(bq, seq)
    assert row0 % seq == 0 and seq % bq == 0 and seq % bk == 0
    group_w = Q_PER_KV * HEAD_DIM
    nq = seq // bq
    nk = seq // bk
    in_specs = [
        pl.BlockSpec((group_w, bq), lambda b, h, i: (h, row0 // bq + b * nq + i)),
        pl.BlockSpec((seq, HEAD_DIM), lambda b, h, i: (row0 // seq + b, h)),
        pl.BlockSpec((None, nk, V_ROWS, bk), lambda b, h, i: (h, row0 // seq + b, 0, 0)),
    ]
    args = [qt, k_rot, vt]
    if use_shift:
        unroll = max(u for u in (8, 4, 2, 1) if nk % u == 0)
        kernel = functools.partial(_attn_shift_kernel, bk=bk, nk=nk, unroll=unroll)
        in_specs.insert(0, pl.BlockSpec(memory_space=pltpu.SMEM))
        args.insert(0, shift)
        scratch = [pltpu.VMEM((Q_PER_KV, V_ROWS, bq), F32),
                   pltpu.VMEM((bk, bq), F32)]
    else:
        kernel = functools.partial(_attn_kernel, bk=bk, nk=nk)
        scratch = [pltpu.VMEM((Q_PER_KV, 1, bq), F32),
                   pltpu.VMEM((Q_PER_KV, V_ROWS, bq), F32),
                   pltpu.VMEM((2, bk, bq), F32),
                   pltpu.VMEM((2, 1, bq), F32)]
    aliases = {}
    if prev_out is not None:
        in_specs.append(pl.BlockSpec(memory_space=pl.ANY))
        aliases = {len(args): 0}
        args.append(prev_out)
    return pl.pallas_call(
        kernel,
        grid=(batch, N_KV_HEADS, nq),
        in_specs=in_specs,
        out_specs=pl.BlockSpec((bq, group_w), lambda b, h, i: (row0 // bq + b * nq + i, h)),
        out_shape=jax.ShapeDtypeStruct((t, ATTN_WIDTH), BF16),
        scratch_shapes=scratch,
        input_output_aliases=aliases,
        compiler_params=_params(("parallel", "parallel", "parallel"), 48),
        name="attention_shift" if use_shift else "attention",
    )(*args)


MAX_SOFTMAX_SHIFT = 60.0


def score_bound(q_gain, k_gain):
    bound = 1.01 * HEAD_DIM * (HEAD_DIM ** -0.5 * LOG2E) * jnp.max(jnp.abs(q_gain)) * jnp.max(jnp.abs(k_gain))
    return bound.astype(F32).reshape(1)


def attention_all(qt, k_rot, vt, shift, attn_calls):
    def run(use_shift):
        a_out = None
        for row0, batch, seq in attn_calls:
            a_out = attention(qt, k_rot, vt, shift, a_out, row0, batch, seq, use_shift)
        return a_out

    return lax.cond(shift[0] <= MAX_SOFTMAX_SHIFT, lambda: run(True), lambda: run(False))


def _convpool_kernel(ch_ref, cb_ref, cc_ref, pu_ref,
                     chp_ref, ccp_ref, pup_ref, chn_ref, ccn_ref, pun_ref,
                     cw_ref, pw_ref, ps_ref, o_ref, *, rows, segments):
    i = pl.program_id(0)
    start = i * rows
    pos0 = start
    seq_len = jnp.int32(0)
    has_prev = jnp.bool_(True)
    has_next = jnp.bool_(True)
    for seg_start, seg_len in segments:
        inside = (start >= seg_start) & (start < seg_start + seg_len)
        pos0 = jnp.where(inside, start - seg_start, pos0)
        seq_len = jnp.where(inside, seg_len, seq_len)
        has_prev = has_prev & (start != seg_start)
        has_next = has_next & (start + rows != seg_start + seg_len)
    prev_on = has_prev.astype(F32)
    next_on = has_next.astype(F32)

    def extended(cur_ref, prev_ref, next_ref):
        return jnp.concatenate([prev_ref[...].astype(F32) * prev_on,
                                cur_ref[...].astype(F32),
                                next_ref[...].astype(F32) * next_on], axis=0)

    ext_rows = rows + 2 * HALO

    def shifted(x, d):
        return pltpu.roll(x, d % ext_rows, axis=0)

    u = extended(cc_ref, ccp_ref, ccn_ref) * extended(ch_ref, chp_ref, chn_ref)
    cw = cw_ref[...]
    y = shifted(u, 1) * cw[0:1, :] + u * cw[1:2, :] + shifted(u, -1) * cw[2:3, :]
    o_ref[:, :CONV_WIDTH] = (cb_ref[...].astype(F32) * y[HALO:HALO + rows, :]).astype(o_ref.dtype)

    x = extended(pu_ref, pup_ref, pun_ref)
    pos = (pos0 + lax.broadcasted_iota(jnp.int32, (rows, 1), 0))
    for gi, w in enumerate(POOL_WINDOWS):
        sl = slice(gi * POOL_GROUP, (gi + 1) * POOL_GROUP)
        xg = x[:, sl]
        tot = xg + shifted(xg, 1)
        half = 1
        while 2 * half < w:
            tot = shifted(tot, half) + shifted(tot, -half)
            half *= 2
        hi = jnp.minimum(pos + w // 2, seq_len)
        lo = jnp.maximum(pos - w // 2, 0)
        cnt = (hi - lo).astype(F32)
        m = tot[HALO:HALO + rows, :] / cnt - xg[HALO:HALO + rows, :]
        yg = jnp.dot(m.astype(BF16), pw_ref[gi], preferred_element_type=F32)
        o_ref[:, CONV_WIDTH + gi * POOL_GROUP:CONV_WIDTH + (gi + 1) * POOL_GROUP] = (
            yg * ps_ref[:, sl]).astype(o_ref.dtype)


def conv_pool(proj, conv_w, pool_w, pool_scale, segments, rows=512):
    t = proj.shape[0]
    rows = min(rows, min(n for _, n in segments))
    assert all(s % rows == 0 and n % rows == 0 for s, n in segments)
    w = CONV_WIDTH
    per = rows // HALO
    last = t // HALO - 1

    def cur(off):
        return pl.BlockSpec((rows, w), lambda i: (i, off // w))

    def prev(off):
        return pl.BlockSpec((HALO, w), lambda i: (jnp.maximum(i * per - 1, 0), off // w))

    def nxt(off):
        return pl.BlockSpec((HALO, w), lambda i: (jnp.minimum((i + 1) * per, last), off // w))

    kernel = functools.partial(_convpool_kernel, rows=rows, segments=segments)
    return pl.pallas_call(
        kernel,
        grid=(t // rows,),
        in_specs=[cur(CH_OFF), cur(CB_OFF), cur(CC_OFF), cur(PU_OFF),
                  prev(CH_OFF), prev(CC_OFF), prev(PU_OFF),
                  nxt(CH_OFF), nxt(CC_OFF), nxt(PU_OFF),
                  pl.BlockSpec((3, w), lambda i: (0, 0)),
                  pl.BlockSpec((len(POOL_WINDOWS), POOL_GROUP, POOL_GROUP), lambda i: (0, 0, 0)),
                  pl.BlockSpec((1, POOL_WIDTH), lambda i: (0, 0))],
        out_specs=pl.BlockSpec((rows, CONV_WIDTH + POOL_WIDTH), lambda i: (i, 0)),
        out_shape=jax.ShapeDtypeStruct((t, CONV_WIDTH + POOL_WIDTH), BF16),
        compiler_params=_params(("parallel",), 48),
        name="conv_pool",
    )(proj, proj, proj, proj, proj, proj, proj, proj, proj, proj,
      conv_w, pool_w, pool_scale.reshape(1, POOL_WIDTH))


def _trunk(x_parts, segments, attn_calls, ln_mix_pre, ln_mix_post, ln_ffn_pre, ln_ffn_post, q_norm, k_norm,
           w_in, conv_w, pool_w, pool_scale, w_out, w_gate, w_up, w_down):
    total = sum(n for _, _, n in x_parts)
    cos, sin = _rope_tables(segments)
    key_block = min(256, min(n for _, n in segments))
    h = norm_cast_parts(x_parts, ln_mix_pre[0], total)
    x = None
    for l in range(DEPTH):
        proj = matmul(h, cast_weight(w_in, l), BF16)
        qt, k_rot, vt = qkv_prep(proj, cos, sin, q_norm[l], k_norm[l], rows=key_block)
        a_out = attention_all(qt, k_rot, vt, score_bound(q_norm[l], k_norm[l]), attn_calls)
        cp_out = conv_pool(proj, conv_w[l], pool_w[l].astype(BF16), pool_scale[l], segments)
        mix = matmul_two_lhs(a_out, cp_out, cast_weight(w_out, l))
        if x is None:
            x, h = resnorm_from_parts(mix, x_parts, ln_mix_post[l], ln_ffn_pre[l], total)
        else:
            x, h = resnorm(mix, x, ln_mix_post[l], ln_ffn_pre[l])
        act = gate_up(h, cast_weight(w_gate, l, out_cols=D_FF_PAD), cast_weight(w_up, l, out_cols=D_FF_PAD))
        f = matmul_split_k(act, cast_weight(w_down, l, out_rows=D_FF_PAD), bk=D_FF_PAD // 4)
        if l + 1 < DEPTH:
            x, h = resnorm(f, x, ln_ffn_post[l], ln_mix_pre[l + 1])
    return resnorm_last_to_parts(f, x, ln_ffn_post[DEPTH - 1], [(row0, n) for _, row0, n in x_parts])


def kernel(x_prompt, x_sample, ln_mix_pre, ln_mix_post, ln_ffn_pre, ln_ffn_post, q_norm, k_norm,
           w_in, conv_w, pool_w, pool_scale, w_out, w_gate, w_up, w_down):
    pb, ps, d = x_prompt.shape
    sb, ss, _ = x_sample.shape
    x_parts = [(x_prompt.reshape(pb * ps, d), 0, pb * ps), (x_sample.reshape(sb * ss, d), pb * ps, sb * ss)]
    segments = tuple((b * ps, ps) for b in range(pb)) + tuple((pb * ps + b * ss, ss) for b in range(sb))
    attn_calls = ((0, pb, ps), (pb * ps, sb, ss))
    y_prompt, y_sample = _trunk(x_parts, segments, attn_calls, ln_mix_pre, ln_mix_post, ln_ffn_pre, ln_ffn_post,
                                q_norm, k_norm, w_in, conv_w, pool_w, pool_scale, w_out, w_gate, w_up, w_down)
    return (y_prompt.reshape(pb, ps, d), y_sample.reshape(sb, ss, d))
```

```python
import functools

import jax
import jax.numpy as jnp
from jax import lax
from jax.experimental import pallas as pl
from jax.experimental.pallas import tpu as pltpu

D_MODEL = 4096
DEPTH = 2
GRID_W = 64
HEAD_DIM = 128
N_Q_HEADS = 16
N_KV_HEADS = 4
Q_PER_KV = N_Q_HEADS // N_KV_HEADS
ATTN_WIDTH = N_Q_HEADS * HEAD_DIM
KV_WIDTH = N_KV_HEADS * HEAD_DIM
CONV_WIDTH = D_MODEL // 4
POOL_WIDTH = D_MODEL // 4
POOL_WINDOWS = (2, 4, 8, 16)
POOL_GROUP = POOL_WIDTH // len(POOL_WINDOWS)
IN_WIDTH = ATTN_WIDTH + 2 * KV_WIDTH + 3 * CONV_WIDTH + POOL_WIDTH
D_FF = 11008
D_FF_PAD = 11264
ROPE_HALF = HEAD_DIM // 2
ROPE_THETA = 10000.0
EPS = 1e-6

Q_OFF = 0
K_OFF = ATTN_WIDTH
V_OFF = K_OFF + KV_WIDTH
CH_OFF = V_OFF + KV_WIDTH
CB_OFF = CH_OFF + CONV_WIDTH
CC_OFF = CB_OFF + CONV_WIDTH
PU_OFF = CC_OFF + CONV_WIDTH

HALO = 16
V7X_VMEM_BYTES = 64 * 1024 * 1024

F32 = jnp.float32
BF16 = jnp.bfloat16


def _params(semantics, vmem_mb):
    assert vmem_mb * 1024 * 1024 < V7X_VMEM_BYTES
    return pltpu.CompilerParams(dimension_semantics=semantics,
                                vmem_limit_bytes=vmem_mb * 1024 * 1024)


def _rms(x):
    return x * lax.rsqrt(jnp.mean(x * x, axis=-1, keepdims=True) + EPS)


def _rows_spec(rows, d, row0):
    assert row0 % rows == 0
    return pl.BlockSpec((rows, d), lambda i: (row0 // rows + i, 0))


def _rowwise_call(body, name, n_rows, ins, gains, outs, prev_outs, rows=256):
    d = ins[0][0].shape[1]
    in_specs = [_rows_spec(rows, d, r0) for _, r0 in ins]
    in_specs += [pl.BlockSpec((1, d), lambda i: (0, 0)) for _ in gains]
    args = [a for a, _ in ins] + [g.reshape(1, d) for g in gains]
    aliases = {}
    kernel = body
    if prev_outs is not None:
        n_in = len(args)
        in_specs += [pl.BlockSpec(memory_space=pl.ANY) for _ in prev_outs]
        args += list(prev_outs)
        aliases = {n_in + k: k for k in range(len(prev_outs))}

        def kernel(*refs):
            body(*refs[:n_in], *refs[n_in + len(prev_outs):])

    return pl.pallas_call(
        kernel,
        grid=(n_rows // rows,),
        in_specs=in_specs,
        out_specs=[_rows_spec(rows, d, r0) for _, r0, _ in outs],
        out_shape=[jax.ShapeDtypeStruct((total, d), dt) for total, _, dt in outs],
        input_output_aliases=aliases,
        compiler_params=_params(("parallel",), 40),
        name=name,
    )(*args)


def _norm_cast_kernel(x_ref, g_ref, h_ref):
    h_ref[...] = (_rms(x_ref[...]) * g_ref[...]).astype(h_ref.dtype)


def _resnorm_kernel(f_ref, x_ref, g_post_ref, g_next_ref, xo_ref, h_ref):
    xn = x_ref[...] + _rms(f_ref[...]) * g_post_ref[...]
    xo_ref[...] = xn
    h_ref[...] = (_rms(xn) * g_next_ref[...]).astype(h_ref.dtype)


def _resnorm_last_kernel(f_ref, x_ref, g_post_ref, xo_ref):
    xo_ref[...] = x_ref[...] + _rms(f_ref[...]) * g_post_ref[...]


def norm_cast_parts(parts, g, total):
    h = None
    for x, row0, n in parts:
        h, = _rowwise_call(_norm_cast_kernel, "norm_cast", n, [(x, 0)], [g], [(total, row0, BF16)],
                           None if h is None else [h])
    return h


def resnorm_from_parts(f, parts, g_post, g_next, total):
    outs = None
    for x, row0, n in parts:
        outs = _rowwise_call(_resnorm_kernel, "resnorm", n, [(f, row0), (x, 0)], [g_post, g_next],
                             [(total, row0, F32), (total, row0, BF16)], outs)
    return outs


def resnorm(f, x, g_post, g_next):
    total = x.shape[0]
    return _rowwise_call(_resnorm_kernel, "resnorm", total, [(f, 0), (x, 0)], [g_post, g_next],
                         [(total, 0, F32), (total, 0, BF16)], None)


def resnorm_last_to_parts(f, x, g_post, parts):
    return [_rowwise_call(_resnorm_last_kernel, "resnorm_last", n, [(f, row0), (x, row0)], [g_post],
                          [(n, 0, F32)], None)[0] for row0, n in parts]


WS_SEMANTICS = ("arbitrary", "arbitrary")


def _mm_ws_kernel(a_ref, w_ref, o_ref, wb_scr):
    @pl.when(pl.program_id(1) == 0)
    def _():
        wb_scr[...] = w_ref[...].astype(wb_scr.dtype)

    o_ref[...] = jnp.dot(a_ref[...], wb_scr[...], preferred_element_type=F32).astype(o_ref.dtype)


def matmul(a, w, layer, out_dtype, bm=1024, bn=512):
    m, k = a.shape
    _, _, n = w.shape
    bm = min(bm, m)
    return pl.pallas_call(
        _mm_ws_kernel,
        grid=(n // bn, m // bm),
        in_specs=[pl.BlockSpec((bm, k), lambda j, i: (i, 0)),
                  pl.BlockSpec((None, k, bn), lambda j, i: (layer, 0, j))],
        out_specs=pl.BlockSpec((bm, bn), lambda j, i: (i, j)),
        out_shape=jax.ShapeDtypeStruct((m, n), out_dtype),
        scratch_shapes=[pltpu.VMEM((k, bn), BF16)],
        compiler_params=_params(WS_SEMANTICS, 48),
        name="matmul_full_k",
    )(a, w)


def _mm2_ws_kernel(a1_ref, a2_ref, w1_ref, w2_ref, o_ref, w1_scr, w2_scr):
    @pl.when(pl.program_id(1) == 0)
    def _():
        w1_scr[...] = w1_ref[...].astype(w1_scr.dtype)
        w2_scr[...] = w2_ref[...].astype(w2_scr.dtype)

    o_ref[...] = (jnp.dot(a1_ref[...], w1_scr[...], preferred_element_type=F32)
                  + jnp.dot(a2_ref[...], w2_scr[...], preferred_element_type=F32))


def matmul_two_lhs(a1, a2, w, layer, bm=1024, bn=512):
    m, k1 = a1.shape
    _, k2 = a2.shape
    assert k1 == k2 and w.shape[1] == k1 + k2
    n = w.shape[2]
    bm = min(bm, m)
    return pl.pallas_call(
        _mm2_ws_kernel,
        grid=(n // bn, m // bm),
        in_specs=[pl.BlockSpec((bm, k1), lambda j, i: (i, 0)),
                  pl.BlockSpec((bm, k2), lambda j, i: (i, 0)),
                  pl.BlockSpec((None, k1, bn), lambda j, i: (layer, 0, j)),
                  pl.BlockSpec((None, k2, bn), lambda j, i: (layer, 1, j))],
        out_specs=pl.BlockSpec((bm, bn), lambda j, i: (i, j)),
        out_shape=jax.ShapeDtypeStruct((m, n), F32),
        scratch_shapes=[pltpu.VMEM((k1, bn), BF16), pltpu.VMEM((k2, bn), BF16)],
        compiler_params=_params(WS_SEMANTICS, 48),
        name="matmul_two_lhs",
    )(a1, a2, w, w)


def _mm_acc_kernel(a_ref, b_ref, o_ref):
    part = jnp.dot(a_ref[...], b_ref[...], preferred_element_type=F32)

    @pl.when(pl.program_id(2) == 0)
    def _():
        o_ref[...] = part

    @pl.when(pl.program_id(2) != 0)
    def _():
        o_ref[...] += part


def matmul_split_k(a, b, bk, bm=1024, bn=1024):
    m, k = a.shape
    _, n = b.shape
    bm = min(bm, m)
    return pl.pallas_call(
        _mm_acc_kernel,
        grid=(m // bm, n // bn, k // bk),
        in_specs=[pl.BlockSpec((bm, bk), lambda i, j, kk: (i, kk)),
                  pl.BlockSpec((bk, bn), lambda i, j, kk: (kk, j))],
        out_specs=pl.BlockSpec((bm, bn), lambda i, j, kk: (i, j)),
        out_shape=jax.ShapeDtypeStruct((m, n), F32),
        compiler_params=_params(("parallel", "parallel", "arbitrary"), 48),
        name="matmul_split_k",
    )(a, b)


DOWN_CAST_ROW_CHOICES = (128, 256)


def _gateup_ws_kernel(h_ref, wg_ref, wu_ref, wd_ref, o_ref, wd_o_ref, wg_scr, wu_scr, *,
                      n_tiles, n_i, wd_blocks, wd_out_blocks):
    j = pl.program_id(0)
    i = pl.program_id(1)

    @pl.when((i == 0) & (j < n_tiles))
    def _():
        wg_scr[...] = wg_ref[...].astype(wg_scr.dtype)
        wu_scr[...] = wu_ref[...].astype(wu_scr.dtype)

    @pl.when(j < n_tiles)
    def _():
        h = h_ref[...]
        g = jnp.dot(h, wg_scr[...], preferred_element_type=F32)
        u = jnp.dot(h, wu_scr[...], preferred_element_type=F32)
        o_ref[...] = (g * jax.nn.sigmoid(g) * u).astype(o_ref.dtype)

    @pl.when(j >= n_tiles)
    def _():
        o_ref[...] = jnp.zeros(o_ref.shape, o_ref.dtype)

    step = j * n_i + i

    @pl.when(step < wd_blocks)
    def _():
        wd_o_ref[...] = wd_ref[...].astype(wd_o_ref.dtype)

    @pl.when((step >= wd_blocks) & (step < wd_out_blocks))
    def _():
        wd_o_ref[...] = jnp.zeros(wd_o_ref.shape, wd_o_ref.dtype)


def gate_up(h, w_gate, w_up, w_down, layer, out_cols, bm=1024, bn=256):
    m, k = h.shape
    _, _, n = w_gate.shape
    _, _, d_out = w_down.shape
    bm = min(bm, m)
    n_tiles = n // bn
    n_i = m // bm
    n_steps = (out_cols // bn) * n_i
    r = next(c for c in DOWN_CAST_ROW_CHOICES if n % c == 0 and out_cols % c == 0 and out_cols // c <= n_steps)
    wd_blocks, wd_out_blocks = n // r, out_cols // r
    step = lambda j, i: j * n_i + i
    return pl.pallas_call(
        functools.partial(_gateup_ws_kernel, n_tiles=n_tiles, n_i=n_i,
                          wd_blocks=wd_blocks, wd_out_blocks=wd_out_blocks),
        grid=(out_cols // bn, n_i),
        in_specs=[pl.BlockSpec((bm, k), lambda j, i: (i, 0)),
                  pl.BlockSpec((None, k, bn), lambda j, i: (layer, 0, jnp.minimum(j, n_tiles - 1))),
                  pl.BlockSpec((None, k, bn), lambda j, i: (layer, 0, jnp.minimum(j, n_tiles - 1))),
                  pl.BlockSpec((None, r, d_out), lambda j, i: (layer, jnp.minimum(step(j, i), wd_blocks - 1), 0))],
        out_specs=[pl.BlockSpec((bm, bn), lambda j, i: (i, j)),
                   pl.BlockSpec((r, d_out), lambda j, i: (jnp.minimum(step(j, i), wd_out_blocks - 1), 0))],
        out_shape=[jax.ShapeDtypeStruct((m, out_cols), BF16),
                   jax.ShapeDtypeStruct((out_cols, d_out), BF16)],
        scratch_shapes=[pltpu.VMEM((k, bn), BF16), pltpu.VMEM((k, bn), BF16)],
        compiler_params=_params(WS_SEMANTICS, 48),
        name="gate_up",
    )(h, w_gate, w_up, w_down)


V_ROWS = HEAD_DIM + 16
LOG2E = 1.4426950408889634


def _rope_tables(segments):
    pos = jnp.concatenate([jnp.arange(n, dtype=jnp.int32) for _, n in segments])
    r = (pos // GRID_W).astype(F32)
    c = (pos % GRID_W).astype(F32)
    inv = 1.0 / (ROPE_THETA ** (jnp.arange(0, ROPE_HALF, 2, dtype=F32) / ROPE_HALF))
    ang_r = r[:, None] * inv
    ang_c = c[:, None] * inv
    cos = jnp.concatenate([jnp.cos(ang_r), jnp.cos(ang_r), jnp.cos(ang_c), jnp.cos(ang_c)], axis=-1)
    sin = jnp.concatenate([-jnp.sin(ang_r), jnp.sin(ang_r), -jnp.sin(ang_c), jnp.sin(ang_c)], axis=-1)
    return cos, sin


def _norm_rope(x, gain, cos, sin, upper):
    xn = _rms(x) * gain
    quarter = ROPE_HALF // 2
    partner = jnp.where(upper, pltpu.roll(xn, quarter, axis=1), pltpu.roll(xn, HEAD_DIM - quarter, axis=1))
    return xn * cos + partner * sin


def _qkv_prep_kernel(q_ref, k_ref, v_ref, cos_ref, sin_ref, qg_ref, kg_ref, qt_ref, ko_ref, vt_ref):
    cos = cos_ref[...]
    sin = sin_ref[...]
    rows = cos.shape[0]
    lane = lax.broadcasted_iota(jnp.int32, cos.shape, 1)
    upper = (lane & (ROPE_HALF // 2)) != 0
    q_scale = HEAD_DIM ** -0.5 * LOG2E
    qg = qg_ref[...]
    kg = kg_ref[...]
    for h in range(N_Q_HEADS):
        sl = slice(h * HEAD_DIM, (h + 1) * HEAD_DIM)
        y = _norm_rope(q_ref[:, sl].astype(F32), qg, cos, sin, upper) * q_scale
        qt_ref[sl, :] = y.T.astype(qt_ref.dtype)
    for h in range(N_KV_HEADS):
        sl = slice(h * HEAD_DIM, (h + 1) * HEAD_DIM)
        y = _norm_rope(k_ref[:, sl].astype(F32), kg, cos, sin, upper)
        ko_ref[:, sl] = y.astype(ko_ref.dtype)
        vt_ref[h, 0, :HEAD_DIM, :] = v_ref[:, sl].astype(F32).T.astype(vt_ref.dtype)
        vt_ref[h, 0, HEAD_DIM:, :] = jnp.ones((V_ROWS - HEAD_DIM, rows), vt_ref.dtype)


def qkv_prep(proj, cos, sin, q_gain, k_gain, rows):
    t = proj.shape[0]
    return pl.pallas_call(
        _qkv_prep_kernel,
        grid=(t // rows,),
        in_specs=[pl.BlockSpec((rows, ATTN_WIDTH), lambda i: (i, Q_OFF // ATTN_WIDTH)),
                  pl.BlockSpec((rows, KV_WIDTH), lambda i: (i, K_OFF // KV_WIDTH)),
                  pl.BlockSpec((rows, KV_WIDTH), lambda i: (i, V_OFF // KV_WIDTH)),
                  pl.BlockSpec((rows, HEAD_DIM), lambda i: (i, 0)),
                  pl.BlockSpec((rows, HEAD_DIM), lambda i: (i, 0)),
                  pl.BlockSpec((1, HEAD_DIM), lambda i: (0, 0)),
                  pl.BlockSpec((1, HEAD_DIM), lambda i: (0, 0))],
        out_specs=[pl.BlockSpec((ATTN_WIDTH, rows), lambda i: (0, i)),
                   pl.BlockSpec((rows, KV_WIDTH), lambda i: (i, 0)),
                   pl.BlockSpec((N_KV_HEADS, 1, V_ROWS, rows), lambda i: (0, i, 0, 0))],
        out_shape=[jax.ShapeDtypeStruct((ATTN_WIDTH, t), BF16),
                   jax.ShapeDtypeStruct((t, KV_WIDTH), BF16),
                   jax.ShapeDtypeStruct((N_KV_HEADS, t // rows, V_ROWS, rows), BF16)],
        compiler_params=_params(("parallel",), 40),
        name="qkv_prep",
    )(proj, proj, proj, cos, sin, q_gain.reshape(1, HEAD_DIM), k_gain.reshape(1, HEAD_DIM))


def _attn_kernel(qt_ref, k_ref, vt_ref, *rest, bk, nk):
    o_ref, m_scr, acc_scr, s_scr, smax_scr = rest[-5:]
    m_scr[...] = jnp.full(m_scr.shape, -jnp.inf, F32)
    acc_scr[...] = jnp.zeros(acc_scr.shape, F32)

    def scores(kb, h, slot):
        s = jnp.dot(kb, qt_ref[h * HEAD_DIM:(h + 1) * HEAD_DIM, :], preferred_element_type=F32)
        s_scr[slot] = s
        smax_scr[slot] = jnp.max(s, axis=0, keepdims=True)

    def key_block(j):
        return k_ref[pl.ds(pl.multiple_of(j * bk, bk), bk), :]

    scores(key_block(0), 0, 0)

    def body(j, carry):
        kb = key_block(j)
        kb_next = key_block(jnp.minimum(j + 1, nk - 1))
        vb = vt_ref[j]
        for h in range(Q_PER_KV):
            slot = h % 2
            if h + 1 < Q_PER_KV:
                scores(kb, h + 1, 1 - slot)
            else:
                scores(kb_next, 0, 1 - slot)
            m_prev = m_scr[h]
            m_new = jnp.maximum(m_prev, smax_scr[slot])
            alpha = jnp.exp2(m_prev - m_new)
            p = jnp.exp2(s_scr[slot] - m_new).astype(vb.dtype)
            acc_scr[h] = acc_scr[h] * alpha + jnp.dot(vb, p, preferred_element_type=F32)
            m_scr[h] = m_new
        return carry

    lax.fori_loop(0, nk, body, 0, unroll=4)
    for h in range(Q_PER_KV):
        acc = acc_scr[h]
        o = acc[:HEAD_DIM, :] / acc[HEAD_DIM:HEAD_DIM + 1, :]
        o_ref[:, h * HEAD_DIM:(h + 1) * HEAD_DIM] = o.T.astype(o_ref.dtype)


def attention(qt, k_rot, vt, prev_out, row0, batch, seq, bq=512):
    t = k_rot.shape[0]
    bk = vt.shape[-1]
    bq = min(bq, seq)
    assert row0 % seq == 0 and seq % bq == 0 and seq % bk == 0
    group_w = Q_PER_KV * HEAD_DIM
    nq = seq // bq
    nk = seq // bk
    in_specs = [
        pl.BlockSpec((group_w, bq), lambda b, h, i: (h, row0 // bq + b * nq + i)),
        pl.BlockSpec((seq, HEAD_DIM), lambda b, h, i: (row0 // seq + b, h)),
        pl.BlockSpec((None, nk, V_ROWS, bk), lambda b, h, i: (h, row0 // seq + b, 0, 0)),
    ]
    args = [qt, k_rot, vt]
    aliases = {}
    if prev_out is not None:
        in_specs.append(pl.BlockSpec(memory_space=pl.ANY))
        args.append(prev_out)
        aliases = {3: 0}
    return pl.pallas_call(
        functools.partial(_attn_kernel, bk=bk, nk=nk),
        grid=(batch, N_KV_HEADS, nq),
        in_specs=in_specs,
        out_specs=pl.BlockSpec((bq, group_w), lambda b, h, i: (row0 // bq + b * nq + i, h)),
        out_shape=jax.ShapeDtypeStruct((t, ATTN_WIDTH), BF16),
        scratch_shapes=[pltpu.VMEM((Q_PER_KV, 1, bq), F32),
                        pltpu.VMEM((Q_PER_KV, V_ROWS, bq), F32),
                        pltpu.VMEM((2, bk, bq), F32),
                        pltpu.VMEM((2, 1, bq), F32)],
        input_output_aliases=aliases,
        compiler_params=_params(("parallel", "parallel", "parallel"), 48),
        name="attention",
    )(*args)


def _convpool_kernel(ch_ref, cb_ref, cc_ref, pu_ref,
                     chp_ref, ccp_ref, pup_ref, chn_ref, ccn_ref, pun_ref,
                     cw_ref, pw_ref, ps_ref, o_ref, *, rows, segments):
    i = pl.program_id(0)
    start = i * rows
    pos0 = start
    seq_len = jnp.int32(0)
    has_prev = jnp.bool_(True)
    has_next = jnp.bool_(True)
    for seg_start, seg_len in segments:
        inside = (start >= seg_start) & (start < seg_start + seg_len)
        pos0 = jnp.where(inside, start - seg_start, pos0)
        seq_len = jnp.where(inside, seg_len, seq_len)
        has_prev = has_prev & (start != seg_start)
        has_next = has_next & (start + rows != seg_start + seg_len)
    prev_on = has_prev.astype(F32)
    next_on = has_next.astype(F32)

    def extended(cur_ref, prev_ref, next_ref):
        return jnp.concatenate([prev_ref[...].astype(F32) * prev_on,
                                cur_ref[...].astype(F32),
                                next_ref[...].astype(F32) * next_on], axis=0)

    ext_rows = rows + 2 * HALO

    def shifted(x, d):
        return pltpu.roll(x, d % ext_rows, axis=0)

    u = extended(cc_ref, ccp_ref, ccn_ref) * extended(ch_ref, chp_ref, chn_ref)
    cw = cw_ref[...]
    y = shifted(u, 1) * cw[0:1, :] + u * cw[1:2, :] + shifted(u, -1) * cw[2:3, :]
    o_ref[:, :CONV_WIDTH] = (cb_ref[...].astype(F32) * y[HALO:HALO + rows, :]).astype(o_ref.dtype)

    x = extended(pu_ref, pup_ref, pun_ref)
    pos = (pos0 + lax.broadcasted_iota(jnp.int32, (rows, 1), 0))
    for gi, w in enumerate(POOL_WINDOWS):
        sl = slice(gi * POOL_GROUP, (gi + 1) * POOL_GROUP)
        xg = x[:, sl]
        tot = xg + shifted(xg, 1)
        half = 1
        while 2 * half < w:
            tot = shifted(tot, half) + shifted(tot, -half)
            half *= 2
        hi = jnp.minimum(pos + w // 2, seq_len)
        lo = jnp.maximum(pos - w // 2, 0)
        cnt = (hi - lo).astype(F32)
        m = tot[HALO:HALO + rows, :] / cnt - xg[HALO:HALO + rows, :]
        yg = jnp.dot(m.astype(BF16), pw_ref[gi], preferred_element_type=F32)
        o_ref[:, CONV_WIDTH + gi * POOL_GROUP:CONV_WIDTH + (gi + 1) * POOL_GROUP] = (
            yg * ps_ref[:, sl]).astype(o_ref.dtype)


def conv_pool(proj, conv_w, pool_w, pool_scale, segments, rows=512):
    t = proj.shape[0]
    rows = min(rows, min(n for _, n in segments))
    assert all(s % rows == 0 and n % rows == 0 for s, n in segments)
    w = CONV_WIDTH
    per = rows // HALO
    last = t // HALO - 1

    def cur(off):
        return pl.BlockSpec((rows, w), lambda i: (i, off // w))

    def prev(off):
        return pl.BlockSpec((HALO, w), lambda i: (jnp.maximum(i * per - 1, 0), off // w))

    def nxt(off):
        return pl.BlockSpec((HALO, w), lambda i: (jnp.minimum((i + 1) * per, last), off // w))

    kernel = functools.partial(_convpool_kernel, rows=rows, segments=segments)
    return pl.pallas_call(
        kernel,
        grid=(t // rows,),
        in_specs=[cur(CH_OFF), cur(CB_OFF), cur(CC_OFF), cur(PU_OFF),
                  prev(CH_OFF), prev(CC_OFF), prev(PU_OFF),
                  nxt(CH_OFF), nxt(CC_OFF), nxt(PU_OFF),
                  pl.BlockSpec((3, w), lambda i: (0, 0)),
                  pl.BlockSpec((len(POOL_WINDOWS), POOL_GROUP, POOL_GROUP), lambda i: (0, 0, 0)),
                  pl.BlockSpec((1, POOL_WIDTH), lambda i: (0, 0))],
        out_specs=pl.BlockSpec((rows, CONV_WIDTH + POOL_WIDTH), lambda i: (i, 0)),
        out_shape=jax.ShapeDtypeStruct((t, CONV_WIDTH + POOL_WIDTH), BF16),
        compiler_params=_params(("parallel",), 48),
        name="conv_pool",
    )(proj, proj, proj, proj, proj, proj, proj, proj, proj, proj,
      conv_w, pool_w, pool_scale.reshape(1, POOL_WIDTH))


def _trunk(x_parts, segments, attn_calls, ln_mix_pre, ln_mix_post, ln_ffn_pre, ln_ffn_post, q_norm, k_norm,
           w_in, conv_w, pool_w, pool_scale, w_out, w_gate, w_up, w_down):
    total = sum(n for _, _, n in x_parts)
    cos, sin = _rope_tables(segments)
    key_block = min(512, min(n for _, n in segments))
    h = norm_cast_parts(x_parts, ln_mix_pre[0], total)
    x = None
    for l in range(DEPTH):
        proj = matmul(h, w_in, l, BF16)
        qt, k_rot, vt = qkv_prep(proj, cos, sin, q_norm[l], k_norm[l], rows=key_block)
        a_out = None
        for row0, batch, seq in attn_calls:
            a_out = attention(qt, k_rot, vt, a_out, row0, batch, seq)
        cp_out = conv_pool(proj, conv_w[l], pool_w[l].astype(BF16), pool_scale[l], segments)
        mix = matmul_two_lhs(a_out, cp_out, w_out, l)
        if x is None:
            x, h = resnorm_from_parts(mix, x_parts, ln_mix_post[l], ln_ffn_pre[l], total)
        else:
            x, h = resnorm(mix, x, ln_mix_post[l], ln_ffn_pre[l])
        act, w_down_bf16 = gate_up(h, w_gate, w_up, w_down, l, out_cols=D_FF_PAD)
        f = matmul_split_k(act, w_down_bf16, bk=D_FF_PAD // 4)
        if l + 1 < DEPTH:
            x, h = resnorm(f, x, ln_ffn_post[l], ln_mix_pre[l + 1])
    return resnorm_last_to_parts(f, x, ln_ffn_post[DEPTH - 1], [(row0, n) for _, row0, n in x_parts])


def kernel(x_prompt, x_sample, ln_mix_pre, ln_mix_post, ln_ffn_pre, ln_ffn_post, q_norm, k_norm,
           w_in, conv_w, pool_w, pool_scale, w_out, w_gate, w_up, w_down):
    pb, ps, d = x_prompt.shape
    sb, ss, _ = x_sample.shape
    x_parts = [(x_prompt.reshape(pb * ps, d), 0, pb * ps), (x_sample.reshape(sb * ss, d), pb * ps, sb * ss)]
    segments = tuple((b * ps, ps) for b in range(pb)) + tuple((pb * ps + b * ss, ss) for b in range(sb))
    attn_calls = ((0, pb, ps), (pb * ps, sb, ss))
    y_prompt, y_sample = _trunk(x_parts, segments, attn_calls, ln_mix_pre, ln_mix_post, ln_ffn_pre, ln_ffn_post,
                                q_norm, k_norm, w_in, conv_w, pool_w, pool_scale, w_out, w_gate, w_up, w_down)
    return (y_prompt.reshape(pb, ps, d), y_sample.reshape(sb, ss, d))
```

```python
import functools

import jax
import jax.numpy as jnp
from jax import lax
from jax.experimental import pallas as pl
from jax.experimental.pallas import tpu as pltpu

D_MODEL = 4096
DEPTH = 2
GRID_W = 64
HEAD_DIM = 128
N_Q_HEADS = 16
N_KV_HEADS = 4
Q_PER_KV = N_Q_HEADS // N_KV_HEADS
ATTN_WIDTH = N_Q_HEADS * HEAD_DIM
KV_WIDTH = N_KV_HEADS * HEAD_DIM
CONV_WIDTH = D_MODEL // 4
POOL_WIDTH = D_MODEL // 4
POOL_WINDOWS = (2, 4, 8, 16)
POOL_GROUP = POOL_WIDTH // len(POOL_WINDOWS)
IN_WIDTH = ATTN_WIDTH + 2 * KV_WIDTH + 3 * CONV_WIDTH + POOL_WIDTH
D_FF = 11008
D_FF_PAD = 11264
ROPE_HALF = HEAD_DIM // 2
ROPE_THETA = 10000.0
EPS = 1e-6

Q_OFF = 0
K_OFF = ATTN_WIDTH
V_OFF = K_OFF + KV_WIDTH
CH_OFF = V_OFF + KV_WIDTH
CB_OFF = CH_OFF + CONV_WIDTH
CC_OFF = CB_OFF + CONV_WIDTH
PU_OFF = CC_OFF + CONV_WIDTH

HALO = 16
V7X_VMEM_BYTES = 64 * 1024 * 1024

F32 = jnp.float32
BF16 = jnp.bfloat16


def _params(semantics, vmem_mb):
    assert vmem_mb * 1024 * 1024 < V7X_VMEM_BYTES
    return pltpu.CompilerParams(dimension_semantics=semantics,
                                vmem_limit_bytes=vmem_mb * 1024 * 1024)


def _rms(x):
    return x * lax.rsqrt(jnp.mean(x * x, axis=-1, keepdims=True) + EPS)


def _rows_spec(rows, d, row0):
    assert row0 % rows == 0
    return pl.BlockSpec((rows, d), lambda i: (row0 // rows + i, 0))


def _rowwise_call(body, name, n_rows, ins, gains, outs, prev_outs, rows=256):
    d = ins[0][0].shape[1]
    in_specs = [_rows_spec(rows, d, r0) for _, r0 in ins]
    in_specs += [pl.BlockSpec((1, d), lambda i: (0, 0)) for _ in gains]
    args = [a for a, _ in ins] + [g.reshape(1, d) for g in gains]
    aliases = {}
    kernel = body
    if prev_outs is not None:
        n_in = len(args)
        in_specs += [pl.BlockSpec(memory_space=pl.ANY) for _ in prev_outs]
        args += list(prev_outs)
        aliases = {n_in + k: k for k in range(len(prev_outs))}

        def kernel(*refs):
            body(*refs[:n_in], *refs[n_in + len(prev_outs):])

    return pl.pallas_call(
        kernel,
        grid=(n_rows // rows,),
        in_specs=in_specs,
        out_specs=[_rows_spec(rows, d, r0) for _, r0, _ in outs],
        out_shape=[jax.ShapeDtypeStruct((total, d), dt) for total, _, dt in outs],
        input_output_aliases=aliases,
        compiler_params=_params(("parallel",), 40),
        name=name,
    )(*args)


def _norm_cast_kernel(x_ref, g_ref, h_ref):
    h_ref[...] = (_rms(x_ref[...]) * g_ref[...]).astype(h_ref.dtype)


def _resnorm_kernel(f_ref, x_ref, g_post_ref, g_next_ref, xo_ref, h_ref):
    xn = x_ref[...] + _rms(f_ref[...]) * g_post_ref[...]
    xo_ref[...] = xn
    h_ref[...] = (_rms(xn) * g_next_ref[...]).astype(h_ref.dtype)


def _resnorm_last_kernel(f_ref, x_ref, g_post_ref, xo_ref):
    xo_ref[...] = x_ref[...] + _rms(f_ref[...]) * g_post_ref[...]


def norm_cast_parts(parts, g, total):
    h = None
    for x, row0, n in parts:
        h, = _rowwise_call(_norm_cast_kernel, "norm_cast", n, [(x, 0)], [g], [(total, row0, BF16)],
                           None if h is None else [h])
    return h


def resnorm_from_parts(f, parts, g_post, g_next, total):
    outs = None
    for x, row0, n in parts:
        outs = _rowwise_call(_resnorm_kernel, "resnorm", n, [(f, row0), (x, 0)], [g_post, g_next],
                             [(total, row0, F32), (total, row0, BF16)], outs)
    return outs


def resnorm(f, x, g_post, g_next):
    total = x.shape[0]
    return _rowwise_call(_resnorm_kernel, "resnorm", total, [(f, 0), (x, 0)], [g_post, g_next],
                         [(total, 0, F32), (total, 0, BF16)], None)


def resnorm_last_to_parts(f, x, g_post, parts):
    return [_rowwise_call(_resnorm_last_kernel, "resnorm_last", n, [(f, row0), (x, row0)], [g_post],
                          [(n, 0, F32)], None)[0] for row0, n in parts]


CAST_JOB_ROW_CHOICES = (32, 64, 128, 256, 512, 1024)


class CastJob:
    def __init__(self, w, layer, out_rows, out_cols, n_steps):
        _, n_rows, n_cols = w.shape
        fits = [r for r in CAST_JOB_ROW_CHOICES
                if n_rows % r == 0 and out_rows % r == 0 and out_rows // r <= n_steps]
        self.feasible = bool(fits)
        if not fits:
            return
        self.w, self.layer, self.rows = w, layer, fits[0]
        self.n_cols, self.out_rows, self.out_cols = n_cols, out_rows, out_cols
        self.in_blocks, self.out_blocks = n_rows // self.rows, out_rows // self.rows

    def specs(self, step_of):
        in_spec = pl.BlockSpec((None, self.rows, self.n_cols),
                               lambda *g: (self.layer, jnp.minimum(step_of(*g), self.in_blocks - 1), 0))
        out_spec = pl.BlockSpec((self.rows, self.out_cols),
                                lambda *g: (jnp.minimum(step_of(*g), self.out_blocks - 1), 0))
        return in_spec, out_spec, jax.ShapeDtypeStruct((self.out_rows, self.out_cols), BF16)

    def run(self, step, w_ref, o_ref):
        @pl.when(step < self.in_blocks)
        def _():
            o_ref[:, :self.n_cols] = w_ref[...].astype(o_ref.dtype)
            if self.out_cols > self.n_cols:
                o_ref[:, self.n_cols:] = jnp.zeros((self.rows, self.out_cols - self.n_cols), o_ref.dtype)

        if self.out_blocks > self.in_blocks:
            @pl.when((step >= self.in_blocks) & (step < self.out_blocks))
            def _():
                o_ref[...] = jnp.zeros(o_ref.shape, o_ref.dtype)


def _cast_weight_kernel(w_ref, o_ref, *, valid_rows, valid_cols):
    br, bc = o_ref.shape
    r = pl.program_id(0) * br + lax.broadcasted_iota(jnp.int32, (br, bc), 0)
    c = pl.program_id(1) * bc + lax.broadcasted_iota(jnp.int32, (br, bc), 1)
    o_ref[...] = jnp.where((r < valid_rows) & (c < valid_cols), w_ref[...], 0.0).astype(o_ref.dtype)


def cast_weight(w, layer, out_rows=None, out_cols=None, br=512, bc=1024):
    _, rows, cols = w.shape
    out_rows = out_rows or rows
    out_cols = out_cols or cols
    return pl.pallas_call(
        functools.partial(_cast_weight_kernel, valid_rows=rows, valid_cols=cols),
        grid=(out_rows // br, out_cols // bc),
        in_specs=[pl.BlockSpec((None, br, bc), lambda i, j: (layer, i, j))],
        out_specs=pl.BlockSpec((br, bc), lambda i, j: (i, j)),
        out_shape=jax.ShapeDtypeStruct((out_rows, out_cols), BF16),
        compiler_params=_params(("parallel", "parallel"), 32),
        name="cast_weight",
    )(w)


def _with_cast_jobs(body, n_in, n_out, jobs, n_j):
    n_jobs = len(jobs)

    def kernel(*refs):
        ins, job_ins = refs[:n_in], refs[n_in:n_in + n_jobs]
        outs = refs[n_in + n_jobs:n_in + n_jobs + n_out]
        job_outs = refs[n_in + n_jobs + n_out:n_in + n_jobs + n_out + n_jobs]
        body(*ins, *outs, *refs[n_in + n_jobs + n_out + n_jobs:])
        step = pl.program_id(0) * n_j + pl.program_id(1)
        for job, w_ref, o_ref in zip(jobs, job_ins, job_outs):
            job.run(step, w_ref, o_ref)

    return kernel


def _matmul_call(body, name, ins, in_specs, out_spec, out_shape, grid, job_requests, vmem_mb):
    n_i, n_j = grid
    jobs = [CastJob(w, layer, r, c, n_i * n_j) for w, layer, r, c in job_requests]
    riding = [job for job in jobs if job.feasible]
    job_specs = [job.specs(lambda i, j: i * n_j + j) for job in riding]
    outs = pl.pallas_call(
        _with_cast_jobs(body, len(ins), 1, riding, n_j),
        grid=grid,
        in_specs=list(in_specs) + [s[0] for s in job_specs],
        out_specs=[out_spec] + [s[1] for s in job_specs],
        out_shape=[out_shape] + [s[2] for s in job_specs],
        compiler_params=_params(("arbitrary", "arbitrary"), vmem_mb),
        name=name,
    )(*ins, *[job.w for job in riding])
    cast = iter(outs[1:])
    weights = [next(cast) if job.feasible else cast_weight(w, layer, r, c)
               for job, (w, layer, r, c) in zip(jobs, job_requests)]
    return outs[0], weights


def _mm_kernel(a_ref, b_ref, o_ref):
    o_ref[...] = jnp.dot(a_ref[...], b_ref[...], preferred_element_type=F32).astype(o_ref.dtype)


def matmul(a, b, out_dtype, job_requests=(), bm=1024, bn=1024):
    m, k = a.shape
    _, n = b.shape
    bm = min(bm, m)
    return _matmul_call(
        _mm_kernel, "matmul_full_k", [a, b],
        [pl.BlockSpec((bm, k), lambda i, j: (i, 0)), pl.BlockSpec((k, bn), lambda i, j: (0, j))],
        pl.BlockSpec((bm, bn), lambda i, j: (i, j)), jax.ShapeDtypeStruct((m, n), out_dtype),
        (m // bm, n // bn), job_requests, 56)


def _mm2_kernel(a1_ref, a2_ref, b1_ref, b2_ref, o_ref):
    o_ref[...] = (jnp.dot(a1_ref[...], b1_ref[...], preferred_element_type=F32)
                  + jnp.dot(a2_ref[...], b2_ref[...], preferred_element_type=F32))


def matmul_two_lhs(a1, a2, b, bm=1024, bn=1024):
    m, k1 = a1.shape
    _, k2 = a2.shape
    assert k1 == k2 and b.shape[0] == k1 + k2
    n = b.shape[1]
    bm = min(bm, m)
    return pl.pallas_call(
        _mm2_kernel,
        grid=(m // bm, n // bn),
        in_specs=[pl.BlockSpec((bm, k1), lambda i, j: (i, 0)),
                  pl.BlockSpec((bm, k2), lambda i, j: (i, 0)),
                  pl.BlockSpec((k1, bn), lambda i, j: (0, j)),
                  pl.BlockSpec((k2, bn), lambda i, j: (1, j))],
        out_specs=pl.BlockSpec((bm, bn), lambda i, j: (i, j)),
        out_shape=jax.ShapeDtypeStruct((m, n), F32),
        compiler_params=_params(("parallel", "parallel"), 48),
        name="matmul_two_lhs",
    )(a1, a2, b, b)


def _mm_acc_kernel(a_ref, b_ref, o_ref):
    part = jnp.dot(a_ref[...], b_ref[...], preferred_element_type=F32)

    @pl.when(pl.program_id(2) == 0)
    def _():
        o_ref[...] = part

    @pl.when(pl.program_id(2) != 0)
    def _():
        o_ref[...] += part


def matmul_split_k(a, b, bk, bm=1024, bn=1024):
    m, k = a.shape
    _, n = b.shape
    bm = min(bm, m)
    return pl.pallas_call(
        _mm_acc_kernel,
        grid=(m // bm, n // bn, k // bk),
        in_specs=[pl.BlockSpec((bm, bk), lambda i, j, kk: (i, kk)),
                  pl.BlockSpec((bk, bn), lambda i, j, kk: (kk, j))],
        out_specs=pl.BlockSpec((bm, bn), lambda i, j, kk: (i, j)),
        out_shape=jax.ShapeDtypeStruct((m, n), F32),
        compiler_params=_params(("parallel", "parallel", "arbitrary"), 48),
        name="matmul_split_k",
    )(a, b)


def _gateup_kernel(h_ref, wg_ref, wu_ref, o_ref):
    h = h_ref[...]
    g = jnp.dot(h, wg_ref[...], preferred_element_type=F32)
    u = jnp.dot(h, wu_ref[...], preferred_element_type=F32)
    o_ref[...] = (g * jax.nn.sigmoid(g) * u).astype(o_ref.dtype)


def gate_up(h, wg, wu, job_requests=(), bm=1024, bn=512):
    m, k = h.shape
    _, n = wg.shape
    bm = min(bm, m)
    w_spec = pl.BlockSpec((k, bn), lambda i, j: (0, j))
    return _matmul_call(
        _gateup_kernel, "gate_up", [h, wg, wu],
        [pl.BlockSpec((bm, k), lambda i, j: (i, 0)), w_spec, w_spec],
        pl.BlockSpec((bm, bn), lambda i, j: (i, j)), jax.ShapeDtypeStruct((m, n), BF16),
        (m // bm, n // bn), job_requests, 56)


ATTN_KEY_BLOCK = 1024
ATTN_KEYS_PER_LOOP_TRIP = 2048
V_ROWS = HEAD_DIM + 16
LOG2E = 1.4426950408889634


def _rope_tables(segments):
    pos = jnp.concatenate([jnp.arange(n, dtype=jnp.int32) for _, n in segments])
    r = (pos // GRID_W).astype(F32)
    c = (pos % GRID_W).astype(F32)
    inv = 1.0 / (ROPE_THETA ** (jnp.arange(0, ROPE_HALF, 2, dtype=F32) / ROPE_HALF))
    ang_r = r[:, None] * inv
    ang_c = c[:, None] * inv
    cos = jnp.concatenate([jnp.cos(ang_r), jnp.cos(ang_r), jnp.cos(ang_c), jnp.cos(ang_c)], axis=-1)
    sin = jnp.concatenate([-jnp.sin(ang_r), jnp.sin(ang_r), -jnp.sin(ang_c), jnp.sin(ang_c)], axis=-1)
    return cos, sin


def _norm_rope(x, gain, cos, sin, upper):
    xn = _rms(x) * gain
    quarter = ROPE_HALF // 2
    partner = jnp.where(upper, pltpu.roll(xn, quarter, axis=1), pltpu.roll(xn, HEAD_DIM - quarter, axis=1))
    return xn * cos + partner * sin


def _qkv_prep_kernel(q_ref, k_ref, v_ref, cos_ref, sin_ref, qg_ref, kg_ref, qt_ref, ko_ref, vt_ref):
    cos = cos_ref[...]
    sin = sin_ref[...]
    rows = cos.shape[0]
    lane = lax.broadcasted_iota(jnp.int32, cos.shape, 1)
    upper = (lane & (ROPE_HALF // 2)) != 0
    q_scale = HEAD_DIM ** -0.5 * LOG2E
    qg = qg_ref[...]
    kg = kg_ref[...]
    for h in range(N_Q_HEADS):
        sl = slice(h * HEAD_DIM, (h + 1) * HEAD_DIM)
        y = _norm_rope(q_ref[:, sl].astype(F32), qg, cos, sin, upper) * q_scale
        qt_ref[sl, :] = y.T.astype(qt_ref.dtype)
    for h in range(N_KV_HEADS):
        sl = slice(h * HEAD_DIM, (h + 1) * HEAD_DIM)
        y = _norm_rope(k_ref[:, sl].astype(F32), kg, cos, sin, upper)
        ko_ref[:, sl] = y.astype(ko_ref.dtype)
        vt_ref[h, 0, :HEAD_DIM, :] = v_ref[:, sl].astype(F32).T.astype(vt_ref.dtype)
        vt_ref[h, 0, HEAD_DIM:, :] = jnp.ones((V_ROWS - HEAD_DIM, rows), vt_ref.dtype)


def qkv_prep(proj, cos, sin, q_gain, k_gain, key_block, rows=512):
    t = proj.shape[0]
    rows = min(rows, key_block)
    per_key_block = key_block // rows
    return pl.pallas_call(
        _qkv_prep_kernel,
        grid=(t // rows,),
        in_specs=[pl.BlockSpec((rows, ATTN_WIDTH), lambda i: (i, Q_OFF // ATTN_WIDTH)),
                  pl.BlockSpec((rows, KV_WIDTH), lambda i: (i, K_OFF // KV_WIDTH)),
                  pl.BlockSpec((rows, KV_WIDTH), lambda i: (i, V_OFF // KV_WIDTH)),
                  pl.BlockSpec((rows, HEAD_DIM), lambda i: (i, 0)),
                  pl.BlockSpec((rows, HEAD_DIM), lambda i: (i, 0)),
                  pl.BlockSpec((1, HEAD_DIM), lambda i: (0, 0)),
                  pl.BlockSpec((1, HEAD_DIM), lambda i: (0, 0))],
        out_specs=[pl.BlockSpec((ATTN_WIDTH, rows), lambda i: (0, i)),
                   pl.BlockSpec((rows, KV_WIDTH), lambda i: (i, 0)),
                   pl.BlockSpec((N_KV_HEADS, 1, V_ROWS, rows),
                                lambda i: (0, i // per_key_block, 0, i % per_key_block))],
        out_shape=[jax.ShapeDtypeStruct((ATTN_WIDTH, t), BF16),
                   jax.ShapeDtypeStruct((t, KV_WIDTH), BF16),
                   jax.ShapeDtypeStruct((N_KV_HEADS, t // key_block, V_ROWS, key_block), BF16)],
        compiler_params=_params(("parallel",), 40),
        name="qkv_prep",
    )(proj, proj, proj, cos, sin, q_gain.reshape(1, HEAD_DIM), k_gain.reshape(1, HEAD_DIM))


def _attn_kernel(qt_ref, k_ref, vt_ref, *rest, bk, nk):
    o_ref, m_scr, acc_scr, s_scr, smax_scr = rest[-5:]
    m_scr[...] = jnp.full(m_scr.shape, -jnp.inf, F32)
    acc_scr[...] = jnp.zeros(acc_scr.shape, F32)

    def scores(kb, h, slot):
        s = jnp.dot(kb, qt_ref[h * HEAD_DIM:(h + 1) * HEAD_DIM, :], preferred_element_type=F32)
        s_scr[slot] = s
        smax_scr[slot] = jnp.max(s, axis=0, keepdims=True)

    def key_block(j):
        return k_ref[pl.ds(pl.multiple_of(j * bk, bk), bk), :]

    scores(key_block(0), 0, 0)

    def body(j, carry):
        kb = key_block(j)
        kb_next = key_block(jnp.minimum(j + 1, nk - 1))
        vb = vt_ref[j]
        for h in range(Q_PER_KV):
            slot = h % 2
            if h + 1 < Q_PER_KV:
                scores(kb, h + 1, 1 - slot)
            else:
                scores(kb_next, 0, 1 - slot)
            m_prev = m_scr[h]
            m_new = jnp.maximum(m_prev, smax_scr[slot])
            alpha = jnp.exp2(m_prev - m_new)
            p = jnp.exp2(s_scr[slot] - m_new).astype(vb.dtype)
            acc_scr[h] = acc_scr[h] * alpha + jnp.dot(vb, p, preferred_element_type=F32)
            m_scr[h] = m_new
        return carry

    lax.fori_loop(0, nk, body, 0, unroll=max(1, ATTN_KEYS_PER_LOOP_TRIP // bk))
    for h in range(Q_PER_KV):
        acc = acc_scr[h]
        o = acc[:HEAD_DIM, :] / acc[HEAD_DIM:HEAD_DIM + 1, :]
        o_ref[:, h * HEAD_DIM:(h + 1) * HEAD_DIM] = o.T.astype(o_ref.dtype)


def attention(qt, k_rot, vt, prev_out, row0, batch, seq, bq=512):
    t = k_rot.shape[0]
    bk = vt.shape[-1]
    bq = min(bq, seq)
    assert row0 % seq == 0 and seq % bq == 0 and seq % bk == 0
    group_w = Q_PER_KV * HEAD_DIM
    nq = seq // bq
    nk = seq // bk
    in_specs = [
        pl.BlockSpec((group_w, bq), lambda b, h, i: (h, row0 // bq + b * nq + i)),
        pl.BlockSpec((seq, HEAD_DIM), lambda b, h, i: (row0 // seq + b, h)),
        pl.BlockSpec((None, nk, V_ROWS, bk), lambda b, h, i: (h, row0 // seq + b, 0, 0)),
    ]
    args = [qt, k_rot, vt]
    aliases = {}
    if prev_out is not None:
        in_specs.append(pl.BlockSpec(memory_space=pl.ANY))
        args.append(prev_out)
        aliases = {3: 0}
    return pl.pallas_call(
        functools.partial(_attn_kernel, bk=bk, nk=nk),
        grid=(batch, N_KV_HEADS, nq),
        in_specs=in_specs,
        out_specs=pl.BlockSpec((bq, group_w), lambda b, h, i: (row0 // bq + b * nq + i, h)),
        out_shape=jax.ShapeDtypeStruct((t, ATTN_WIDTH), BF16),
        scratch_shapes=[pltpu.VMEM((Q_PER_KV, 1, bq), F32),
                        pltpu.VMEM((Q_PER_KV, V_ROWS, bq), F32),
                        pltpu.VMEM((2, bk, bq), F32),
                        pltpu.VMEM((2, 1, bq), F32)],
        input_output_aliases=aliases,
        compiler_params=_params(("parallel", "parallel", "parallel"), 48),
        name="attention",
    )(*args)


def _convpool_kernel(ch_ref, cb_ref, cc_ref, pu_ref,
                     chp_ref, ccp_ref, pup_ref, chn_ref, ccn_ref, pun_ref,
                     cw_ref, pw_ref, ps_ref, o_ref, *, rows, segments):
    i = pl.program_id(0)
    start = i * rows
    pos0 = start
    seq_len = jnp.int32(0)
    has_prev = jnp.bool_(True)
    has_next = jnp.bool_(True)
    for seg_start, seg_len in segments:
        inside = (start >= seg_start) & (start < seg_start + seg_len)
        pos0 = jnp.where(inside, start - seg_start, pos0)
        seq_len = jnp.where(inside, seg_len, seq_len)
        has_prev = has_prev & (start != seg_start)
        has_next = has_next & (start + rows != seg_start + seg_len)
    prev_on = has_prev.astype(F32)
    next_on = has_next.astype(F32)

    def extended(cur_ref, prev_ref, next_ref):
        return jnp.concatenate([prev_ref[...].astype(F32) * prev_on,
                                cur_ref[...].astype(F32),
                                next_ref[...].astype(F32) * next_on], axis=0)

    ext_rows = rows + 2 * HALO

    def shifted(x, d):
        return pltpu.roll(x, d % ext_rows, axis=0)

    u = extended(cc_ref, ccp_ref, ccn_ref) * extended(ch_ref, chp_ref, chn_ref)
    cw = cw_ref[...]
    y = shifted(u, 1) * cw[0:1, :] + u * cw[1:2, :] + shifted(u, -1) * cw[2:3, :]
    o_ref[:, :CONV_WIDTH] = (cb_ref[...].astype(F32) * y[HALO:HALO + rows, :]).astype(o_ref.dtype)

    x = extended(pu_ref, pup_ref, pun_ref)
    pos = (pos0 + lax.broadcasted_iota(jnp.int32, (rows, 1), 0))
    for gi, w in enumerate(POOL_WINDOWS):
        sl = slice(gi * POOL_GROUP, (gi + 1) * POOL_GROUP)
        xg = x[:, sl]
        tot = xg + shifted(xg, 1)
        half = 1
        while 2 * half < w:
            tot = shifted(tot, half) + shifted(tot, -half)
            half *= 2
        hi = jnp.minimum(pos + w // 2, seq_len)
        lo = jnp.maximum(pos - w // 2, 0)
        cnt = (hi - lo).astype(F32)
        m = tot[HALO:HALO + rows, :] / cnt - xg[HALO:HALO + rows, :]
        yg = jnp.dot(m.astype(BF16), pw_ref[gi], preferred_element_type=F32)
        o_ref[:, CONV_WIDTH + gi * POOL_GROUP:CONV_WIDTH + (gi + 1) * POOL_GROUP] = (
            yg * ps_ref[:, sl]).astype(o_ref.dtype)


def conv_pool(proj, conv_w, pool_w, pool_scale, segments, rows=512):
    t = proj.shape[0]
    rows = min(rows, min(n for _, n in segments))
    assert all(s % rows == 0 and n % rows == 0 for s, n in segments)
    w = CONV_WIDTH
    per = rows // HALO
    last = t // HALO - 1

    def cur(off):
        return pl.BlockSpec((rows, w), lambda i: (i, off // w))

    def prev(off):
        return pl.BlockSpec((HALO, w), lambda i: (jnp.maximum(i * per - 1, 0), off // w))

    def nxt(off):
        return pl.BlockSpec((HALO, w), lambda i: (jnp.minimum((i + 1) * per, last), off // w))

    kernel = functools.partial(_convpool_kernel, rows=rows, segments=segments)
    return pl.pallas_call(
        kernel,
        grid=(t // rows,),
        in_specs=[cur(CH_OFF), cur(CB_OFF), cur(CC_OFF), cur(PU_OFF),
                  prev(CH_OFF), prev(CC_OFF), prev(PU_OFF),
                  nxt(CH_OFF), nxt(CC_OFF), nxt(PU_OFF),
                  pl.BlockSpec((3, w), lambda i: (0, 0)),
                  pl.BlockSpec((len(POOL_WINDOWS), POOL_GROUP, POOL_GROUP), lambda i: (0, 0, 0)),
                  pl.BlockSpec((1, POOL_WIDTH), lambda i: (0, 0))],
        out_specs=pl.BlockSpec((rows, CONV_WIDTH + POOL_WIDTH), lambda i: (i, 0)),
        out_shape=jax.ShapeDtypeStruct((t, CONV_WIDTH + POOL_WIDTH), BF16),
        compiler_params=_params(("parallel",), 48),
        name="conv_pool",
    )(proj, proj, proj, proj, proj, proj, proj, proj, proj, proj,
      conv_w, pool_w, pool_scale.reshape(1, POOL_WIDTH))


def _trunk(x_parts, segments, attn_calls, ln_mix_pre, ln_mix_post, ln_ffn_pre, ln_ffn_post, q_norm, k_norm,
           w_in, conv_w, pool_w, pool_scale, w_out, w_gate, w_up, w_down):
    total = sum(n for _, _, n in x_parts)
    cos, sin = _rope_tables(segments)
    key_block = min(ATTN_KEY_BLOCK, min(n for _, n in segments))
    h = norm_cast_parts(x_parts, ln_mix_pre[0], total)
    x = None
    w_in_bf16 = cast_weight(w_in, 0)
    d_model = w_out.shape[2]
    for l in range(DEPTH):
        proj, (w_out_bf16, w_gate_bf16, w_up_bf16) = matmul(
            h, w_in_bf16, BF16,
            [(w_out, l, w_out.shape[1], d_model), (w_gate, l, d_model, D_FF_PAD), (w_up, l, d_model, D_FF_PAD)])
        qt, k_rot, vt = qkv_prep(proj, cos, sin, q_norm[l], k_norm[l], key_block)
        a_out = None
        for row0, batch, seq in attn_calls:
            a_out = attention(qt, k_rot, vt, a_out, row0, batch, seq)
        cp_out = conv_pool(proj, conv_w[l], pool_w[l].astype(BF16), pool_scale[l], segments)
        mix = matmul_two_lhs(a_out, cp_out, w_out_bf16)
        if x is None:
            x, h = resnorm_from_parts(mix, x_parts, ln_mix_post[l], ln_ffn_pre[l], total)
        else:
            x, h = resnorm(mix, x, ln_mix_post[l], ln_ffn_pre[l])
        jobs = [(w_down, l, D_FF_PAD, d_model)]
        if l + 1 < DEPTH:
            jobs.append((w_in, l + 1, d_model, w_in.shape[2]))
        act, cast = gate_up(h, w_gate_bf16, w_up_bf16, jobs)
        w_down_bf16 = cast[0]
        w_in_bf16 = cast[1] if l + 1 < DEPTH else None
        f = matmul_split_k(act, w_down_bf16, bk=D_FF_PAD // 4)
        if l + 1 < DEPTH:
            x, h = resnorm(f, x, ln_ffn_post[l], ln_mix_pre[l + 1])
    return resnorm_last_to_parts(f, x, ln_ffn_post[DEPTH - 1], [(row0, n) for _, row0, n in x_parts])


def kernel(x_prompt, x_sample, ln_mix_pre, ln_mix_post, ln_ffn_pre, ln_ffn_post, q_norm, k_norm,
           w_in, conv_w, pool_w, pool_scale, w_out, w_gate, w_up, w_down):
    pb, ps, d = x_prompt.shape
    sb, ss, _ = x_sample.shape
    x_parts = [(x_prompt.reshape(pb * ps, d), 0, pb * ps), (x_sample.reshape(sb * ss, d), pb * ps, sb * ss)]
    segments = tuple((b * ps, ps) for b in range(pb)) + tuple((pb * ps + b * ss, ss) for b in range(sb))
    attn_calls = ((0, pb, ps), (pb * ps, sb, ss))
    y_prompt, y_sample = _trunk(x_parts, segments, attn_calls, ln_mix_pre, ln_mix_post, ln_ffn_pre, ln_ffn_post,
                                q_norm, k_norm, w_in, conv_w, pool_w, pool_scale, w_out, w_gate, w_up, w_down)
    return (y_prompt.reshape(pb, ps, d), y_sample.reshape(sb, ss, d))
```

```python
import functools

import jax
import jax.numpy as jnp
from jax import lax
from jax.experimental import pallas as pl
from jax.experimental.pallas import tpu as pltpu

D_MODEL = 4096
DEPTH = 2
GRID_W = 64
HEAD_DIM = 128
N_Q_HEADS = 16
N_KV_HEADS = 4
Q_PER_KV = N_Q_HEADS // N_KV_HEADS
ATTN_WIDTH = N_Q_HEADS * HEAD_DIM
KV_WIDTH = N_KV_HEADS * HEAD_DIM
CONV_WIDTH = D_MODEL // 4
POOL_WIDTH = D_MODEL // 4
POOL_WINDOWS = (2, 4, 8, 16)
POOL_GROUP = POOL_WIDTH // len(POOL_WINDOWS)
IN_WIDTH = ATTN_WIDTH + 2 * KV_WIDTH + 3 * CONV_WIDTH + POOL_WIDTH
D_FF = 11008
D_FF_PAD = 11264
ROPE_HALF = HEAD_DIM // 2
ROPE_THETA = 10000.0
EPS = 1e-6

Q_OFF = 0
K_OFF = ATTN_WIDTH
V_OFF = K_OFF + KV_WIDTH
CH_OFF = V_OFF + KV_WIDTH
CB_OFF = CH_OFF + CONV_WIDTH
CC_OFF = CB_OFF + CONV_WIDTH
PU_OFF = CC_OFF + CONV_WIDTH

HALO = 16
V7X_VMEM_BYTES = 64 * 1024 * 1024

F32 = jnp.float32
BF16 = jnp.bfloat16


def _params(semantics, vmem_mb):
    assert vmem_mb * 1024 * 1024 < V7X_VMEM_BYTES
    return pltpu.CompilerParams(dimension_semantics=semantics,
                                vmem_limit_bytes=vmem_mb * 1024 * 1024)


def _rms(x):
    return x * lax.rsqrt(jnp.mean(x * x, axis=-1, keepdims=True) + EPS)


def _rows_spec(rows, d, row0):
    assert row0 % rows == 0
    return pl.BlockSpec((rows, d), lambda i: (row0 // rows + i, 0))


def _rowwise_call(body, name, n_rows, ins, gains, outs, prev_outs, rows=256):
    d = ins[0][0].shape[1]
    in_specs = [_rows_spec(rows, d, r0) for _, r0 in ins]
    in_specs += [pl.BlockSpec((1, d), lambda i: (0, 0)) for _ in gains]
    args = [a for a, _ in ins] + [g.reshape(1, d) for g in gains]
    aliases = {}
    kernel = body
    if prev_outs is not None:
        n_in = len(args)
        in_specs += [pl.BlockSpec(memory_space=pl.ANY) for _ in prev_outs]
        args += list(prev_outs)
        aliases = {n_in + k: k for k in range(len(prev_outs))}

        def kernel(*refs):
            body(*refs[:n_in], *refs[n_in + len(prev_outs):])

    return pl.pallas_call(
        kernel,
        grid=(n_rows // rows,),
        in_specs=in_specs,
        out_specs=[_rows_spec(rows, d, r0) for _, r0, _ in outs],
        out_shape=[jax.ShapeDtypeStruct((total, d), dt) for total, _, dt in outs],
        input_output_aliases=aliases,
        compiler_params=_params(("parallel",), 40),
        name=name,
    )(*args)


def _norm_cast_kernel(x_ref, g_ref, h_ref):
    h_ref[...] = (_rms(x_ref[...]) * g_ref[...]).astype(h_ref.dtype)


def _resnorm_kernel(f_ref, x_ref, g_post_ref, g_next_ref, xo_ref, h_ref):
    xn = x_ref[...] + _rms(f_ref[...]) * g_post_ref[...]
    xo_ref[...] = xn
    h_ref[...] = (_rms(xn) * g_next_ref[...]).astype(h_ref.dtype)


def _resnorm_last_kernel(f_ref, x_ref, g_post_ref, xo_ref):
    xo_ref[...] = x_ref[...] + _rms(f_ref[...]) * g_post_ref[...]


def norm_cast_parts(parts, g, total):
    h = None
    for x, row0, n in parts:
        h, = _rowwise_call(_norm_cast_kernel, "norm_cast", n, [(x, 0)], [g], [(total, row0, BF16)],
                           None if h is None else [h])
    return h


def resnorm_from_parts(f, parts, g_post, g_next, total):
    outs = None
    for x, row0, n in parts:
        outs = _rowwise_call(_resnorm_kernel, "resnorm", n, [(f, row0), (x, 0)], [g_post, g_next],
                             [(total, row0, F32), (total, row0, BF16)], outs)
    return outs


def resnorm(f, x, g_post, g_next):
    total = x.shape[0]
    return _rowwise_call(_resnorm_kernel, "resnorm", total, [(f, 0), (x, 0)], [g_post, g_next],
                         [(total, 0, F32), (total, 0, BF16)], None)


def resnorm_last_to_parts(f, x, g_post, parts):
    return [_rowwise_call(_resnorm_last_kernel, "resnorm_last", n, [(f, row0), (x, row0)], [g_post],
                          [(n, 0, F32)], None)[0] for row0, n in parts]


CAST_JOB_ROW_CHOICES = (32, 64, 128, 256, 512, 1024)


class CastJob:
    def __init__(self, w, layer, out_rows, out_cols, n_steps):
        _, n_rows, n_cols = w.shape
        fits = [r for r in CAST_JOB_ROW_CHOICES
                if n_rows % r == 0 and out_rows % r == 0 and out_rows // r <= n_steps]
        self.feasible = bool(fits)
        if not fits:
            return
        self.w, self.layer, self.rows = w, layer, fits[0]
        self.n_cols, self.out_rows, self.out_cols = n_cols, out_rows, out_cols
        self.in_blocks, self.out_blocks = n_rows // self.rows, out_rows // self.rows

    def specs(self, step_of):
        in_spec = pl.BlockSpec((None, self.rows, self.n_cols),
                               lambda *g: (self.layer, jnp.minimum(step_of(*g), self.in_blocks - 1), 0))
        out_spec = pl.BlockSpec((self.rows, self.out_cols),
                                lambda *g: (jnp.minimum(step_of(*g), self.out_blocks - 1), 0))
        return in_spec, out_spec, jax.ShapeDtypeStruct((self.out_rows, self.out_cols), BF16)

    def run(self, step, w_ref, o_ref):
        @pl.when(step < self.in_blocks)
        def _():
            o_ref[:, :self.n_cols] = w_ref[...].astype(o_ref.dtype)
            if self.out_cols > self.n_cols:
                o_ref[:, self.n_cols:] = jnp.zeros((self.rows, self.out_cols - self.n_cols), o_ref.dtype)

        if self.out_blocks > self.in_blocks:
            @pl.when((step >= self.in_blocks) & (step < self.out_blocks))
            def _():
                o_ref[...] = jnp.zeros(o_ref.shape, o_ref.dtype)


def _cast_weight_kernel(w_ref, o_ref, *, valid_rows, valid_cols):
    br, bc = o_ref.shape
    r = pl.program_id(0) * br + lax.broadcasted_iota(jnp.int32, (br, bc), 0)
    c = pl.program_id(1) * bc + lax.broadcasted_iota(jnp.int32, (br, bc), 1)
    o_ref[...] = jnp.where((r < valid_rows) & (c < valid_cols), w_ref[...], 0.0).astype(o_ref.dtype)


def cast_weight(w, layer, out_rows=None, out_cols=None, br=512, bc=1024):
    _, rows, cols = w.shape
    out_rows = out_rows or rows
    out_cols = out_cols or cols
    return pl.pallas_call(
        functools.partial(_cast_weight_kernel, valid_rows=rows, valid_cols=cols),
        grid=(out_rows // br, out_cols // bc),
        in_specs=[pl.BlockSpec((None, br, bc), lambda i, j: (layer, i, j))],
        out_specs=pl.BlockSpec((br, bc), lambda i, j: (i, j)),
        out_shape=jax.ShapeDtypeStruct((out_rows, out_cols), BF16),
        compiler_params=_params(("parallel", "parallel"), 32),
        name="cast_weight",
    )(w)


def _with_cast_jobs(body, n_in, n_out, jobs, n_j):
    n_jobs = len(jobs)

    def kernel(*refs):
        ins, job_ins = refs[:n_in], refs[n_in:n_in + n_jobs]
        outs = refs[n_in + n_jobs:n_in + n_jobs + n_out]
        job_outs = refs[n_in + n_jobs + n_out:n_in + n_jobs + n_out + n_jobs]
        body(*ins, *outs, *refs[n_in + n_jobs + n_out + n_jobs:])
        step = pl.program_id(0) * n_j + pl.program_id(1)
        for job, w_ref, o_ref in zip(jobs, job_ins, job_outs):
            job.run(step, w_ref, o_ref)

    return kernel


def _matmul_call(body, name, ins, in_specs, out_spec, out_shape, grid, job_requests, vmem_mb):
    n_i, n_j = grid
    jobs = [CastJob(w, layer, r, c, n_i * n_j) for w, layer, r, c in job_requests]
    riding = [job for job in jobs if job.feasible]
    job_specs = [job.specs(lambda i, j: i * n_j + j) for job in riding]
    outs = pl.pallas_call(
        _with_cast_jobs(body, len(ins), 1, riding, n_j),
        grid=grid,
        in_specs=list(in_specs) + [s[0] for s in job_specs],
        out_specs=[out_spec] + [s[1] for s in job_specs],
        out_shape=[out_shape] + [s[2] for s in job_specs],
        compiler_params=_params(("arbitrary", "arbitrary"), vmem_mb),
        name=name,
    )(*ins, *[job.w for job in riding])
    cast = iter(outs[1:])
    weights = [next(cast) if job.feasible else cast_weight(w, layer, r, c)
               for job, (w, layer, r, c) in zip(jobs, job_requests)]
    return outs[0], weights


RESNORM_CHUNK_CHOICES = (64, 128, 256, 512, 1024)


def resnorm_matmul(dot_body, name, f, x_parts, g_post, g_next, weights, bn, out_dtype, job_requests=(),
                   bm=1024, vmem_mb=60):
    t, d = f.shape
    n = weights[0].shape[1]
    bm = min(bm, t)
    n_i, n_j = t // bm, n // bn
    chunk = next(c for c in RESNORM_CHUNK_CHOICES if bm % c == 0 and bm // c <= n_j)
    n_chunks = bm // chunk
    n_parts, n_w = len(x_parts), len(weights)
    part_first = [row0 // chunk for _, row0, _ in x_parts]
    part_chunks = [rows // chunk for _, _, rows in x_parts]
    assert all(row0 % chunk == 0 and rows % chunk == 0 for _, row0, rows in x_parts)

    def chunk_index(r, j):
        return jnp.where(r < n_i, r * n_chunks + jnp.minimum(j, n_chunks - 1), n_i * n_chunks - 1)

    def part_spec(first, count):
        return pl.BlockSpec((chunk, d), lambda r, j: (jnp.clip(chunk_index(r, j) - first, 0, count - 1), 0))

    jobs = [CastJob(w, layer, rr, cc, (n_i + 1) * n_j) for w, layer, rr, cc in job_requests]
    riding = [job for job in jobs if job.feasible]
    job_specs = [job.specs(lambda r, j: r * n_j + j) for job in riding]
    n_jobs = len(riding)

    def kernel(*refs):
        f_ref, x_refs = refs[0], refs[1:1 + n_parts]
        gp_ref, gn_ref = refs[1 + n_parts:3 + n_parts]
        w_refs = refs[3 + n_parts:3 + n_parts + n_w]
        base = 3 + n_parts + n_w
        job_ins = refs[base:base + n_jobs]
        o_ref, x1_ref = refs[base + n_jobs:base + n_jobs + 2]
        job_outs = refs[base + n_jobs + 2:base + 2 * n_jobs + 2]
        h_scr, stage_scr = refs[-2:]
        r = pl.program_id(0)
        j = pl.program_id(1)

        def norms():
            at = chunk_index(r, j)
            x = x_refs[0][...]
            for first, x_ref in zip(part_first[1:], x_refs[1:]):
                x = jnp.where(at >= first, x_ref[...], x)
            xn = x + _rms(f_ref[...]) * gp_ref[...]
            x1_ref[...] = xn
            stage_scr[...] = (_rms(xn) * gn_ref[...]).astype(stage_scr.dtype)

        def publish():
            rows = pl.ds(pl.multiple_of(jnp.minimum(j, n_chunks - 1) * chunk, chunk), chunk)
            h_scr[r % 2, rows, :] = stage_scr[...]

        def consume(between):
            dot_body(lambda: h_scr[(r + 1) % 2], w_refs, o_ref, between)

        @pl.when(r == 0)
        def _():
            norms()
            publish()

        @pl.when((r > 0) & (r < n_i))
        def _():
            def between():
                norms()
                publish()
            consume(between)

        @pl.when(r == n_i)
        def _():
            consume(lambda: None)

        step = r * n_j + j
        for job, w_ref, jo_ref in zip(riding, job_ins, job_outs):
            job.run(step, w_ref, jo_ref)

    gain_spec = pl.BlockSpec((1, d), lambda r, j: (0, 0))
    w_specs = [pl.BlockSpec((w.shape[0], bn), lambda r, j: (0, j)) for w in weights]
    chunk_spec = pl.BlockSpec((chunk, d), lambda r, j: (chunk_index(r, j), 0))
    outs = pl.pallas_call(
        kernel,
        grid=(n_i + 1, n_j),
        in_specs=([chunk_spec] + [part_spec(fi, ct) for fi, ct in zip(part_first, part_chunks)]
                  + [gain_spec, gain_spec] + w_specs + [s[0] for s in job_specs]),
        out_specs=[pl.BlockSpec((bm, bn), lambda r, j: (jnp.maximum(r - 1, 0), jnp.where(r > 0, j, 0))),
                   chunk_spec] + [s[1] for s in job_specs],
        out_shape=[jax.ShapeDtypeStruct((t, n), out_dtype), jax.ShapeDtypeStruct((t, d), F32)]
                  + [s[2] for s in job_specs],
        scratch_shapes=[pltpu.VMEM((2, bm, d), BF16), pltpu.VMEM((chunk, d), BF16)],
        compiler_params=_params(("arbitrary", "arbitrary"), vmem_mb),
        name=name,
    )(f, *[x for x, _, _ in x_parts], g_post.reshape(1, d), g_next.reshape(1, d), *weights,
      *[job.w for job in riding])
    cast = iter(outs[2:])
    cast_weights = [next(cast) if job.feasible else cast_weight(w, layer, rr, cc)
                    for job, (w, layer, rr, cc) in zip(jobs, job_requests)]
    return outs[0], outs[1], cast_weights


def _dot_single(load_h, w_refs, o_ref, between):
    w_ref, = w_refs
    o_ref[...] = jnp.dot(load_h(), w_ref[...], preferred_element_type=F32).astype(o_ref.dtype)
    between()


def _dot_gate_up(load_h, w_refs, o_ref, between):
    wg_ref, wu_ref = w_refs
    g = jnp.dot(load_h(), wg_ref[...], preferred_element_type=F32)
    between()
    u = jnp.dot(load_h(), wu_ref[...], preferred_element_type=F32)
    o_ref[...] = (g * jax.nn.sigmoid(g) * u).astype(o_ref.dtype)


def _mm_kernel(a_ref, b_ref, o_ref):
    o_ref[...] = jnp.dot(a_ref[...], b_ref[...], preferred_element_type=F32).astype(o_ref.dtype)


def matmul(a, b, out_dtype, job_requests=(), bm=1024, bn=1024):
    m, k = a.shape
    _, n = b.shape
    bm = min(bm, m)
    return _matmul_call(
        _mm_kernel, "matmul_full_k", [a, b],
        [pl.BlockSpec((bm, k), lambda i, j: (i, 0)), pl.BlockSpec((k, bn), lambda i, j: (0, j))],
        pl.BlockSpec((bm, bn), lambda i, j: (i, j)), jax.ShapeDtypeStruct((m, n), out_dtype),
        (m // bm, n // bn), job_requests, 56)


def _mm2_kernel(a1_ref, a2_ref, b1_ref, b2_ref, o_ref):
    o_ref[...] = (jnp.dot(a1_ref[...], b1_ref[...], preferred_element_type=F32)
                  + jnp.dot(a2_ref[...], b2_ref[...], preferred_element_type=F32))


def matmul_two_lhs(a1, a2, b, bm=1024, bn=1024):
    m, k1 = a1.shape
    _, k2 = a2.shape
    assert k1 == k2 and b.shape[0] == k1 + k2
    n = b.shape[1]
    bm = min(bm, m)
    return pl.pallas_call(
        _mm2_kernel,
        grid=(m // bm, n // bn),
        in_specs=[pl.BlockSpec((bm, k1), lambda i, j: (i, 0)),
                  pl.BlockSpec((bm, k2), lambda i, j: (i, 0)),
                  pl.BlockSpec((k1, bn), lambda i, j: (0, j)),
                  pl.BlockSpec((k2, bn), lambda i, j: (1, j))],
        out_specs=pl.BlockSpec((bm, bn), lambda i, j: (i, j)),
        out_shape=jax.ShapeDtypeStruct((m, n), F32),
        compiler_params=_params(("parallel", "parallel"), 48),
        name="matmul_two_lhs",
    )(a1, a2, b, b)


def _mm_acc_kernel(a_ref, b_ref, o_ref):
    part = jnp.dot(a_ref[...], b_ref[...], preferred_element_type=F32)

    @pl.when(pl.program_id(2) == 0)
    def _():
        o_ref[...] = part

    @pl.when(pl.program_id(2) != 0)
    def _():
        o_ref[...] += part


def matmul_split_k(a, b, bk, bm=1024, bn=1024):
    m, k = a.shape
    _, n = b.shape
    bm = min(bm, m)
    return pl.pallas_call(
        _mm_acc_kernel,
        grid=(m // bm, n // bn, k // bk),
        in_specs=[pl.BlockSpec((bm, bk), lambda i, j, kk: (i, kk)),
                  pl.BlockSpec((bk, bn), lambda i, j, kk: (kk, j))],
        out_specs=pl.BlockSpec((bm, bn), lambda i, j, kk: (i, j)),
        out_shape=jax.ShapeDtypeStruct((m, n), F32),
        compiler_params=_params(("parallel", "parallel", "arbitrary"), 48),
        name="matmul_split_k",
    )(a, b)


def _gateup_kernel(h_ref, wg_ref, wu_ref, o_ref):
    h = h_ref[...]
    g = jnp.dot(h, wg_ref[...], preferred_element_type=F32)
    u = jnp.dot(h, wu_ref[...], preferred_element_type=F32)
    o_ref[...] = (g * jax.nn.sigmoid(g) * u).astype(o_ref.dtype)


def gate_up(h, wg, wu, job_requests=(), bm=1024, bn=512):
    m, k = h.shape
    _, n = wg.shape
    bm = min(bm, m)
    w_spec = pl.BlockSpec((k, bn), lambda i, j: (0, j))
    return _matmul_call(
        _gateup_kernel, "gate_up", [h, wg, wu],
        [pl.BlockSpec((bm, k), lambda i, j: (i, 0)), w_spec, w_spec],
        pl.BlockSpec((bm, bn), lambda i, j: (i, j)), jax.ShapeDtypeStruct((m, n), BF16),
        (m // bm, n // bn), job_requests, 56)


ATTN_KEY_BLOCK = 1024
ATTN_KEYS_PER_LOOP_TRIP = 2048
V_ROWS = HEAD_DIM + 16
LOG2E = 1.4426950408889634


def _rope_tables(segments):
    pos = jnp.concatenate([jnp.arange(n, dtype=jnp.int32) for _, n in segments])
    r = (pos // GRID_W).astype(F32)
    c = (pos % GRID_W).astype(F32)
    inv = 1.0 / (ROPE_THETA ** (jnp.arange(0, ROPE_HALF, 2, dtype=F32) / ROPE_HALF))
    ang_r = r[:, None] * inv
    ang_c = c[:, None] * inv
    cos = jnp.concatenate([jnp.cos(ang_r), jnp.cos(ang_r), jnp.cos(ang_c), jnp.cos(ang_c)], axis=-1)
    sin = jnp.concatenate([-jnp.sin(ang_r), jnp.sin(ang_r), -jnp.sin(ang_c), jnp.sin(ang_c)], axis=-1)
    return cos, sin


def _norm_rope(x, gain, cos, sin, upper):
    xn = _rms(x) * gain
    quarter = ROPE_HALF // 2
    partner = jnp.where(upper, pltpu.roll(xn, quarter, axis=1), pltpu.roll(xn, HEAD_DIM - quarter, axis=1))
    return xn * cos + partner * sin


def _qkv_prep_kernel(q_ref, k_ref, v_ref, cos_ref, sin_ref, qg_ref, kg_ref, qt_ref, ko_ref, vt_ref):
    cos = cos_ref[...]
    sin = sin_ref[...]
    rows = cos.shape[0]
    lane = lax.broadcasted_iota(jnp.int32, cos.shape, 1)
    upper = (lane & (ROPE_HALF // 2)) != 0
    q_scale = HEAD_DIM ** -0.5 * LOG2E
    qg = qg_ref[...]
    kg = kg_ref[...]
    for h in range(N_Q_HEADS):
        sl = slice(h * HEAD_DIM, (h + 1) * HEAD_DIM)
        y = _norm_rope(q_ref[:, sl].astype(F32), qg, cos, sin, upper) * q_scale
        qt_ref[sl, :] = y.T.astype(qt_ref.dtype)
    for h in range(N_KV_HEADS):
        sl = slice(h * HEAD_DIM, (h + 1) * HEAD_DIM)
        y = _norm_rope(k_ref[:, sl].astype(F32), kg, cos, sin, upper)
        ko_ref[:, sl] = y.astype(ko_ref.dtype)
        vt_ref[h, 0, :HEAD_DIM, :] = v_ref[:, sl].astype(F32).T.astype(vt_ref.dtype)
        vt_ref[h, 0, HEAD_DIM:, :] = jnp.ones((V_ROWS - HEAD_DIM, rows), vt_ref.dtype)


def qkv_prep(proj, cos, sin, q_gain, k_gain, key_block, rows=512):
    t = proj.shape[0]
    rows = min(rows, key_block)
    per_key_block = key_block // rows
    return pl.pallas_call(
        _qkv_prep_kernel,
        grid=(t // rows,),
        in_specs=[pl.BlockSpec((rows, ATTN_WIDTH), lambda i: (i, Q_OFF // ATTN_WIDTH)),
                  pl.BlockSpec((rows, KV_WIDTH), lambda i: (i, K_OFF // KV_WIDTH)),
                  pl.BlockSpec((rows, KV_WIDTH), lambda i: (i, V_OFF // KV_WIDTH)),
                  pl.BlockSpec((rows, HEAD_DIM), lambda i: (i, 0)),
                  pl.BlockSpec((rows, HEAD_DIM), lambda i: (i, 0)),
                  pl.BlockSpec((1, HEAD_DIM), lambda i: (0, 0)),
                  pl.BlockSpec((1, HEAD_DIM), lambda i: (0, 0))],
        out_specs=[pl.BlockSpec((ATTN_WIDTH, rows), lambda i: (0, i)),
                   pl.BlockSpec((rows, KV_WIDTH), lambda i: (i, 0)),
                   pl.BlockSpec((N_KV_HEADS, 1, V_ROWS, rows),
                                lambda i: (0, i // per_key_block, 0, i % per_key_block))],
        out_shape=[jax.ShapeDtypeStruct((ATTN_WIDTH, t), BF16),
                   jax.ShapeDtypeStruct((t, KV_WIDTH), BF16),
                   jax.ShapeDtypeStruct((N_KV_HEADS, t // key_block, V_ROWS, key_block), BF16)],
        compiler_params=_params(("parallel",), 40),
        name="qkv_prep",
    )(proj, proj, proj, cos, sin, q_gain.reshape(1, HEAD_DIM), k_gain.reshape(1, HEAD_DIM))


def _attn_kernel(qt_ref, k_ref, vt_ref, *rest, bk, nk):
    o_ref, m_scr, acc_scr, s_scr, smax_scr = rest[-5:]
    m_scr[...] = jnp.full(m_scr.shape, -jnp.inf, F32)
    acc_scr[...] = jnp.zeros(acc_scr.shape, F32)

    def scores(kb, h, slot):
        s = jnp.dot(kb, qt_ref[h * HEAD_DIM:(h + 1) * HEAD_DIM, :], preferred_element_type=F32)
        s_scr[slot] = s
        smax_scr[slot] = jnp.max(s, axis=0, keepdims=True)

    def key_block(j):
        return k_ref[pl.ds(pl.multiple_of(j * bk, bk), bk), :]

    scores(key_block(0), 0, 0)

    def body(j, carry):
        kb = key_block(j)
        kb_next = key_block(jnp.minimum(j + 1, nk - 1))
        vb = vt_ref[j]
        for h in range(Q_PER_KV):
            slot = h % 2
            if h + 1 < Q_PER_KV:
                scores(kb, h + 1, 1 - slot)
            else:
                scores(kb_next, 0, 1 - slot)
            m_prev = m_scr[h]
            m_new = jnp.maximum(m_prev, smax_scr[slot])
            alpha = jnp.exp2(m_prev - m_new)
            p = jnp.exp2(s_scr[slot] - m_new).astype(vb.dtype)
            acc_scr[h] = acc_scr[h] * alpha + jnp.dot(vb, p, preferred_element_type=F32)
            m_scr[h] = m_new
        return carry

    lax.fori_loop(0, nk, body, 0, unroll=max(1, ATTN_KEYS_PER_LOOP_TRIP // bk))
    for h in range(Q_PER_KV):
        acc = acc_scr[h]
        o = acc[:HEAD_DIM, :] / acc[HEAD_DIM:HEAD_DIM + 1, :]
        o_ref[:, h * HEAD_DIM:(h + 1) * HEAD_DIM] = o.T.astype(o_ref.dtype)


def attention(qt, k_rot, vt, prev_out, row0, batch, seq, bq=512):
    t = k_rot.shape[0]
    bk = vt.shape[-1]
    bq = min(bq, seq)
    assert row0 % seq == 0 and seq % bq == 0 and seq % bk == 0
    group_w = Q_PER_KV * HEAD_DIM
    nq = seq // bq
    nk = seq // bk
    in_specs = [
        pl.BlockSpec((group_w, bq), lambda b, h, i: (h, row0 // bq + b * nq + i)),
        pl.BlockSpec((seq, HEAD_DIM), lambda b, h, i: (row0 // seq + b, h)),
        pl.BlockSpec((None, nk, V_ROWS, bk), lambda b, h, i: (h, row0 // seq + b, 0, 0)),
    ]
    args = [qt, k_rot, vt]
    aliases = {}
    if prev_out is not None:
        in_specs.append(pl.BlockSpec(memory_space=pl.ANY))
        args.append(prev_out)
        aliases = {3: 0}
    return pl.pallas_call(
        functools.partial(_attn_kernel, bk=bk, nk=nk),
        grid=(batch, N_KV_HEADS, nq),
        in_specs=in_specs,
        out_specs=pl.BlockSpec((bq, group_w), lambda b, h, i: (row0 // bq + b * nq + i, h)),
        out_shape=jax.ShapeDtypeStruct((t, ATTN_WIDTH), BF16),
        scratch_shapes=[pltpu.VMEM((Q_PER_KV, 1, bq), F32),
                        pltpu.VMEM((Q_PER_KV, V_ROWS, bq), F32),
                        pltpu.VMEM((2, bk, bq), F32),
                        pltpu.VMEM((2, 1, bq), F32)],
        input_output_aliases=aliases,
        compiler_params=_params(("parallel", "parallel", "parallel"), 48),
        name="attention",
    )(*args)


def _convpool_kernel(ch_ref, cb_ref, cc_ref, pu_ref,
                     chp_ref, ccp_ref, pup_ref, chn_ref, ccn_ref, pun_ref,
                     cw_ref, pw_ref, ps_ref, o_ref, *, rows, segments):
    i = pl.program_id(0)
    start = i * rows
    pos0 = start
    seq_len = jnp.int32(0)
    has_prev = jnp.bool_(True)
    has_next = jnp.bool_(True)
    for seg_start, seg_len in segments:
        inside = (start >= seg_start) & (start < seg_start + seg_len)
        pos0 = jnp.where(inside, start - seg_start, pos0)
        seq_len = jnp.where(inside, seg_len, seq_len)
        has_prev = has_prev & (start != seg_start)
        has_next = has_next & (start + rows != seg_start + seg_len)
    prev_on = has_prev.astype(F32)
    next_on = has_next.astype(F32)

    def extended(cur_ref, prev_ref, next_ref):
        return jnp.concatenate([prev_ref[...].astype(F32) * prev_on,
                                cur_ref[...].astype(F32),
                                next_ref[...].astype(F32) * next_on], axis=0)

    ext_rows = rows + 2 * HALO

    def shifted(x, d):
        return pltpu.roll(x, d % ext_rows, axis=0)

    u = extended(cc_ref, ccp_ref, ccn_ref) * extended(ch_ref, chp_ref, chn_ref)
    cw = cw_ref[...]
    y = shifted(u, 1) * cw[0:1, :] + u * cw[1:2, :] + shifted(u, -1) * cw[2:3, :]
    o_ref[:, :CONV_WIDTH] = (cb_ref[...].astype(F32) * y[HALO:HALO + rows, :]).astype(o_ref.dtype)

    x = extended(pu_ref, pup_ref, pun_ref)
    pos = (pos0 + lax.broadcasted_iota(jnp.int32, (rows, 1), 0))
    for gi, w in enumerate(POOL_WINDOWS):
        sl = slice(gi * POOL_GROUP, (gi + 1) * POOL_GROUP)
        xg = x[:, sl]
        tot = xg + shifted(xg, 1)
        half = 1
        while 2 * half < w:
            tot = shifted(tot, half) + shifted(tot, -half)
            half *= 2
        hi = jnp.minimum(pos + w // 2, seq_len)
        lo = jnp.maximum(pos - w // 2, 0)
        cnt = (hi - lo).astype(F32)
        m = tot[HALO:HALO + rows, :] / cnt - xg[HALO:HALO + rows, :]
        yg = jnp.dot(m.astype(BF16), pw_ref[gi], preferred_element_type=F32)
        o_ref[:, CONV_WIDTH + gi * POOL_GROUP:CONV_WIDTH + (gi + 1) * POOL_GROUP] = (
            yg * ps_ref[:, sl]).astype(o_ref.dtype)


def conv_pool(proj, conv_w, pool_w, pool_scale, segments, rows=512):
    t = proj.shape[0]
    rows = min(rows, min(n for _, n in segments))
    assert all(s % rows == 0 and n % rows == 0 for s, n in segments)
    w = CONV_WIDTH
    per = rows // HALO
    last = t // HALO - 1

    def cur(off):
        return pl.BlockSpec((rows, w), lambda i: (i, off // w))

    def prev(off):
        return pl.BlockSpec((HALO, w), lambda i: (jnp.maximum(i * per - 1, 0), off // w))

    def nxt(off):
        return pl.BlockSpec((HALO, w), lambda i: (jnp.minimum((i + 1) * per, last), off // w))

    kernel = functools.partial(_convpool_kernel, rows=rows, segments=segments)
    return pl.pallas_call(
        kernel,
        grid=(t // rows,),
        in_specs=[cur(CH_OFF), cur(CB_OFF), cur(CC_OFF), cur(PU_OFF),
                  prev(CH_OFF), prev(CC_OFF), prev(PU_OFF),
                  nxt(CH_OFF), nxt(CC_OFF), nxt(PU_OFF),
                  pl.BlockSpec((3, w), lambda i: (0, 0)),
                  pl.BlockSpec((len(POOL_WINDOWS), POOL_GROUP, POOL_GROUP), lambda i: (0, 0, 0)),
                  pl.BlockSpec((1, POOL_WIDTH), lambda i: (0, 0))],
        out_specs=pl.BlockSpec((rows, CONV_WIDTH + POOL_WIDTH), lambda i: (i, 0)),
        out_shape=jax.ShapeDtypeStruct((t, CONV_WIDTH + POOL_WIDTH), BF16),
        compiler_params=_params(("parallel",), 48),
        name="conv_pool",
    )(proj, proj, proj, proj, proj, proj, proj, proj, proj, proj,
      conv_w, pool_w, pool_scale.reshape(1, POOL_WIDTH))


def _trunk(x_parts, segments, attn_calls, ln_mix_pre, ln_mix_post, ln_ffn_pre, ln_ffn_post, q_norm, k_norm,
           w_in, conv_w, pool_w, pool_scale, w_out, w_gate, w_up, w_down):
    total = sum(n for _, _, n in x_parts)
    cos, sin = _rope_tables(segments)
    key_block = min(ATTN_KEY_BLOCK, min(n for _, n in segments))
    w_in_bf16 = cast_weight(w_in, 0)
    d_model = w_out.shape[2]
    x = f = None
    for l in range(DEPTH):
        mixer_weight_jobs = [(w_out, l, w_out.shape[1], d_model), (w_gate, l, d_model, D_FF_PAD),
                             (w_up, l, d_model, D_FF_PAD)]
        if l == 0:
            h = norm_cast_parts(x_parts, ln_mix_pre[0], total)
            proj, cast = matmul(h, w_in_bf16, BF16, mixer_weight_jobs)
        else:
            proj, x, cast = resnorm_matmul(_dot_single, "resnorm_matmul_in", f, [(x, 0, total)], ln_ffn_post[l - 1],
                                           ln_mix_pre[l], [w_in_bf16], 512, BF16, mixer_weight_jobs)
        w_out_bf16, w_gate_bf16, w_up_bf16 = cast
        qt, k_rot, vt = qkv_prep(proj, cos, sin, q_norm[l], k_norm[l], key_block)
        a_out = None
        for row0, batch, seq in attn_calls:
            a_out = attention(qt, k_rot, vt, a_out, row0, batch, seq)
        cp_out = conv_pool(proj, conv_w[l], pool_w[l].astype(BF16), pool_scale[l], segments)
        mix = matmul_two_lhs(a_out, cp_out, w_out_bf16)
        jobs = [(w_down, l, D_FF_PAD, d_model)]
        if l + 1 < DEPTH:
            jobs.append((w_in, l + 1, d_model, w_in.shape[2]))
        act, x, cast = resnorm_matmul(_dot_gate_up, "resnorm_gate_up", mix, x_parts if l == 0 else [(x, 0, total)],
                                      ln_mix_post[l], ln_ffn_pre[l], [w_gate_bf16, w_up_bf16], 512, BF16, jobs)
        w_down_bf16 = cast[0]
        w_in_bf16 = cast[1] if l + 1 < DEPTH else None
        f = matmul_split_k(act, w_down_bf16, bk=D_FF_PAD // 4)
    return resnorm_last_to_parts(f, x, ln_ffn_post[DEPTH - 1], [(row0, n) for _, row0, n in x_parts])


def kernel(x_prompt, x_sample, ln_mix_pre, ln_mix_post, ln_ffn_pre, ln_ffn_post, q_norm, k_norm,
           w_in, conv_w, pool_w, pool_scale, w_out, w_gate, w_up, w_down):
    pb, ps, d = x_prompt.shape
    sb, ss, _ = x_sample.shape
    x_parts = [(x_prompt.reshape(pb * ps, d), 0, pb * ps), (x_sample.reshape(sb * ss, d), pb * ps, sb * ss)]
    segments = tuple((b * ps, ps) for b in range(pb)) + tuple((pb * ps + b * ss, ss) for b in range(sb))
    attn_calls = ((0, pb, ps), (pb * ps, sb, ss))
    y_prompt, y_sample = _trunk(x_parts, segments, attn_calls, ln_mix_pre, ln_mix_post, ln_ffn_pre, ln_ffn_post,
                                q_norm, k_norm, w_in, conv_w, pool_w, pool_scale, w_out, w_gate, w_up, w_down)
    return (y_prompt.reshape(pb, ps, d), y_sample.reshape(sb, ss, d))
```

```python
import functools

import jax
import jax.numpy as jnp
from jax import lax
from jax.experimental import pallas as pl
from jax.experimental.pallas import tpu as pltpu

D_MODEL = 4096
DEPTH = 2
GRID_W = 64
HEAD_DIM = 128
N_Q_HEADS = 16
N_KV_HEADS = 4
Q_PER_KV = N_Q_HEADS // N_KV_HEADS
ATTN_WIDTH = N_Q_HEADS * HEAD_DIM
KV_WIDTH = N_KV_HEADS * HEAD_DIM
CONV_WIDTH = D_MODEL // 4
POOL_WIDTH = D_MODEL // 4
POOL_WINDOWS = (2, 4, 8, 16)
POOL_GROUP = POOL_WIDTH // len(POOL_WINDOWS)
IN_WIDTH = ATTN_WIDTH + 2 * KV_WIDTH + 3 * CONV_WIDTH + POOL_WIDTH
D_FF = 11008
D_FF_PAD = 11264
ROPE_HALF = HEAD_DIM // 2
ROPE_THETA = 10000.0
EPS = 1e-6

Q_OFF = 0
K_OFF = ATTN_WIDTH
V_OFF = K_OFF + KV_WIDTH
CH_OFF = V_OFF + KV_WIDTH
CB_OFF = CH_OFF + CONV_WIDTH
CC_OFF = CB_OFF + CONV_WIDTH
PU_OFF = CC_OFF + CONV_WIDTH

HALO = 16
V7X_VMEM_BYTES = 64 * 1024 * 1024

F32 = jnp.float32
BF16 = jnp.bfloat16


def _params(semantics, vmem_mb):
    assert vmem_mb * 1024 * 1024 < V7X_VMEM_BYTES
    return pltpu.CompilerParams(dimension_semantics=semantics,
                                vmem_limit_bytes=vmem_mb * 1024 * 1024)


def _rms(x):
    return x * lax.rsqrt(jnp.mean(x * x, axis=-1, keepdims=True) + EPS)


def _rows_spec(rows, d, row0):
    assert row0 % rows == 0
    return pl.BlockSpec((rows, d), lambda i: (row0 // rows + i, 0))


def _rowwise_call(body, name, n_rows, ins, gains, outs, prev_outs, rows=256):
    d = ins[0][0].shape[1]
    in_specs = [_rows_spec(rows, d, r0) for _, r0 in ins]
    in_specs += [pl.BlockSpec((1, d), lambda i: (0, 0)) for _ in gains]
    args = [a for a, _ in ins] + [g.reshape(1, d) for g in gains]
    aliases = {}
    kernel = body
    if prev_outs is not None:
        n_in = len(args)
        in_specs += [pl.BlockSpec(memory_space=pl.ANY) for _ in prev_outs]
        args += list(prev_outs)
        aliases = {n_in + k: k for k in range(len(prev_outs))}

        def kernel(*refs):
            body(*refs[:n_in], *refs[n_in + len(prev_outs):])

    return pl.pallas_call(
        kernel,
        grid=(n_rows // rows,),
        in_specs=in_specs,
        out_specs=[_rows_spec(rows, d, r0) for _, r0, _ in outs],
        out_shape=[jax.ShapeDtypeStruct((total, d), dt) for total, _, dt in outs],
        input_output_aliases=aliases,
        compiler_params=_params(("parallel",), 40),
        name=name,
    )(*args)


def _norm_cast_kernel(x_ref, g_ref, h_ref):
    h_ref[...] = (_rms(x_ref[...]) * g_ref[...]).astype(h_ref.dtype)


def _resnorm_kernel(f_ref, x_ref, g_post_ref, g_next_ref, xo_ref, h_ref):
    xn = x_ref[...] + _rms(f_ref[...]) * g_post_ref[...]
    xo_ref[...] = xn
    h_ref[...] = (_rms(xn) * g_next_ref[...]).astype(h_ref.dtype)


def _resnorm_last_kernel(f_ref, x_ref, g_post_ref, xo_ref):
    xo_ref[...] = x_ref[...] + _rms(f_ref[...]) * g_post_ref[...]


def norm_cast_parts(parts, g, total):
    h = None
    for x, row0, n in parts:
        h, = _rowwise_call(_norm_cast_kernel, "norm_cast", n, [(x, 0)], [g], [(total, row0, BF16)],
                           None if h is None else [h])
    return h


def resnorm_from_parts(f, parts, g_post, g_next, total):
    outs = None
    for x, row0, n in parts:
        outs = _rowwise_call(_resnorm_kernel, "resnorm", n, [(f, row0), (x, 0)], [g_post, g_next],
                             [(total, row0, F32), (total, row0, BF16)], outs)
    return outs


def resnorm(f, x, g_post, g_next):
    total = x.shape[0]
    return _rowwise_call(_resnorm_kernel, "resnorm", total, [(f, 0), (x, 0)], [g_post, g_next],
                         [(total, 0, F32), (total, 0, BF16)], None)


def resnorm_last_to_parts(f, x, g_post, parts):
    return [_rowwise_call(_resnorm_last_kernel, "resnorm_last", n, [(f, row0), (x, row0)], [g_post],
                          [(n, 0, F32)], None)[0] for row0, n in parts]


CAST_JOB_ROW_CHOICES = (32, 64, 128, 256, 512, 1024)


class CastJob:
    def __init__(self, w, layer, out_rows, out_cols, n_steps):
        _, n_rows, n_cols = w.shape
        fits = [r for r in CAST_JOB_ROW_CHOICES
                if n_rows % r == 0 and out_rows % r == 0 and out_rows // r <= n_steps]
        self.feasible = bool(fits)
        if not fits:
            return
        self.w, self.layer, self.rows = w, layer, fits[0]
        self.n_cols, self.out_rows, self.out_cols = n_cols, out_rows, out_cols
        self.in_blocks, self.out_blocks = n_rows // self.rows, out_rows // self.rows

    def specs(self, step_of):
        in_spec = pl.BlockSpec((None, self.rows, self.n_cols),
                               lambda *g: (self.layer, jnp.minimum(step_of(*g), self.in_blocks - 1), 0))
        out_spec = pl.BlockSpec((self.rows, self.out_cols),
                                lambda *g: (jnp.minimum(step_of(*g), self.out_blocks - 1), 0))
        return in_spec, out_spec, jax.ShapeDtypeStruct((self.out_rows, self.out_cols), BF16)

    def run(self, step, w_ref, o_ref):
        @pl.when(step < self.in_blocks)
        def _():
            o_ref[:, :self.n_cols] = w_ref[...].astype(o_ref.dtype)
            if self.out_cols > self.n_cols:
                o_ref[:, self.n_cols:] = jnp.zeros((self.rows, self.out_cols - self.n_cols), o_ref.dtype)

        if self.out_blocks > self.in_blocks:
            @pl.when((step >= self.in_blocks) & (step < self.out_blocks))
            def _():
                o_ref[...] = jnp.zeros(o_ref.shape, o_ref.dtype)


def _cast_weight_kernel(w_ref, o_ref, *, valid_rows, valid_cols):
    br, bc = o_ref.shape
    r = pl.program_id(0) * br + lax.broadcasted_iota(jnp.int32, (br, bc), 0)
    c = pl.program_id(1) * bc + lax.broadcasted_iota(jnp.int32, (br, bc), 1)
    o_ref[...] = jnp.where((r < valid_rows) & (c < valid_cols), w_ref[...], 0.0).astype(o_ref.dtype)


def cast_weight(w, layer, out_rows=None, out_cols=None, br=512, bc=1024):
    _, rows, cols = w.shape
    out_rows = out_rows or rows
    out_cols = out_cols or cols
    return pl.pallas_call(
        functools.partial(_cast_weight_kernel, valid_rows=rows, valid_cols=cols),
        grid=(out_rows // br, out_cols // bc),
        in_specs=[pl.BlockSpec((None, br, bc), lambda i, j: (layer, i, j))],
        out_specs=pl.BlockSpec((br, bc), lambda i, j: (i, j)),
        out_shape=jax.ShapeDtypeStruct((out_rows, out_cols), BF16),
        compiler_params=_params(("parallel", "parallel"), 32),
        name="cast_weight",
    )(w)


def _with_cast_jobs(body, n_in, n_out, jobs, n_j):
    n_jobs = len(jobs)

    def kernel(*refs):
        ins, job_ins = refs[:n_in], refs[n_in:n_in + n_jobs]
        outs = refs[n_in + n_jobs:n_in + n_jobs + n_out]
        job_outs = refs[n_in + n_jobs + n_out:n_in + n_jobs + n_out + n_jobs]
        body(*ins, *outs, *refs[n_in + n_jobs + n_out + n_jobs:])
        step = pl.program_id(0) * n_j + pl.program_id(1)
        for job, w_ref, o_ref in zip(jobs, job_ins, job_outs):
            job.run(step, w_ref, o_ref)

    return kernel


def _matmul_call(body, name, ins, in_specs, out_spec, out_shape, grid, job_requests, vmem_mb):
    n_i, n_j = grid
    jobs = [CastJob(w, layer, r, c, n_i * n_j) for w, layer, r, c in job_requests]
    riding = [job for job in jobs if job.feasible]
    job_specs = [job.specs(lambda i, j: i * n_j + j) for job in riding]
    outs = pl.pallas_call(
        _with_cast_jobs(body, len(ins), 1, riding, n_j),
        grid=grid,
        in_specs=list(in_specs) + [s[0] for s in job_specs],
        out_specs=[out_spec] + [s[1] for s in job_specs],
        out_shape=[out_shape] + [s[2] for s in job_specs],
        compiler_params=_params(("arbitrary", "arbitrary"), vmem_mb),
        name=name,
    )(*ins, *[job.w for job in riding])
    cast = iter(outs[1:])
    weights = [next(cast) if job.feasible else cast_weight(w, layer, r, c)
               for job, (w, layer, r, c) in zip(jobs, job_requests)]
    return outs[0], weights


def _mm_kernel(a_ref, b_ref, o_ref):
    o_ref[...] = jnp.dot(a_ref[...], b_ref[...], preferred_element_type=F32).astype(o_ref.dtype)


def matmul(a, b, out_dtype, job_requests=(), bm=1024, bn=1024):
    m, k = a.shape
    _, n = b.shape
    bm = min(bm, m)
    return _matmul_call(
        _mm_kernel, "matmul_full_k", [a, b],
        [pl.BlockSpec((bm, k), lambda i, j: (i, 0)), pl.BlockSpec((k, bn), lambda i, j: (0, j))],
        pl.BlockSpec((bm, bn), lambda i, j: (i, j)), jax.ShapeDtypeStruct((m, n), out_dtype),
        (m // bm, n // bn), job_requests, 56)


def _mm2_kernel(a1_ref, a2_ref, b1_ref, b2_ref, o_ref):
    o_ref[...] = (jnp.dot(a1_ref[...], b1_ref[...], preferred_element_type=F32)
                  + jnp.dot(a2_ref[...], b2_ref[...], preferred_element_type=F32))


def matmul_two_lhs(a1, a2, b, bm=1024, bn=1024):
    m, k1 = a1.shape
    _, k2 = a2.shape
    assert k1 == k2 and b.shape[0] == k1 + k2
    n = b.shape[1]
    bm = min(bm, m)
    return pl.pallas_call(
        _mm2_kernel,
        grid=(m // bm, n // bn),
        in_specs=[pl.BlockSpec((bm, k1), lambda i, j: (i, 0)),
                  pl.BlockSpec((bm, k2), lambda i, j: (i, 0)),
                  pl.BlockSpec((k1, bn), lambda i, j: (0, j)),
                  pl.BlockSpec((k2, bn), lambda i, j: (1, j))],
        out_specs=pl.BlockSpec((bm, bn), lambda i, j: (i, j)),
        out_shape=jax.ShapeDtypeStruct((m, n), F32),
        compiler_params=_params(("parallel", "parallel"), 48),
        name="matmul_two_lhs",
    )(a1, a2, b, b)


def _mm_acc_kernel(a_ref, b_ref, o_ref):
    @pl.when(pl.program_id(2) == 0)
    def _():
        o_ref[...] = jnp.dot(a_ref[...], b_ref[...], preferred_element_type=F32)

    @pl.when(pl.program_id(2) != 0)
    def _():
        o_ref[...] += jnp.dot(a_ref[...], b_ref[...], preferred_element_type=F32)


def matmul_split_k(a, b, bk, bm=1024, bn=1024):
    m, k = a.shape
    _, n = b.shape
    bm = min(bm, m)
    return pl.pallas_call(
        _mm_acc_kernel,
        grid=(m // bm, n // bn, k // bk),
        in_specs=[pl.BlockSpec((bm, bk), lambda i, j, kk: (i, kk)),
                  pl.BlockSpec((bk, bn), lambda i, j, kk: (kk, j))],
        out_specs=pl.BlockSpec((bm, bn), lambda i, j, kk: (i, j)),
        out_shape=jax.ShapeDtypeStruct((m, n), F32),
        compiler_params=_params(("parallel", "parallel", "arbitrary"), 48),
        name="matmul_split_k",
    )(a, b)


def _gateup_kernel(h_ref, wg_ref, wu_ref, o_ref):
    h = h_ref[...]
    g = jnp.dot(h, wg_ref[...], preferred_element_type=F32)
    u = jnp.dot(h, wu_ref[...], preferred_element_type=F32)
    o_ref[...] = (g * jax.nn.sigmoid(g) * u).astype(o_ref.dtype)


def gate_up(h, wg, wu, job_requests=(), bm=1024, bn=512):
    m, k = h.shape
    _, n = wg.shape
    bm = min(bm, m)
    w_spec = pl.BlockSpec((k, bn), lambda i, j: (0, j))
    return _matmul_call(
        _gateup_kernel, "gate_up", [h, wg, wu],
        [pl.BlockSpec((bm, k), lambda i, j: (i, 0)), w_spec, w_spec],
        pl.BlockSpec((bm, bn), lambda i, j: (i, j)), jax.ShapeDtypeStruct((m, n), BF16),
        (m // bm, n // bn), job_requests, 56)


ATTN_KEY_BLOCK = 1024
ATTN_KEYS_PER_LOOP_TRIP = 2048
V_ROWS = HEAD_DIM + 16
LOG2E = 1.4426950408889634


def _rope_tables(segments):
    pos = jnp.concatenate([jnp.arange(n, dtype=jnp.int32) for _, n in segments])
    r = (pos // GRID_W).astype(F32)
    c = (pos % GRID_W).astype(F32)
    inv = 1.0 / (ROPE_THETA ** (jnp.arange(0, ROPE_HALF, 2, dtype=F32) / ROPE_HALF))
    ang_r = r[:, None] * inv
    ang_c = c[:, None] * inv
    cos = jnp.concatenate([jnp.cos(ang_r), jnp.cos(ang_r), jnp.cos(ang_c), jnp.cos(ang_c)], axis=-1)
    sin = jnp.concatenate([-jnp.sin(ang_r), jnp.sin(ang_r), -jnp.sin(ang_c), jnp.sin(ang_c)], axis=-1)
    return cos, sin


def _norm_rope(x, gain, cos, sin, upper):
    xn = _rms(x) * gain
    quarter = ROPE_HALF // 2
    partner = jnp.where(upper, pltpu.roll(xn, quarter, axis=1), pltpu.roll(xn, HEAD_DIM - quarter, axis=1))
    return xn * cos + partner * sin


def _qkv_prep_kernel(q_ref, k_ref, v_ref, cos_ref, sin_ref, qg_ref, kg_ref, qt_ref, ko_ref, vt_ref):
    cos = cos_ref[...]
    sin = sin_ref[...]
    rows = cos.shape[0]
    lane = lax.broadcasted_iota(jnp.int32, cos.shape, 1)
    upper = (lane & (ROPE_HALF // 2)) != 0
    q_scale = HEAD_DIM ** -0.5 * LOG2E
    qg = qg_ref[...]
    kg = kg_ref[...]
    for h in range(N_Q_HEADS):
        sl = slice(h * HEAD_DIM, (h + 1) * HEAD_DIM)
        y = _norm_rope(q_ref[:, sl].astype(F32), qg, cos, sin, upper) * q_scale
        qt_ref[sl, :] = y.T.astype(qt_ref.dtype)
    for h in range(N_KV_HEADS):
        sl = slice(h * HEAD_DIM, (h + 1) * HEAD_DIM)
        y = _norm_rope(k_ref[:, sl].astype(F32), kg, cos, sin, upper)
        ko_ref[:, sl] = y.astype(ko_ref.dtype)
        vt_ref[h, 0, :HEAD_DIM, :] = v_ref[:, sl].astype(F32).T.astype(vt_ref.dtype)
        vt_ref[h, 0, HEAD_DIM:, :] = jnp.ones((V_ROWS - HEAD_DIM, rows), vt_ref.dtype)


def qkv_prep(proj, cos, sin, q_gain, k_gain, key_block, rows=512):
    t = proj.shape[0]
    rows = min(rows, key_block)
    per_key_block = key_block // rows
    return pl.pallas_call(
        _qkv_prep_kernel,
        grid=(t // rows,),
        in_specs=[pl.BlockSpec((rows, ATTN_WIDTH), lambda i: (i, Q_OFF // ATTN_WIDTH)),
                  pl.BlockSpec((rows, KV_WIDTH), lambda i: (i, K_OFF // KV_WIDTH)),
                  pl.BlockSpec((rows, KV_WIDTH), lambda i: (i, V_OFF // KV_WIDTH)),
                  pl.BlockSpec((rows, HEAD_DIM), lambda i: (i, 0)),
                  pl.BlockSpec((rows, HEAD_DIM), lambda i: (i, 0)),
                  pl.BlockSpec((1, HEAD_DIM), lambda i: (0, 0)),
                  pl.BlockSpec((1, HEAD_DIM), lambda i: (0, 0))],
        out_specs=[pl.BlockSpec((ATTN_WIDTH, rows), lambda i: (0, i)),
                   pl.BlockSpec((rows, KV_WIDTH), lambda i: (i, 0)),
                   pl.BlockSpec((N_KV_HEADS, 1, V_ROWS, rows),
                                lambda i: (0, i // per_key_block, 0, i % per_key_block))],
        out_shape=[jax.ShapeDtypeStruct((ATTN_WIDTH, t), BF16),
                   jax.ShapeDtypeStruct((t, KV_WIDTH), BF16),
                   jax.ShapeDtypeStruct((N_KV_HEADS, t // key_block, V_ROWS, key_block), BF16)],
        compiler_params=_params(("parallel",), 40),
        name="qkv_prep",
    )(proj, proj, proj, cos, sin, q_gain.reshape(1, HEAD_DIM), k_gain.reshape(1, HEAD_DIM))


def _attn_kernel(qt_ref, k_ref, vt_ref, *rest, bk, nk):
    o_ref, m_scr, acc_scr, s_scr, smax_scr = rest[-5:]
    m_scr[...] = jnp.full(m_scr.shape, -jnp.inf, F32)
    acc_scr[...] = jnp.zeros(acc_scr.shape, F32)

    def scores(kb, h, slot):
        s = jnp.dot(kb, qt_ref[h * HEAD_DIM:(h + 1) * HEAD_DIM, :], preferred_element_type=F32)
        s_scr[slot] = s
        smax_scr[slot] = jnp.max(s, axis=0, keepdims=True)

    def key_block(j):
        return k_ref[pl.ds(pl.multiple_of(j * bk, bk), bk), :]

    scores(key_block(0), 0, 0)

    def body(j, carry):
        kb = key_block(j)
        kb_next = key_block(jnp.minimum(j + 1, nk - 1))
        vb = vt_ref[j]
        for h in range(Q_PER_KV):
            slot = h % 2
            if h + 1 < Q_PER_KV:
                scores(kb, h + 1, 1 - slot)
            else:
                scores(kb_next, 0, 1 - slot)
            m_prev = m_scr[h]
            m_new = jnp.maximum(m_prev, smax_scr[slot])
            alpha = jnp.exp2(m_prev - m_new)
            p = jnp.exp2(s_scr[slot] - m_new).astype(vb.dtype)
            acc_scr[h] = acc_scr[h] * alpha + jnp.dot(vb, p, preferred_element_type=F32)
            m_scr[h] = m_new
        return carry

    lax.fori_loop(0, nk, body, 0, unroll=max(1, ATTN_KEYS_PER_LOOP_TRIP // bk))
    for h in range(Q_PER_KV):
        acc = acc_scr[h]
        o = acc[:HEAD_DIM, :] / acc[HEAD_DIM:HEAD_DIM + 1, :]
        o_ref[:, h * HEAD_DIM:(h + 1) * HEAD_DIM] = o.T.astype(o_ref.dtype)


def attention(qt, k_rot, vt, prev_out, row0, batch, seq, bq=512):
    t = k_rot.shape[0]
    bk = vt.shape[-1]
    bq = min(bq, seq)
    assert row0 % seq == 0 and seq % bq == 0 and seq % bk == 0
    group_w = Q_PER_KV * HEAD_DIM
    nq = seq // bq
    nk = seq // bk
    in_specs = [
        pl.BlockSpec((group_w, bq), lambda b, h, i: (h, row0 // bq + b * nq + i)),
        pl.BlockSpec((seq, HEAD_DIM), lambda b, h, i: (row0 // seq + b, h)),
        pl.BlockSpec((None, nk, V_ROWS, bk), lambda b, h, i: (h, row0 // seq + b, 0, 0)),
    ]
    args = [qt, k_rot, vt]
    aliases = {}
    if prev_out is not None:
        in_specs.append(pl.BlockSpec(memory_space=pl.ANY))
        args.append(prev_out)
        aliases = {3: 0}
    return pl.pallas_call(
        functools.partial(_attn_kernel, bk=bk, nk=nk),
        grid=(batch, N_KV_HEADS, nq),
        in_specs=in_specs,
        out_specs=pl.BlockSpec((bq, group_w), lambda b, h, i: (row0 // bq + b * nq + i, h)),
        out_shape=jax.ShapeDtypeStruct((t, ATTN_WIDTH), BF16),
        scratch_shapes=[pltpu.VMEM((Q_PER_KV, 1, bq), F32),
                        pltpu.VMEM((Q_PER_KV, V_ROWS, bq), F32),
                        pltpu.VMEM((2, bk, bq), F32),
                        pltpu.VMEM((2, 1, bq), F32)],
        input_output_aliases=aliases,
        compiler_params=_params(("parallel", "parallel", "parallel"), 48),
        name="attention",
    )(*args)


def _convpool_kernel(ch_ref, cb_ref, cc_ref, pu_ref,
                     chp_ref, ccp_ref, pup_ref, chn_ref, ccn_ref, pun_ref,
                     cw_ref, pw_ref, ps_ref, o_ref, *, rows, segments):
    i = pl.program_id(0)
    start = i * rows
    pos0 = start
    seq_len = jnp.int32(0)
    has_prev = jnp.bool_(True)
    has_next = jnp.bool_(True)
    for seg_start, seg_len in segments:
        inside = (start >= seg_start) & (start < seg_start + seg_len)
        pos0 = jnp.where(inside, start - seg_start, pos0)
        seq_len = jnp.where(inside, seg_len, seq_len)
        has_prev = has_prev & (start != seg_start)
        has_next = has_next & (start + rows != seg_start + seg_len)
    prev_on = has_prev.astype(F32)
    next_on = has_next.astype(F32)

    def extended(cur_ref, prev_ref, next_ref):
        return jnp.concatenate([prev_ref[...].astype(F32) * prev_on,
                                cur_ref[...].astype(F32),
                                next_ref[...].astype(F32) * next_on], axis=0)

    ext_rows = rows + 2 * HALO

    def shifted(x, d):
        return pltpu.roll(x, d % ext_rows, axis=0)

    u = extended(cc_ref, ccp_ref, ccn_ref) * extended(ch_ref, chp_ref, chn_ref)
    cw = cw_ref[...]
    y = shifted(u, 1) * cw[0:1, :] + u * cw[1:2, :] + shifted(u, -1) * cw[2:3, :]
    o_ref[:, :CONV_WIDTH] = (cb_ref[...].astype(F32) * y[HALO:HALO + rows, :]).astype(o_ref.dtype)

    x = extended(pu_ref, pup_ref, pun_ref)
    pos = (pos0 + lax.broadcasted_iota(jnp.int32, (rows, 1), 0))
    for gi, w in enumerate(POOL_WINDOWS):
        sl = slice(gi * POOL_GROUP, (gi + 1) * POOL_GROUP)
        xg = x[:, sl]
        tot = xg + shifted(xg, 1)
        half = 1
        while 2 * half < w:
            tot = shifted(tot, half) + shifted(tot, -half)
            half *= 2
        hi = jnp.minimum(pos + w // 2, seq_len)
        lo = jnp.maximum(pos - w // 2, 0)
        cnt = (hi - lo).astype(F32)
        m = tot[HALO:HALO + rows, :] / cnt - xg[HALO:HALO + rows, :]
        yg = jnp.dot(m.astype(BF16), pw_ref[gi], preferred_element_type=F32)
        o_ref[:, CONV_WIDTH + gi * POOL_GROUP:CONV_WIDTH + (gi + 1) * POOL_GROUP] = (
            yg * ps_ref[:, sl]).astype(o_ref.dtype)


def conv_pool(proj, conv_w, pool_w, pool_scale, segments, rows=512):
    t = proj.shape[0]
    rows = min(rows, min(n for _, n in segments))
    assert all(s % rows == 0 and n % rows == 0 for s, n in segments)
    w = CONV_WIDTH
    per = rows // HALO
    last = t // HALO - 1

    def cur(off):
        return pl.BlockSpec((rows, w), lambda i: (i, off // w))

    def prev(off):
        return pl.BlockSpec((HALO, w), lambda i: (jnp.maximum(i * per - 1, 0), off // w))

    def nxt(off):
        return pl.BlockSpec((HALO, w), lambda i: (jnp.minimum((i + 1) * per, last), off // w))

    kernel = functools.partial(_convpool_kernel, rows=rows, segments=segments)
    return pl.pallas_call(
        kernel,
        grid=(t // rows,),
        in_specs=[cur(CH_OFF), cur(CB_OFF), cur(CC_OFF), cur(PU_OFF),
                  prev(CH_OFF), prev(CC_OFF), prev(PU_OFF),
                  nxt(CH_OFF), nxt(CC_OFF), nxt(PU_OFF),
                  pl.BlockSpec((3, w), lambda i: (0, 0)),
                  pl.BlockSpec((len(POOL_WINDOWS), POOL_GROUP, POOL_GROUP), lambda i: (0, 0, 0)),
                  pl.BlockSpec((1, POOL_WIDTH), lambda i: (0, 0))],
        out_specs=pl.BlockSpec((rows, CONV_WIDTH + POOL_WIDTH), lambda i: (i, 0)),
        out_shape=jax.ShapeDtypeStruct((t, CONV_WIDTH + POOL_WIDTH), BF16),
        compiler_params=_params(("parallel",), 48),
        name="conv_pool",
    )(proj, proj, proj, proj, proj, proj, proj, proj, proj, proj,
      conv_w, pool_w, pool_scale.reshape(1, POOL_WIDTH))


def _trunk(x_parts, segments, attn_calls, ln_mix_pre, ln_mix_post, ln_ffn_pre, ln_ffn_post, q_norm, k_norm,
           w_in, conv_w, pool_w, pool_scale, w_out, w_gate, w_up, w_down):
    total = sum(n for _, _, n in x_parts)
    cos, sin = _rope_tables(segments)
    key_block = min(ATTN_KEY_BLOCK, min(n for _, n in segments))
    h = norm_cast_parts(x_parts, ln_mix_pre[0], total)
    x = None
    w_in_bf16 = cast_weight(w_in, 0)
    d_model = w_out.shape[2]
    for l in range(DEPTH):
        proj, (w_out_bf16, w_gate_bf16, w_up_bf16) = matmul(
            h, w_in_bf16, BF16,
            [(w_out, l, w_out.shape[1], d_model), (w_gate, l, d_model, D_FF_PAD), (w_up, l, d_model, D_FF_PAD)])
        qt, k_rot, vt = qkv_prep(proj, cos, sin, q_norm[l], k_norm[l], key_block)
        a_out = None
        for row0, batch, seq in attn_calls:
            a_out = attention(qt, k_rot, vt, a_out, row0, batch, seq)
        cp_out = conv_pool(proj, conv_w[l], pool_w[l].astype(BF16), pool_scale[l], segments)
        mix = matmul_two_lhs(a_out, cp_out, w_out_bf16)
        if x is None:
            x, h = resnorm_from_parts(mix, x_parts, ln_mix_post[l], ln_ffn_pre[l], total)
        else:
            x, h = resnorm(mix, x, ln_mix_post[l], ln_ffn_pre[l])
        jobs = [(w_down, l, D_FF_PAD, d_model)]
        if l + 1 < DEPTH:
            jobs.append((w_in, l + 1, d_model, w_in.shape[2]))
        act, cast = gate_up(h, w_gate_bf16, w_up_bf16, jobs)
        w_down_bf16 = cast[0]
        w_in_bf16 = cast[1] if l + 1 < DEPTH else None
        f = matmul_split_k(act, w_down_bf16, bk=D_FF_PAD // 4)
        if l + 1 < DEPTH:
            x, h = resnorm(f, x, ln_ffn_post[l], ln_mix_pre[l + 1])
    return resnorm_last_to_parts(f, x, ln_ffn_post[DEPTH - 1], [(row0, n) for _, row0, n in x_parts])


def kernel(x_prompt, x_sample, ln_mix_pre, ln_mix_post, ln_ffn_pre, ln_ffn_post, q_norm, k_norm,
           w_in, conv_w, pool_w, pool_scale, w_out, w_gate, w_up, w_down):
    pb, ps, d = x_prompt.shape
    sb, ss, _ = x_sample.shape
    x_parts = [(x_prompt.reshape(pb * ps, d), 0, pb * ps), (x_sample.reshape(sb * ss, d), pb * ps, sb * ss)]
    segments = tuple((b * ps, ps) for b in range(pb)) + tuple((pb * ps + b * ss, ss) for b in range(sb))
    attn_calls = ((0, pb, ps), (pb * ps, sb, ss))
    y_prompt, y_sample = _trunk(x_parts, segments, attn_calls, ln_mix_pre, ln_mix_post, ln_ffn_pre, ln_ffn_post,
                                q_norm, k_norm, w_in, conv_w, pool_w, pool_scale, w_out, w_gate, w_up, w_down)
    return (y_prompt.reshape(pb, ps, d), y_sample.reshape(sb, ss, d))
```

```python
import functools

import numpy as np
import jax
import jax.numpy as jnp
from jax import lax
from jax.experimental import pallas as pl
from jax.experimental.pallas import tpu as pltpu

D_MODEL = 4096
DEPTH = 2
GRID_W = 64
HEAD_DIM = 128
N_Q_HEADS = 16
N_KV_HEADS = 4
Q_PER_KV = N_Q_HEADS // N_KV_HEADS
ATTN_WIDTH = N_Q_HEADS * HEAD_DIM
KV_WIDTH = N_KV_HEADS * HEAD_DIM
CONV_WIDTH = D_MODEL // 4
POOL_WIDTH = D_MODEL // 4
POOL_WINDOWS = (2, 4, 8, 16)
POOL_GROUP = POOL_WIDTH // len(POOL_WINDOWS)
IN_WIDTH = ATTN_WIDTH + 2 * KV_WIDTH + 3 * CONV_WIDTH + POOL_WIDTH
D_FF = 11008
D_FF_PAD = 11264
ROPE_HALF = HEAD_DIM // 2
ROPE_THETA = 10000.0
EPS = 1e-6

Q_OFF = 0
K_OFF = ATTN_WIDTH
V_OFF = K_OFF + KV_WIDTH
CH_OFF = V_OFF + KV_WIDTH
CB_OFF = CH_OFF + CONV_WIDTH
CC_OFF = CB_OFF + CONV_WIDTH
PU_OFF = CC_OFF + CONV_WIDTH

HALO = 16
V7X_VMEM_BYTES = 64 * 1024 * 1024

F32 = jnp.float32
BF16 = jnp.bfloat16


def _params(semantics, vmem_mb):
    assert vmem_mb * 1024 * 1024 < V7X_VMEM_BYTES
    return pltpu.CompilerParams(dimension_semantics=semantics,
                                vmem_limit_bytes=vmem_mb * 1024 * 1024)


def _rms(x):
    return x * lax.rsqrt(jnp.mean(x * x, axis=-1, keepdims=True) + EPS)


def _rows_spec(rows, d, row0):
    assert row0 % rows == 0
    return pl.BlockSpec((rows, d), lambda i: (row0 // rows + i, 0))


def _rowwise_call(body, name, n_rows, ins, gains, outs, prev_outs, rows=256):
    d = ins[0][0].shape[1]
    in_specs = [_rows_spec(rows, d, r0) for _, r0 in ins]
    in_specs += [pl.BlockSpec((1, d), lambda i: (0, 0)) for _ in gains]
    args = [a for a, _ in ins] + [g.reshape(1, d) for g in gains]
    aliases = {}
    kernel = body
    if prev_outs is not None:
        n_in = len(args)
        in_specs += [pl.BlockSpec(memory_space=pl.ANY) for _ in prev_outs]
        args += list(prev_outs)
        aliases = {n_in + k: k for k in range(len(prev_outs))}

        def kernel(*refs):
            body(*refs[:n_in], *refs[n_in + len(prev_outs):])

    return pl.pallas_call(
        kernel,
        grid=(n_rows // rows,),
        in_specs=in_specs,
        out_specs=[_rows_spec(rows, d, r0) for _, r0, _ in outs],
        out_shape=[jax.ShapeDtypeStruct((total, d), dt) for total, _, dt in outs],
        input_output_aliases=aliases,
        compiler_params=_params(("parallel",), 40),
        name=name,
    )(*args)


def _norm_cast_kernel(x_ref, g_ref, h_ref):
    h_ref[...] = (_rms(x_ref[...]) * g_ref[...]).astype(h_ref.dtype)


def _resnorm_kernel(f_ref, x_ref, g_post_ref, g_next_ref, xo_ref, h_ref):
    xn = x_ref[...] + _rms(f_ref[...].astype(F32)) * g_post_ref[...]
    xo_ref[...] = xn
    h_ref[...] = (_rms(xn) * g_next_ref[...]).astype(h_ref.dtype)


def _resnorm_last_kernel(f_ref, x_ref, g_post_ref, xo_ref):
    xo_ref[...] = x_ref[...] + _rms(f_ref[...].astype(F32)) * g_post_ref[...]


def norm_cast_parts(parts, g, total):
    h = None
    for x, row0, n in parts:
        h, = _rowwise_call(_norm_cast_kernel, "norm_cast", n, [(x, 0)], [g], [(total, row0, BF16)],
                           None if h is None else [h])
    return h


def resnorm_from_parts(f, parts, g_post, g_next, total):
    outs = None
    for x, row0, n in parts:
        outs = _rowwise_call(_resnorm_kernel, "resnorm", n, [(f, row0), (x, 0)], [g_post, g_next],
                             [(total, row0, F32), (total, row0, BF16)], outs)
    return outs


def resnorm(f, x, g_post, g_next):
    total = x.shape[0]
    return _rowwise_call(_resnorm_kernel, "resnorm", total, [(f, 0), (x, 0)], [g_post, g_next],
                         [(total, 0, F32), (total, 0, BF16)], None)


def resnorm_last_to_parts(f, x, g_post, parts):
    return [_rowwise_call(_resnorm_last_kernel, "resnorm_last", n, [(f, row0), (x, row0)], [g_post],
                          [(n, 0, F32)], None)[0] for row0, n in parts]


CAST_JOB_ROW_CHOICES = (32, 64, 128, 256, 512, 1024)


class CastJob:
    def __init__(self, w, layer, out_rows, out_cols, n_steps):
        _, n_rows, n_cols = w.shape
        fits = [r for r in CAST_JOB_ROW_CHOICES
                if n_rows % r == 0 and out_rows % r == 0 and out_rows // r <= n_steps]
        self.feasible = bool(fits)
        if not fits:
            return
        self.w, self.layer, self.rows = w, layer, fits[0]
        self.n_cols, self.out_rows, self.out_cols = n_cols, out_rows, out_cols
        self.in_blocks, self.out_blocks = n_rows // self.rows, out_rows // self.rows

    def specs(self, step_of):
        in_spec = pl.BlockSpec((None, self.rows, self.n_cols),
                               lambda *g: (self.layer, jnp.minimum(step_of(*g), self.in_blocks - 1), 0))
        out_spec = pl.BlockSpec((self.rows, self.out_cols),
                                lambda *g: (jnp.minimum(step_of(*g), self.out_blocks - 1), 0))
        return in_spec, out_spec, jax.ShapeDtypeStruct((self.out_rows, self.out_cols), BF16)

    def run(self, step, w_ref, o_ref):
        @pl.when(step < self.in_blocks)
        def _():
            o_ref[:, :self.n_cols] = w_ref[...].astype(o_ref.dtype)
            if self.out_cols > self.n_cols:
                o_ref[:, self.n_cols:] = jnp.zeros((self.rows, self.out_cols - self.n_cols), o_ref.dtype)

        if self.out_blocks > self.in_blocks:
            @pl.when((step >= self.in_blocks) & (step < self.out_blocks))
            def _():
                o_ref[...] = jnp.zeros(o_ref.shape, o_ref.dtype)


def _cast_weight_kernel(w_ref, o_ref, *, valid_rows, valid_cols):
    br, bc = o_ref.shape
    r = pl.program_id(0) * br + lax.broadcasted_iota(jnp.int32, (br, bc), 0)
    c = pl.program_id(1) * bc + lax.broadcasted_iota(jnp.int32, (br, bc), 1)
    o_ref[...] = jnp.where((r < valid_rows) & (c < valid_cols), w_ref[...], 0.0).astype(o_ref.dtype)


def cast_weight(w, layer, out_rows=None, out_cols=None, br=512, bc=1024):
    _, rows, cols = w.shape
    out_rows = out_rows or rows
    out_cols = out_cols or cols
    return pl.pallas_call(
        functools.partial(_cast_weight_kernel, valid_rows=rows, valid_cols=cols),
        grid=(out_rows // br, out_cols // bc),
        in_specs=[pl.BlockSpec((None, br, bc), lambda i, j: (layer, i, j))],
        out_specs=pl.BlockSpec((br, bc), lambda i, j: (i, j)),
        out_shape=jax.ShapeDtypeStruct((out_rows, out_cols), BF16),
        compiler_params=_params(("parallel", "parallel"), 32),
        name="cast_weight",
    )(w)


def _with_cast_jobs(body, n_in, n_out, jobs, n_j):
    n_jobs = len(jobs)

    def kernel(*refs):
        ins, job_ins = refs[:n_in], refs[n_in:n_in + n_jobs]
        outs = refs[n_in + n_jobs:n_in + n_jobs + n_out]
        job_outs = refs[n_in + n_jobs + n_out:n_in + n_jobs + n_out + n_jobs]
        body(*ins, *outs, *refs[n_in + n_jobs + n_out + n_jobs:])
        step = pl.program_id(0) * n_j + pl.program_id(1)
        for job, w_ref, o_ref in zip(jobs, job_ins, job_outs):
            job.run(step, w_ref, o_ref)

    return kernel


def _matmul_call(body, name, ins, in_specs, out_spec, out_shape, grid, job_requests, vmem_mb):
    n_i, n_j = grid
    jobs = [CastJob(w, layer, r, c, n_i * n_j) for w, layer, r, c in job_requests]
    riding = [job for job in jobs if job.feasible]
    job_specs = [job.specs(lambda i, j: i * n_j + j) for job in riding]
    outs = pl.pallas_call(
        _with_cast_jobs(body, len(ins), 1, riding, n_j),
        grid=grid,
        in_specs=list(in_specs) + [s[0] for s in job_specs],
        out_specs=[out_spec] + [s[1] for s in job_specs],
        out_shape=[out_shape] + [s[2] for s in job_specs],
        compiler_params=_params(("arbitrary", "arbitrary"), vmem_mb),
        name=name,
    )(*ins, *[job.w for job in riding])
    cast = iter(outs[1:])
    weights = [next(cast) if job.feasible else cast_weight(w, layer, r, c)
               for job, (w, layer, r, c) in zip(jobs, job_requests)]
    return outs[0], weights


def _mm_kernel(a_ref, b_ref, o_ref):
    o_ref[...] = jnp.dot(a_ref[...], b_ref[...], preferred_element_type=F32).astype(o_ref.dtype)


def matmul(a, b, out_dtype, job_requests=(), bm=1024, bn=1024):
    m, k = a.shape
    _, n = b.shape
    bm = min(bm, m)
    return _matmul_call(
        _mm_kernel, "matmul_full_k", [a, b],
        [pl.BlockSpec((bm, k), lambda i, j: (i, 0)), pl.BlockSpec((k, bn), lambda i, j: (0, j))],
        pl.BlockSpec((bm, bn), lambda i, j: (i, j)), jax.ShapeDtypeStruct((m, n), out_dtype),
        (m // bm, n // bn), job_requests, 56)


def _mm2_kernel(a1_ref, a2_ref, b1_ref, b2_ref, o_ref):
    o_ref[...] = (jnp.dot(a1_ref[...], b1_ref[...], preferred_element_type=F32)
                  + jnp.dot(a2_ref[...], b2_ref[...], preferred_element_type=F32)).astype(o_ref.dtype)


def matmul_two_lhs(a1, a2, b, bm=1024, bn=1024):
    m, k1 = a1.shape
    _, k2 = a2.shape
    assert k1 == k2 and b.shape[0] == k1 + k2
    n = b.shape[1]
    bm = min(bm, m)
    return pl.pallas_call(
        _mm2_kernel,
        grid=(m // bm, n // bn),
        in_specs=[pl.BlockSpec((bm, k1), lambda i, j: (i, 0)),
                  pl.BlockSpec((bm, k2), lambda i, j: (i, 0)),
                  pl.BlockSpec((k1, bn), lambda i, j: (0, j)),
                  pl.BlockSpec((k2, bn), lambda i, j: (1, j))],
        out_specs=pl.BlockSpec((bm, bn), lambda i, j: (i, j)),
        out_shape=jax.ShapeDtypeStruct((m, n), BF16),
        compiler_params=_params(("parallel", "parallel"), 48),
        name="matmul_two_lhs",
    )(a1, a2, b, b)


def _mm_acc_kernel(a_ref, b_ref, o_ref, acc_ref, *, nk):
    k = pl.program_id(2)

    @pl.when(k == 0)
    def _():
        acc_ref[...] = jnp.dot(a_ref[...], b_ref[...], preferred_element_type=F32)

    @pl.when((k > 0) & (k < nk - 1))
    def _():
        acc_ref[...] += jnp.dot(a_ref[...], b_ref[...], preferred_element_type=F32)

    @pl.when(k == nk - 1)
    def _():
        o_ref[...] = (acc_ref[...] + jnp.dot(a_ref[...], b_ref[...], preferred_element_type=F32)
                      ).astype(o_ref.dtype)


def matmul_split_k(a, b, bk, bm=1024, bn=1024):
    m, k = a.shape
    _, n = b.shape
    bm = min(bm, m)
    nk = k // bk
    assert nk >= 2
    return pl.pallas_call(
        functools.partial(_mm_acc_kernel, nk=nk),
        grid=(m // bm, n // bn, nk),
        in_specs=[pl.BlockSpec((bm, bk), lambda i, j, kk: (i, kk)),
                  pl.BlockSpec((bk, bn), lambda i, j, kk: (kk, j))],
        out_specs=pl.BlockSpec((bm, bn), lambda i, j, kk: (i, j)),
        out_shape=jax.ShapeDtypeStruct((m, n), BF16),
        scratch_shapes=[pltpu.VMEM((bm, bn), F32)],
        compiler_params=_params(("parallel", "parallel", "arbitrary"), 48),
        name="matmul_split_k",
    )(a, b)


def _gateup_kernel(h_ref, wg_ref, wu_ref, o_ref):
    h = h_ref[...]
    g = jnp.dot(h, wg_ref[...], preferred_element_type=F32)
    u = jnp.dot(h, wu_ref[...], preferred_element_type=F32)
    o_ref[...] = (g * jax.nn.sigmoid(g) * u).astype(o_ref.dtype)


def gate_up(h, wg, wu, job_requests=(), bm=1024, bn=512):
    m, k = h.shape
    _, n = wg.shape
    bm = min(bm, m)
    w_spec = pl.BlockSpec((k, bn), lambda i, j: (0, j))
    return _matmul_call(
        _gateup_kernel, "gate_up", [h, wg, wu],
        [pl.BlockSpec((bm, k), lambda i, j: (i, 0)), w_spec, w_spec],
        pl.BlockSpec((bm, bn), lambda i, j: (i, j)), jax.ShapeDtypeStruct((m, n), BF16),
        (m // bm, n // bn), job_requests, 56)


ATTN_KEY_BLOCK = 1024
ATTN_KEYS_PER_LOOP_TRIP = 2048
V_ROWS = HEAD_DIM + 16
LOG2E = 1.4426950408889634


def _rope_tables(max_len):
    pos = np.arange(max_len)
    r = (pos // GRID_W).astype(np.float32)
    c = (pos % GRID_W).astype(np.float32)
    inv = (1.0 / (ROPE_THETA ** (np.arange(0, ROPE_HALF, 2, dtype=np.float32) / ROPE_HALF))).astype(np.float32)
    ang_r = r[:, None] * inv
    ang_c = c[:, None] * inv
    cos = np.concatenate([np.cos(ang_r), np.cos(ang_r), np.cos(ang_c), np.cos(ang_c)], axis=-1)
    sin = np.concatenate([-np.sin(ang_r), np.sin(ang_r), -np.sin(ang_c), np.sin(ang_c)], axis=-1)
    return jnp.asarray(cos, F32), jnp.asarray(sin, F32)


def _norm_rope(x, gain, cos, sin, upper):
    xn = _rms(x) * gain
    quarter = ROPE_HALF // 2
    partner = jnp.where(upper, pltpu.roll(xn, quarter, axis=1), pltpu.roll(xn, HEAD_DIM - quarter, axis=1))
    return xn * cos + partner * sin


def _qkv_prep_kernel(q_ref, k_ref, v_ref, cos_ref, sin_ref, qg_ref, kg_ref, qt_ref, ko_ref, vt_ref):
    cos = cos_ref[...]
    sin = sin_ref[...]
    rows = cos.shape[0]
    lane = lax.broadcasted_iota(jnp.int32, cos.shape, 1)
    upper = (lane & (ROPE_HALF // 2)) != 0
    q_scale = HEAD_DIM ** -0.5 * LOG2E
    qg = qg_ref[...]
    kg = kg_ref[...]
    for h in range(N_Q_HEADS):
        sl = slice(h * HEAD_DIM, (h + 1) * HEAD_DIM)
        y = _norm_rope(q_ref[:, sl].astype(F32), qg, cos, sin, upper) * q_scale
        qt_ref[sl, :] = y.T.astype(qt_ref.dtype)
    for h in range(N_KV_HEADS):
        sl = slice(h * HEAD_DIM, (h + 1) * HEAD_DIM)
        y = _norm_rope(k_ref[:, sl].astype(F32), kg, cos, sin, upper)
        ko_ref[:, sl] = y.astype(ko_ref.dtype)
        vt_ref[h, 0, :HEAD_DIM, :] = v_ref[:, sl].astype(F32).T.astype(vt_ref.dtype)
        vt_ref[h, 0, HEAD_DIM:, :] = jnp.ones((V_ROWS - HEAD_DIM, rows), vt_ref.dtype)


def qkv_prep(proj, cos, sin, q_gain, k_gain, key_block, segments, rows=512):
    t = proj.shape[0]
    rows = min(rows, key_block)
    per_key_block = key_block // rows
    assert all(s0 % rows == 0 for s0, _ in segments)

    def position_block(i):
        block = i
        for s0, _ in segments:
            block = jnp.where(i * rows >= s0, i - s0 // rows, block)
        return block

    return pl.pallas_call(
        _qkv_prep_kernel,
        grid=(t // rows,),
        in_specs=[pl.BlockSpec((rows, ATTN_WIDTH), lambda i: (i, Q_OFF // ATTN_WIDTH)),
                  pl.BlockSpec((rows, KV_WIDTH), lambda i: (i, K_OFF // KV_WIDTH)),
                  pl.BlockSpec((rows, KV_WIDTH), lambda i: (i, V_OFF // KV_WIDTH)),
                  pl.BlockSpec((rows, HEAD_DIM), lambda i: (position_block(i), 0)),
                  pl.BlockSpec((rows, HEAD_DIM), lambda i: (position_block(i), 0)),
                  pl.BlockSpec((1, HEAD_DIM), lambda i: (0, 0)),
                  pl.BlockSpec((1, HEAD_DIM), lambda i: (0, 0))],
        out_specs=[pl.BlockSpec((ATTN_WIDTH, rows), lambda i: (0, i)),
                   pl.BlockSpec((rows, KV_WIDTH), lambda i: (i, 0)),
                   pl.BlockSpec((N_KV_HEADS, 1, V_ROWS, rows),
                                lambda i: (0, i // per_key_block, 0, i % per_key_block))],
        out_shape=[jax.ShapeDtypeStruct((ATTN_WIDTH, t), BF16),
                   jax.ShapeDtypeStruct((t, KV_WIDTH), BF16),
                   jax.ShapeDtypeStruct((N_KV_HEADS, t // key_block, V_ROWS, key_block), BF16)],
        compiler_params=_params(("parallel",), 40),
        name="qkv_prep",
    )(proj, proj, proj, cos, sin, q_gain.reshape(1, HEAD_DIM), k_gain.reshape(1, HEAD_DIM))


def _attn_kernel(qt_ref, k_ref, vt_ref, *rest, bk, nk):
    o_ref, m_scr, acc_scr, s_scr, smax_scr = rest[-5:]
    m_scr[...] = jnp.full(m_scr.shape, -jnp.inf, F32)
    acc_scr[...] = jnp.zeros(acc_scr.shape, F32)

    def scores(kb, h, slot):
        s = jnp.dot(kb, qt_ref[h * HEAD_DIM:(h + 1) * HEAD_DIM, :], preferred_element_type=F32)
        s_scr[slot] = s
        smax_scr[slot] = jnp.max(s, axis=0, keepdims=True)

    def key_block(j):
        return k_ref[pl.ds(pl.multiple_of(j * bk, bk), bk), :]

    scores(key_block(0), 0, 0)

    def body(j, carry):
        kb = key_block(j)
        kb_next = key_block(jnp.minimum(j + 1, nk - 1))
        vb = vt_ref[j]
        for h in range(Q_PER_KV):
            slot = h % 2
            if h + 1 < Q_PER_KV:
                scores(kb, h + 1, 1 - slot)
            else:
                scores(kb_next, 0, 1 - slot)
            m_prev = m_scr[h]
            m_new = jnp.maximum(m_prev, smax_scr[slot])
            alpha = jnp.exp2(m_prev - m_new)
            p = jnp.exp2(s_scr[slot] - m_new).astype(vb.dtype)
            acc_scr[h] = acc_scr[h] * alpha + jnp.dot(vb, p, preferred_element_type=F32)
            m_scr[h] = m_new
        return carry

    lax.fori_loop(0, nk, body, 0, unroll=max(1, ATTN_KEYS_PER_LOOP_TRIP // bk))
    for h in range(Q_PER_KV):
        acc = acc_scr[h]
        o = acc[:HEAD_DIM, :] / acc[HEAD_DIM:HEAD_DIM + 1, :]
        o_ref[:, h * HEAD_DIM:(h + 1) * HEAD_DIM] = o.T.astype(o_ref.dtype)


def attention(qt, k_rot, vt, prev_out, row0, batch, seq, bq=512):
    t = k_rot.shape[0]
    bk = vt.shape[-1]
    bq = min(bq, seq)
    assert row0 % seq == 0 and seq % bq == 0 and seq % bk == 0
    group_w = Q_PER_KV * HEAD_DIM
    nq = seq // bq
    nk = seq // bk
    in_specs = [
        pl.BlockSpec((group_w, bq), lambda b, h, i: (h, row0 // bq + b * nq + i)),
        pl.BlockSpec((seq, HEAD_DIM), lambda b, h, i: (row0 // seq + b, h)),
        pl.BlockSpec((None, nk, V_ROWS, bk), lambda b, h, i: (h, row0 // seq + b, 0, 0)),
    ]
    args = [qt, k_rot, vt]
    aliases = {}
    if prev_out is not None:
        in_specs.append(pl.BlockSpec(memory_space=pl.ANY))
        args.append(prev_out)
        aliases = {3: 0}
    return pl.pallas_call(
        functools.partial(_attn_kernel, bk=bk, nk=nk),
        grid=(batch, N_KV_HEADS, nq),
        in_specs=in_specs,
        out_specs=pl.BlockSpec((bq, group_w), lambda b, h, i: (row0 // bq + b * nq + i, h)),
        out_shape=jax.ShapeDtypeStruct((t, ATTN_WIDTH), BF16),
        scratch_shapes=[pltpu.VMEM((Q_PER_KV, 1, bq), F32),
                        pltpu.VMEM((Q_PER_KV, V_ROWS, bq), F32),
                        pltpu.VMEM((2, bk, bq), F32),
                        pltpu.VMEM((2, 1, bq), F32)],
        input_output_aliases=aliases,
        compiler_params=_params(("parallel", "parallel", "parallel"), 48),
        name="attention",
    )(*args)


def _convpool_kernel(ch_ref, cb_ref, cc_ref, pu_ref,
                     chp_ref, ccp_ref, pup_ref, chn_ref, ccn_ref, pun_ref,
                     cw_ref, pw_ref, ps_ref, o_ref, *, rows, segments):
    i = pl.program_id(0)
    start = i * rows
    pos0 = start
    seq_len = jnp.int32(0)
    has_prev = jnp.bool_(True)
    has_next = jnp.bool_(True)
    for seg_start, seg_len in segments:
        inside = (start >= seg_start) & (start < seg_start + seg_len)
        pos0 = jnp.where(inside, start - seg_start, pos0)
        seq_len = jnp.where(inside, seg_len, seq_len)
        has_prev = has_prev & (start != seg_start)
        has_next = has_next & (start + rows != seg_start + seg_len)
    prev_on = has_prev.astype(F32)
    next_on = has_next.astype(F32)

    def extended(cur_ref, prev_ref, next_ref):
        return jnp.concatenate([prev_ref[...].astype(F32) * prev_on,
                                cur_ref[...].astype(F32),
                                next_ref[...].astype(F32) * next_on], axis=0)

    ext_rows = rows + 2 * HALO

    def shifted(x, d):
        return pltpu.roll(x, d % ext_rows, axis=0)

    u = extended(cc_ref, ccp_ref, ccn_ref) * extended(ch_ref, chp_ref, chn_ref)
    cw = cw_ref[...]
    y = shifted(u, 1) * cw[0:1, :] + u * cw[1:2, :] + shifted(u, -1) * cw[2:3, :]
    o_ref[:, :CONV_WIDTH] = (cb_ref[...].astype(F32) * y[HALO:HALO + rows, :]).astype(o_ref.dtype)

    x = extended(pu_ref, pup_ref, pun_ref)
    pos = (pos0 + lax.broadcasted_iota(jnp.int32, (rows, 1), 0))
    for gi, w in enumerate(POOL_WINDOWS):
        sl = slice(gi * POOL_GROUP, (gi + 1) * POOL_GROUP)
        xg = x[:, sl]
        tot = xg + shifted(xg, 1)
        half = 1
        while 2 * half < w:
            tot = shifted(tot, half) + shifted(tot, -half)
            half *= 2
        hi = jnp.minimum(pos + w // 2, seq_len)
        lo = jnp.maximum(pos - w // 2, 0)
        cnt = (hi - lo).astype(F32)
        m = tot[HALO:HALO + rows, :] / cnt - xg[HALO:HALO + rows, :]
        yg = jnp.dot(m.astype(BF16), pw_ref[gi], preferred_element_type=F32)
        o_ref[:, CONV_WIDTH + gi * POOL_GROUP:CONV_WIDTH + (gi + 1) * POOL_GROUP] = (
            yg * ps_ref[:, sl]).astype(o_ref.dtype)


def conv_pool(proj, conv_w, pool_w, pool_scale, segments, rows=512):
    t = proj.shape[0]
    rows = min(rows, min(n for _, n in segments))
    assert all(s % rows == 0 and n % rows == 0 for s, n in segments)
    w = CONV_WIDTH
    per = rows // HALO
    last = t // HALO - 1

    def cur(off):
        return pl.BlockSpec((rows, w), lambda i: (i, off // w))

    def prev(off):
        return pl.BlockSpec((HALO, w), lambda i: (jnp.maximum(i * per - 1, 0), off // w))

    def nxt(off):
        return pl.BlockSpec((HALO, w), lambda i: (jnp.minimum((i + 1) * per, last), off // w))

    kernel = functools.partial(_convpool_kernel, rows=rows, segments=segments)
    return pl.pallas_call(
        kernel,
        grid=(t // rows,),
        in_specs=[cur(CH_OFF), cur(CB_OFF), cur(CC_OFF), cur(PU_OFF),
                  prev(CH_OFF), prev(CC_OFF), prev(PU_OFF),
                  nxt(CH_OFF), nxt(CC_OFF), nxt(PU_OFF),
                  pl.BlockSpec((3, w), lambda i: (0, 0)),
                  pl.BlockSpec((len(POOL_WINDOWS), POOL_GROUP, POOL_GROUP), lambda i: (0, 0, 0)),
                  pl.BlockSpec((1, POOL_WIDTH), lambda i: (0, 0))],
        out_specs=pl.BlockSpec((rows, CONV_WIDTH + POOL_WIDTH), lambda i: (i, 0)),
        out_shape=jax.ShapeDtypeStruct((t, CONV_WIDTH + POOL_WIDTH), BF16),
        compiler_params=_params(("parallel",), 48),
        name="conv_pool",
    )(proj, proj, proj, proj, proj, proj, proj, proj, proj, proj,
      conv_w, pool_w, pool_scale.reshape(1, POOL_WIDTH))


def _trunk(x_parts, segments, attn_calls, ln_mix_pre, ln_mix_post, ln_ffn_pre, ln_ffn_post, q_norm, k_norm,
           w_in, conv_w, pool_w, pool_scale, w_out, w_gate, w_up, w_down):
    total = sum(n for _, _, n in x_parts)
    cos, sin = _rope_tables(max(n for _, n in segments))
    key_block = min(ATTN_KEY_BLOCK, min(n for _, n in segments))
    h = norm_cast_parts(x_parts, ln_mix_pre[0], total)
    x = None
    w_in_bf16 = cast_weight(w_in, 0)
    d_model = w_out.shape[2]
    for l in range(DEPTH):
        proj, (w_out_bf16, w_gate_bf16, w_up_bf16) = matmul(
            h, w_in_bf16, BF16,
            [(w_out, l, w_out.shape[1], d_model), (w_gate, l, d_model, D_FF_PAD), (w_up, l, d_model, D_FF_PAD)])
        qt, k_rot, vt = qkv_prep(proj, cos, sin, q_norm[l], k_norm[l], key_block, segments)
        a_out = None
        for row0, batch, seq in attn_calls:
            a_out = attention(qt, k_rot, vt, a_out, row0, batch, seq)
        cp_out = conv_pool(proj, conv_w[l], pool_w[l].astype(BF16), pool_scale[l], segments)
        mix = matmul_two_lhs(a_out, cp_out, w_out_bf16)
        if x is None:
            x, h = resnorm_from_parts(mix, x_parts, ln_mix_post[l], ln_ffn_pre[l], total)
        else:
            x, h = resnorm(mix, x, ln_mix_post[l], ln_ffn_pre[l])
        jobs = [(w_down, l, D_FF_PAD, d_model)]
        if l + 1 < DEPTH:
            jobs.append((w_in, l + 1, d_model, w_in.shape[2]))
        act, cast = gate_up(h, w_gate_bf16, w_up_bf16, jobs)
        w_down_bf16 = cast[0]
        w_in_bf16 = cast[1] if l + 1 < DEPTH else None
        f = matmul_split_k(act, w_down_bf16, bk=D_FF_PAD // 4)
        if l + 1 < DEPTH:
            x, h = resnorm(f, x, ln_ffn_post[l], ln_mix_pre[l + 1])
    return resnorm_last_to_parts(f, x, ln_ffn_post[DEPTH - 1], [(row0, n) for _, row0, n in x_parts])


def kernel(x_prompt, x_sample, ln_mix_pre, ln_mix_post, ln_ffn_pre, ln_ffn_post, q_norm, k_norm,
           w_in, conv_w, pool_w, pool_scale, w_out, w_gate, w_up, w_down):
    pb, ps, d = x_prompt.shape
    sb, ss, _ = x_sample.shape
    x_parts = [(x_prompt.reshape(pb * ps, d), 0, pb * ps), (x_sample.reshape(sb * ss, d), pb * ps, sb * ss)]
    segments = tuple((b * ps, ps) for b in range(pb)) + tuple((pb * ps + b * ss, ss) for b in range(sb))
    attn_calls = ((0, pb, ps), (pb * ps, sb, ss))
    y_prompt, y_sample = _trunk(x_parts, segments, attn_calls, ln_mix_pre, ln_mix_post, ln_ffn_pre, ln_ffn_post,
                                q_norm, k_norm, w_in, conv_w, pool_w, pool_scale, w_out, w_gate, w_up, w_down)
    return (y_prompt.reshape(pb, ps, d), y_sample.reshape(sb, ss, d))
```

```python
import functools

import numpy as np
import jax
import jax.numpy as jnp
from jax import lax
from jax.experimental import pallas as pl
from jax.experimental.pallas import tpu as pltpu

D_MODEL = 4096
DEPTH = 2
GRID_W = 64
HEAD_DIM = 128
N_Q_HEADS = 16
N_KV_HEADS = 4
Q_PER_KV = N_Q_HEADS // N_KV_HEADS
ATTN_WIDTH = N_Q_HEADS * HEAD_DIM
KV_WIDTH = N_KV_HEADS * HEAD_DIM
CONV_WIDTH = D_MODEL // 4
POOL_WIDTH = D_MODEL // 4
POOL_WINDOWS = (2, 4, 8, 16)
POOL_GROUP = POOL_WIDTH // len(POOL_WINDOWS)
IN_WIDTH = ATTN_WIDTH + 2 * KV_WIDTH + 3 * CONV_WIDTH + POOL_WIDTH
D_FF = 11008
D_FF_PAD = 11264
ROPE_HALF = HEAD_DIM // 2
ROPE_THETA = 10000.0
EPS = 1e-6

Q_OFF = 0
K_OFF = ATTN_WIDTH
V_OFF = K_OFF + KV_WIDTH
CH_OFF = V_OFF + KV_WIDTH
CB_OFF = CH_OFF + CONV_WIDTH
CC_OFF = CB_OFF + CONV_WIDTH
PU_OFF = CC_OFF + CONV_WIDTH

HALO = 16
V7X_VMEM_BYTES = 64 * 1024 * 1024

F32 = jnp.float32
BF16 = jnp.bfloat16


def _params(semantics, vmem_mb):
    assert vmem_mb * 1024 * 1024 < V7X_VMEM_BYTES
    return pltpu.CompilerParams(dimension_semantics=semantics,
                                vmem_limit_bytes=vmem_mb * 1024 * 1024)


def _rms(x):
    return x * lax.rsqrt(jnp.mean(x * x, axis=-1, keepdims=True) + EPS)


def _part_specs(parts, block_rows, width, block_of):
    specs = []
    for _, row0, n in parts:
        assert row0 % block_rows == 0 and n % block_rows == 0
        first, count = row0 // block_rows, n // block_rows
        specs.append(pl.BlockSpec(
            (block_rows, width), lambda *g, first=first, count=count: (jnp.clip(block_of(*g) - first, 0, count - 1), 0)))
    return specs


def _part_select(parts, block_rows, block, refs):
    value = refs[0][...]
    for (_, row0, _), ref in zip(parts[1:], refs[1:]):
        value = jnp.where(block >= row0 // block_rows, ref[...], value)
    return value


def _rowwise_call(body, name, slab_row0, n_rows, sources, gains, out_dtypes, rows=256):
    d = sources[0][0][0].shape[1]
    first_block = slab_row0 // rows
    assert slab_row0 % rows == 0 and n_rows % rows == 0
    in_specs, args = [], []
    for parts in sources:
        in_specs += _part_specs(parts, rows, d, lambda i: first_block + i)
        args += [a for a, _, _ in parts]
    in_specs += [pl.BlockSpec((1, d), lambda i: (0, 0)) for _ in gains]
    args += [g.reshape(1, d) for g in gains]

    def kernel(*refs):
        block = first_block + pl.program_id(0)
        values, at = [], 0
        for parts in sources:
            values.append(_part_select(parts, rows, block, refs[at:at + len(parts)]))
            at += len(parts)
        body(*values, *refs[at:])

    return pl.pallas_call(
        kernel,
        grid=(n_rows // rows,),
        in_specs=in_specs,
        out_specs=[pl.BlockSpec((rows, d), lambda i: (i, 0)) for _ in out_dtypes],
        out_shape=[jax.ShapeDtypeStruct((n_rows, d), dt) for dt in out_dtypes],
        compiler_params=_params(("parallel",), 48),
        name=name,
    )(*args)


def _norm_cast_body(x, g_ref, h_ref):
    h_ref[...] = (_rms(x) * g_ref[...]).astype(h_ref.dtype)


def _resnorm_body(f, x, g_post_ref, g_next_ref, xo_ref, h_ref):
    xn = x + _rms(f.astype(F32)) * g_post_ref[...]
    xo_ref[...] = xn
    h_ref[...] = (_rms(xn) * g_next_ref[...]).astype(h_ref.dtype)


def _resnorm_last_body(f, x, g_post_ref, xo_ref):
    xo_ref[...] = x + _rms(f.astype(F32)) * g_post_ref[...]


def norm_cast(x_parts, g, total):
    return _rowwise_call(_norm_cast_body, "norm_cast", 0, total, [x_parts], [g], [BF16])[0]


def resnorm(f, x_parts, g_post, g_next):
    total = f.shape[0]
    return _rowwise_call(_resnorm_body, "resnorm", 0, total, [[(f, 0, total)], x_parts], [g_post, g_next],
                         [F32, BF16])


def resnorm_last_to_parts(f, x, g_post, parts):
    total = f.shape[0]
    return [_rowwise_call(_resnorm_last_body, "resnorm_last", row0, n, [[(f, 0, total)], [(x, 0, total)]],
                          [g_post], [F32])[0] for row0, n in parts]


CAST_JOB_ROW_CHOICES = (32, 64, 128, 256, 512, 1024)


class CastJob:
    def __init__(self, w, layer, out_rows, out_cols, n_steps):
        _, n_rows, n_cols = w.shape
        fits = [r for r in CAST_JOB_ROW_CHOICES
                if n_rows % r == 0 and out_rows % r == 0 and out_rows // r <= n_steps]
        self.feasible = bool(fits)
        if not fits:
            return
        self.w, self.layer, self.rows = w, layer, fits[0]
        self.n_cols, self.out_rows, self.out_cols = n_cols, out_rows, out_cols
        self.in_blocks, self.out_blocks = n_rows // self.rows, out_rows // self.rows

    def specs(self, step_of):
        in_spec = pl.BlockSpec((None, self.rows, self.n_cols),
                               lambda *g: (self.layer, jnp.minimum(step_of(*g), self.in_blocks - 1), 0))
        out_spec = pl.BlockSpec((self.rows, self.out_cols),
                                lambda *g: (jnp.minimum(step_of(*g), self.out_blocks - 1), 0))
        return in_spec, out_spec, jax.ShapeDtypeStruct((self.out_rows, self.out_cols), BF16)

    def run(self, step, w_ref, o_ref):
        @pl.when(step < self.in_blocks)
        def _():
            o_ref[:, :self.n_cols] = w_ref[...].astype(o_ref.dtype)
            if self.out_cols > self.n_cols:
                o_ref[:, self.n_cols:] = jnp.zeros((self.rows, self.out_cols - self.n_cols), o_ref.dtype)

        if self.out_blocks > self.in_blocks:
            @pl.when((step >= self.in_blocks) & (step < self.out_blocks))
            def _():
                o_ref[...] = jnp.zeros(o_ref.shape, o_ref.dtype)


def _cast_weight_kernel(w_ref, o_ref, *, valid_rows, valid_cols):
    br, bc = o_ref.shape
    r = pl.program_id(0) * br + lax.broadcasted_iota(jnp.int32, (br, bc), 0)
    c = pl.program_id(1) * bc + lax.broadcasted_iota(jnp.int32, (br, bc), 1)
    o_ref[...] = jnp.where((r < valid_rows) & (c < valid_cols), w_ref[...], 0.0).astype(o_ref.dtype)


def cast_weight(w, layer, out_rows=None, out_cols=None, br=512, bc=1024):
    _, rows, cols = w.shape
    out_rows = out_rows or rows
    out_cols = out_cols or cols
    return pl.pallas_call(
        functools.partial(_cast_weight_kernel, valid_rows=rows, valid_cols=cols),
        grid=(out_rows // br, out_cols // bc),
        in_specs=[pl.BlockSpec((None, br, bc), lambda i, j: (layer, i, j))],
        out_specs=pl.BlockSpec((br, bc), lambda i, j: (i, j)),
        out_shape=jax.ShapeDtypeStruct((out_rows, out_cols), BF16),
        compiler_params=_params(("parallel", "parallel"), 32),
        name="cast_weight",
    )(w)


def _with_cast_jobs(body, n_in, n_out, jobs, n_j):
    n_jobs = len(jobs)

    def kernel(*refs):
        ins, job_ins = refs[:n_in], refs[n_in:n_in + n_jobs]
        outs = refs[n_in + n_jobs:n_in + n_jobs + n_out]
        job_outs = refs[n_in + n_jobs + n_out:n_in + n_jobs + n_out + n_jobs]
        body(*ins, *outs, *refs[n_in + n_jobs + n_out + n_jobs:])
        step = pl.program_id(0) * n_j + pl.program_id(1)
        for job, w_ref, o_ref in zip(jobs, job_ins, job_outs):
            job.run(step, w_ref, o_ref)

    return kernel


def _matmul_call(body, name, ins, in_specs, out_spec, out_shape, grid, job_requests, vmem_mb):
    n_i, n_j = grid
    jobs = [CastJob(w, layer, r, c, n_i * n_j) for w, layer, r, c in job_requests]
    riding = [job for job in jobs if job.feasible]
    job_specs = [job.specs(lambda i, j: i * n_j + j) for job in riding]
    outs = pl.pallas_call(
        _with_cast_jobs(body, len(ins), 1, riding, n_j),
        grid=grid,
        in_specs=list(in_specs) + [s[0] for s in job_specs],
        out_specs=[out_spec] + [s[1] for s in job_specs],
        out_shape=[out_shape] + [s[2] for s in job_specs],
        compiler_params=_params(("arbitrary", "arbitrary"), vmem_mb),
        name=name,
    )(*ins, *[job.w for job in riding])
    cast = iter(outs[1:])
    weights = [next(cast) if job.feasible else cast_weight(w, layer, r, c)
               for job, (w, layer, r, c) in zip(jobs, job_requests)]
    return outs[0], weights


def _mm_kernel(a_ref, b_ref, o_ref):
    o_ref[...] = jnp.dot(a_ref[...], b_ref[...], preferred_element_type=F32).astype(o_ref.dtype)


def matmul(a, b, out_dtype, job_requests=(), bm=1024, bn=1024):
    m, k = a.shape
    _, n = b.shape
    bm = min(bm, m)
    return _matmul_call(
        _mm_kernel, "matmul_full_k", [a, b],
        [pl.BlockSpec((bm, k), lambda i, j: (i, 0)), pl.BlockSpec((k, bn), lambda i, j: (0, j))],
        pl.BlockSpec((bm, bn), lambda i, j: (i, j)), jax.ShapeDtypeStruct((m, n), out_dtype),
        (m // bm, n // bn), job_requests, 56)


def matmul_two_lhs(a1_parts, a2, b, bm=1024, bn=1024):
    m, k2 = a2.shape
    k1 = a1_parts[0][0].shape[1]
    assert k1 == k2 and b.shape[0] == k1 + k2
    n = b.shape[1]
    bm = min([bm] + [rows for _, _, rows in a1_parts])
    n_parts = len(a1_parts)
    bounds = [row0 // bm for _, row0, _ in a1_parts] + [m // bm]

    def kernel(*refs):
        a2_ref, b1_ref, b2_ref, o_ref = refs[n_parts:]
        i = pl.program_id(0)
        for p in range(n_parts):
            @pl.when((i >= bounds[p]) & (i < bounds[p + 1]))
            def _(a1_ref=refs[p]):
                o_ref[...] = (jnp.dot(a1_ref[...], b1_ref[...], preferred_element_type=F32)
                              + jnp.dot(a2_ref[...], b2_ref[...], preferred_element_type=F32)).astype(o_ref.dtype)

    return pl.pallas_call(
        kernel,
        grid=(m // bm, n // bn),
        in_specs=_part_specs(a1_parts, bm, k1, lambda i, j: i) + [
            pl.BlockSpec((bm, k2), lambda i, j: (i, 0)),
            pl.BlockSpec((k1, bn), lambda i, j: (0, j)),
            pl.BlockSpec((k2, bn), lambda i, j: (1, j))],
        out_specs=pl.BlockSpec((bm, bn), lambda i, j: (i, j)),
        out_shape=jax.ShapeDtypeStruct((m, n), BF16),
        compiler_params=_params(("parallel", "parallel"), 56),
        name="matmul_two_lhs",
    )(*[a for a, _, _ in a1_parts], a2, b, b)


def _mm_acc_kernel(a_ref, b_ref, o_ref, acc_ref, *, nk):
    k = pl.program_id(2)

    @pl.when(k == 0)
    def _():
        acc_ref[...] = jnp.dot(a_ref[...], b_ref[...], preferred_element_type=F32)

    @pl.when((k > 0) & (k < nk - 1))
    def _():
        acc_ref[...] += jnp.dot(a_ref[...], b_ref[...], preferred_element_type=F32)

    @pl.when(k == nk - 1)
    def _():
        o_ref[...] = (acc_ref[...] + jnp.dot(a_ref[...], b_ref[...], preferred_element_type=F32)
                      ).astype(o_ref.dtype)


def matmul_split_k(a, b, bk, bm=1024, bn=1024):
    m, k = a.shape
    _, n = b.shape
    bm = min(bm, m)
    nk = k // bk
    assert nk >= 2
    return pl.pallas_call(
        functools.partial(_mm_acc_kernel, nk=nk),
        grid=(m // bm, n // bn, nk),
        in_specs=[pl.BlockSpec((bm, bk), lambda i, j, kk: (i, kk)),
                  pl.BlockSpec((bk, bn), lambda i, j, kk: (kk, j))],
        out_specs=pl.BlockSpec((bm, bn), lambda i, j, kk: (i, j)),
        out_shape=jax.ShapeDtypeStruct((m, n), BF16),
        scratch_shapes=[pltpu.VMEM((bm, bn), F32)],
        compiler_params=_params(("parallel", "parallel", "arbitrary"), 48),
        name="matmul_split_k",
    )(a, b)


def _gateup_kernel(h_ref, wg_ref, wu_ref, o_ref):
    h = h_ref[...]
    g = jnp.dot(h, wg_ref[...], preferred_element_type=F32)
    u = jnp.dot(h, wu_ref[...], preferred_element_type=F32)
    o_ref[...] = (g * jax.nn.sigmoid(g) * u).astype(o_ref.dtype)


def gate_up(h, wg, wu, job_requests=(), bm=1024, bn=512):
    m, k = h.shape
    _, n = wg.shape
    bm = min(bm, m)
    w_spec = pl.BlockSpec((k, bn), lambda i, j: (0, j))
    return _matmul_call(
        _gateup_kernel, "gate_up", [h, wg, wu],
        [pl.BlockSpec((bm, k), lambda i, j: (i, 0)), w_spec, w_spec],
        pl.BlockSpec((bm, bn), lambda i, j: (i, j)), jax.ShapeDtypeStruct((m, n), BF16),
        (m // bm, n // bn), job_requests, 56)


ATTN_KEY_BLOCK = 1024
ATTN_KEYS_PER_LOOP_TRIP = 2048
V_ROWS = HEAD_DIM + 16
LOG2E = 1.4426950408889634


def _rope_tables(max_len):
    pos = np.arange(max_len)
    r = (pos // GRID_W).astype(np.float32)
    c = (pos % GRID_W).astype(np.float32)
    inv = (1.0 / (ROPE_THETA ** (np.arange(0, ROPE_HALF, 2, dtype=np.float32) / ROPE_HALF))).astype(np.float32)
    ang_r = r[:, None] * inv
    ang_c = c[:, None] * inv
    cos = np.concatenate([np.cos(ang_r), np.cos(ang_r), np.cos(ang_c), np.cos(ang_c)], axis=-1)
    sin = np.concatenate([-np.sin(ang_r), np.sin(ang_r), -np.sin(ang_c), np.sin(ang_c)], axis=-1)
    return jnp.asarray(cos, F32), jnp.asarray(sin, F32)


def _norm_rope(x, gain, cos, sin, upper):
    xn = _rms(x) * gain
    quarter = ROPE_HALF // 2
    partner = jnp.where(upper, pltpu.roll(xn, quarter, axis=1), pltpu.roll(xn, HEAD_DIM - quarter, axis=1))
    return xn * cos + partner * sin


def _qkv_prep_kernel(q_ref, k_ref, v_ref, cos_ref, sin_ref, qg_ref, kg_ref, qt_ref, ko_ref, vt_ref):
    cos = cos_ref[...]
    sin = sin_ref[...]
    rows = cos.shape[0]
    lane = lax.broadcasted_iota(jnp.int32, cos.shape, 1)
    upper = (lane & (ROPE_HALF // 2)) != 0
    q_scale = HEAD_DIM ** -0.5 * LOG2E
    qg = qg_ref[...]
    kg = kg_ref[...]
    for h in range(N_Q_HEADS):
        sl = slice(h * HEAD_DIM, (h + 1) * HEAD_DIM)
        y = _norm_rope(q_ref[:, sl].astype(F32), qg, cos, sin, upper) * q_scale
        qt_ref[sl, :] = y.T.astype(qt_ref.dtype)
    for h in range(N_KV_HEADS):
        sl = slice(h * HEAD_DIM, (h + 1) * HEAD_DIM)
        y = _norm_rope(k_ref[:, sl].astype(F32), kg, cos, sin, upper)
        ko_ref[:, sl] = y.astype(ko_ref.dtype)
        vt_ref[h, 0, :HEAD_DIM, :] = v_ref[:, sl].astype(F32).T.astype(vt_ref.dtype)
        vt_ref[h, 0, HEAD_DIM:, :] = jnp.ones((V_ROWS - HEAD_DIM, rows), vt_ref.dtype)


def qkv_prep(proj, cos, sin, q_gain, k_gain, key_block, segments, rows=512):
    t = proj.shape[0]
    rows = min(rows, key_block)
    per_key_block = key_block // rows
    assert all(s0 % rows == 0 for s0, _ in segments)

    def position_block(i):
        block = i
        for s0, _ in segments:
            block = jnp.where(i * rows >= s0, i - s0 // rows, block)
        return block

    return pl.pallas_call(
        _qkv_prep_kernel,
        grid=(t // rows,),
        in_specs=[pl.BlockSpec((rows, ATTN_WIDTH), lambda i: (i, Q_OFF // ATTN_WIDTH)),
                  pl.BlockSpec((rows, KV_WIDTH), lambda i: (i, K_OFF // KV_WIDTH)),
                  pl.BlockSpec((rows, KV_WIDTH), lambda i: (i, V_OFF // KV_WIDTH)),
                  pl.BlockSpec((rows, HEAD_DIM), lambda i: (position_block(i), 0)),
                  pl.BlockSpec((rows, HEAD_DIM), lambda i: (position_block(i), 0)),
                  pl.BlockSpec((1, HEAD_DIM), lambda i: (0, 0)),
                  pl.BlockSpec((1, HEAD_DIM), lambda i: (0, 0))],
        out_specs=[pl.BlockSpec((ATTN_WIDTH, rows), lambda i: (0, i)),
                   pl.BlockSpec((rows, KV_WIDTH), lambda i: (i, 0)),
                   pl.BlockSpec((N_KV_HEADS, 1, V_ROWS, rows),
                                lambda i: (0, i // per_key_block, 0, i % per_key_block))],
        out_shape=[jax.ShapeDtypeStruct((ATTN_WIDTH, t), BF16),
                   jax.ShapeDtypeStruct((t, KV_WIDTH), BF16),
                   jax.ShapeDtypeStruct((N_KV_HEADS, t // key_block, V_ROWS, key_block), BF16)],
        compiler_params=_params(("parallel",), 40),
        name="qkv_prep",
    )(proj, proj, proj, cos, sin, q_gain.reshape(1, HEAD_DIM), k_gain.reshape(1, HEAD_DIM))


def _attn_kernel(qt_ref, k_ref, vt_ref, *rest, bk, nk):
    o_ref, m_scr, acc_scr, s_scr, smax_scr = rest[-5:]
    m_scr[...] = jnp.full(m_scr.shape, -jnp.inf, F32)
    acc_scr[...] = jnp.zeros(acc_scr.shape, F32)

    def scores(kb, h, slot):
        s = jnp.dot(kb, qt_ref[h * HEAD_DIM:(h + 1) * HEAD_DIM, :], preferred_element_type=F32)
        s_scr[slot] = s
        smax_scr[slot] = jnp.max(s, axis=0, keepdims=True)

    def key_block(j):
        return k_ref[pl.ds(pl.multiple_of(j * bk, bk), bk), :]

    scores(key_block(0), 0, 0)

    def body(j, carry):
        kb = key_block(j)
        kb_next = key_block(jnp.minimum(j + 1, nk - 1))
        vb = vt_ref[j]
        for h in range(Q_PER_KV):
            slot = h % 2
            if h + 1 < Q_PER_KV:
                scores(kb, h + 1, 1 - slot)
            else:
                scores(kb_next, 0, 1 - slot)
            m_prev = m_scr[h]
            m_new = jnp.maximum(m_prev, smax_scr[slot])
            alpha = jnp.exp2(m_prev - m_new)
            p = jnp.exp2(s_scr[slot] - m_new).astype(vb.dtype)
            acc_scr[h] = acc_scr[h] * alpha + jnp.dot(vb, p, preferred_element_type=F32)
            m_scr[h] = m_new
        return carry

    lax.fori_loop(0, nk, body, 0, unroll=max(1, ATTN_KEYS_PER_LOOP_TRIP // bk))
    for h in range(Q_PER_KV):
        acc = acc_scr[h]
        o = acc[:HEAD_DIM, :] / acc[HEAD_DIM:HEAD_DIM + 1, :]
        o_ref[:, h * HEAD_DIM:(h + 1) * HEAD_DIM] = o.T.astype(o_ref.dtype)


def attention(qt, k_rot, vt, row0, batch, seq, bq=512):
    bk = vt.shape[-1]
    bq = min(bq, seq)
    assert row0 % seq == 0 and seq % bq == 0 and seq % bk == 0
    group_w = Q_PER_KV * HEAD_DIM
    nq = seq // bq
    nk = seq // bk
    in_specs = [
        pl.BlockSpec((group_w, bq), lambda b, h, i: (h, row0 // bq + b * nq + i)),
        pl.BlockSpec((seq, HEAD_DIM), lambda b, h, i: (row0 // seq + b, h)),
        pl.BlockSpec((None, nk, V_ROWS, bk), lambda b, h, i: (h, row0 // seq + b, 0, 0)),
    ]
    return pl.pallas_call(
        functools.partial(_attn_kernel, bk=bk, nk=nk),
        grid=(batch, N_KV_HEADS, nq),
        in_specs=in_specs,
        out_specs=pl.BlockSpec((bq, group_w), lambda b, h, i: (b * nq + i, h)),
        out_shape=jax.ShapeDtypeStruct((batch * seq, ATTN_WIDTH), BF16),
        scratch_shapes=[pltpu.VMEM((Q_PER_KV, 1, bq), F32),
                        pltpu.VMEM((Q_PER_KV, V_ROWS, bq), F32),
                        pltpu.VMEM((2, bk, bq), F32),
                        pltpu.VMEM((2, 1, bq), F32)],
        compiler_params=_params(("parallel", "parallel", "parallel"), 48),
        name="attention",
    )(qt, k_rot, vt)


def _convpool_kernel(ch_ref, cb_ref, cc_ref, pu_ref,
                     chp_ref, ccp_ref, pup_ref, chn_ref, ccn_ref, pun_ref,
                     cw_ref, pw_ref, ps_ref, o_ref, *, rows, segments):
    i = pl.program_id(0)
    start = i * rows
    pos0 = start
    seq_len = jnp.int32(0)
    has_prev = jnp.bool_(True)
    has_next = jnp.bool_(True)
    for seg_start, seg_len in segments:
        inside = (start >= seg_start) & (start < seg_start + seg_len)
        pos0 = jnp.where(inside, start - seg_start, pos0)
        seq_len = jnp.where(inside, seg_len, seq_len)
        has_prev = has_prev & (start != seg_start)
        has_next = has_next & (start + rows != seg_start + seg_len)
    prev_on = has_prev.astype(F32)
    next_on = has_next.astype(F32)

    def extended(cur_ref, prev_ref, next_ref):
        return jnp.concatenate([prev_ref[...].astype(F32) * prev_on,
                                cur_ref[...].astype(F32),
                                next_ref[...].astype(F32) * next_on], axis=0)

    ext_rows = rows + 2 * HALO

    def shifted(x, d):
        return pltpu.roll(x, d % ext_rows, axis=0)

    u = extended(cc_ref, ccp_ref, ccn_ref) * extended(ch_ref, chp_ref, chn_ref)
    cw = cw_ref[...]
    y = shifted(u, 1) * cw[0:1, :] + u * cw[1:2, :] + shifted(u, -1) * cw[2:3, :]
    o_ref[:, :CONV_WIDTH] = (cb_ref[...].astype(F32) * y[HALO:HALO + rows, :]).astype(o_ref.dtype)

    x = extended(pu_ref, pup_ref, pun_ref)
    pos = (pos0 + lax.broadcasted_iota(jnp.int32, (rows, 1), 0))
    for gi, w in enumerate(POOL_WINDOWS):
        sl = slice(gi * POOL_GROUP, (gi + 1) * POOL_GROUP)
        xg = x[:, sl]
        tot = xg + shifted(xg, 1)
        half = 1
        while 2 * half < w:
            tot = shifted(tot, half) + shifted(tot, -half)
            half *= 2
        hi = jnp.minimum(pos + w // 2, seq_len)
        lo = jnp.maximum(pos - w // 2, 0)
        cnt = (hi - lo).astype(F32)
        m = tot[HALO:HALO + rows, :] / cnt - xg[HALO:HALO + rows, :]
        yg = jnp.dot(m.astype(BF16), pw_ref[gi], preferred_element_type=F32)
        o_ref[:, CONV_WIDTH + gi * POOL_GROUP:CONV_WIDTH + (gi + 1) * POOL_GROUP] = (
            yg * ps_ref[:, sl]).astype(o_ref.dtype)


def conv_pool(proj, conv_w, pool_w, pool_scale, segments, rows=512):
    t = proj.shape[0]
    rows = min(rows, min(n for _, n in segments))
    assert all(s % rows == 0 and n % rows == 0 for s, n in segments)
    w = CONV_WIDTH
    per = rows // HALO
    last = t // HALO - 1

    def cur(off):
        return pl.BlockSpec((rows, w), lambda i: (i, off // w))

    def prev(off):
        return pl.BlockSpec((HALO, w), lambda i: (jnp.maximum(i * per - 1, 0), off // w))

    def nxt(off):
        return pl.BlockSpec((HALO, w), lambda i: (jnp.minimum((i + 1) * per, last), off // w))

    kernel = functools.partial(_convpool_kernel, rows=rows, segments=segments)
    return pl.pallas_call(
        kernel,
        grid=(t // rows,),
        in_specs=[cur(CH_OFF), cur(CB_OFF), cur(CC_OFF), cur(PU_OFF),
                  prev(CH_OFF), prev(CC_OFF), prev(PU_OFF),
                  nxt(CH_OFF), nxt(CC_OFF), nxt(PU_OFF),
                  pl.BlockSpec((3, w), lambda i: (0, 0)),
                  pl.BlockSpec((len(POOL_WINDOWS), POOL_GROUP, POOL_GROUP), lambda i: (0, 0, 0)),
                  pl.BlockSpec((1, POOL_WIDTH), lambda i: (0, 0))],
        out_specs=pl.BlockSpec((rows, CONV_WIDTH + POOL_WIDTH), lambda i: (i, 0)),
        out_shape=jax.ShapeDtypeStruct((t, CONV_WIDTH + POOL_WIDTH), BF16),
        compiler_params=_params(("parallel",), 48),
        name="conv_pool",
    )(proj, proj, proj, proj, proj, proj, proj, proj, proj, proj,
      conv_w, pool_w, pool_scale.reshape(1, POOL_WIDTH))


def _trunk(x_parts, segments, attn_calls, ln_mix_pre, ln_mix_post, ln_ffn_pre, ln_ffn_post, q_norm, k_norm,
           w_in, conv_w, pool_w, pool_scale, w_out, w_gate, w_up, w_down):
    total = sum(n for _, _, n in x_parts)
    cos, sin = _rope_tables(max(n for _, n in segments))
    key_block = min(ATTN_KEY_BLOCK, min(n for _, n in segments))
    out_parts = [(row0, n) for _, row0, n in x_parts]
    h = norm_cast(x_parts, ln_mix_pre[0], total)
    w_in_bf16 = cast_weight(w_in, 0)
    d_model = w_out.shape[2]
    for l in range(DEPTH):
        proj, (w_out_bf16, w_gate_bf16, w_up_bf16) = matmul(
            h, w_in_bf16, BF16,
            [(w_out, l, w_out.shape[1], d_model), (w_gate, l, d_model, D_FF_PAD), (w_up, l, d_model, D_FF_PAD)])
        qt, k_rot, vt = qkv_prep(proj, cos, sin, q_norm[l], k_norm[l], key_block, segments)
        a_parts = [(attention(qt, k_rot, vt, row0, batch, seq), row0, batch * seq) for row0, batch, seq in attn_calls]
        cp_out = conv_pool(proj, conv_w[l], pool_w[l].astype(BF16), pool_scale[l], segments)
        mix = matmul_two_lhs(a_parts, cp_out, w_out_bf16)
        x, h = resnorm(mix, x_parts, ln_mix_post[l], ln_ffn_pre[l])
        x_parts = [(x, 0, total)]
        jobs = [(w_down, l, D_FF_PAD, d_model)]
        if l + 1 < DEPTH:
            jobs.append((w_in, l + 1, d_model, w_in.shape[2]))
        act, cast = gate_up(h, w_gate_bf16, w_up_bf16, jobs)
        w_down_bf16 = cast[0]
        w_in_bf16 = cast[1] if l + 1 < DEPTH else None
        f = matmul_split_k(act, w_down_bf16, bk=D_FF_PAD // 4)
        if l + 1 < DEPTH:
            x, h = resnorm(f, x_parts, ln_ffn_post[l], ln_mix_pre[l + 1])
            x_parts = [(x, 0, total)]
    return resnorm_last_to_parts(f, x, ln_ffn_post[DEPTH - 1], out_parts)


def kernel(x_prompt, x_sample, ln_mix_pre, ln_mix_post, ln_ffn_pre, ln_ffn_post, q_norm, k_norm,
           w_in, conv_w, pool_w, pool_scale, w_out, w_gate, w_up, w_down):
    pb, ps, d = x_prompt.shape
    sb, ss, _ = x_sample.shape
    x_parts = [(x_prompt.reshape(pb * ps, d), 0, pb * ps), (x_sample.reshape(sb * ss, d), pb * ps, sb * ss)]
    segments = tuple((b * ps, ps) for b in range(pb)) + tuple((pb * ps + b * ss, ss) for b in range(sb))
    attn_calls = ((0, pb, ps), (pb * ps, sb, ss))
    y_prompt, y_sample = _trunk(x_parts, segments, attn_calls, ln_mix_pre, ln_mix_post, ln_ffn_pre, ln_ffn_post,
                                q_norm, k_norm, w_in, conv_w, pool_w, pool_scale, w_out, w_gate, w_up, w_down)
    return (y_prompt.reshape(pb, ps, d), y_sample.reshape(sb, ss, d))
```

```python
import functools

import numpy as np
import jax
import jax.numpy as jnp
from jax import lax
from jax.experimental import pallas as pl
from jax.experimental.pallas import tpu as pltpu

D_MODEL = 4096
DEPTH = 2
GRID_W = 64
HEAD_DIM = 128
N_Q_HEADS = 16
N_KV_HEADS = 4
Q_PER_KV = N_Q_HEADS // N_KV_HEADS
ATTN_WIDTH = N_Q_HEADS * HEAD_DIM
KV_WIDTH = N_KV_HEADS * HEAD_DIM
CONV_WIDTH = D_MODEL // 4
POOL_WIDTH = D_MODEL // 4
POOL_WINDOWS = (2, 4, 8, 16)
POOL_GROUP = POOL_WIDTH // len(POOL_WINDOWS)
D_FF = 11008
D_FF_PAD = 11264
ROPE_HALF = HEAD_DIM // 2
ROPE_THETA = 10000.0
EPS = 1e-6

Q_OFF = 0
K_OFF = ATTN_WIDTH
V_OFF = K_OFF + KV_WIDTH
CH_OFF = V_OFF + KV_WIDTH
CB_OFF = CH_OFF + CONV_WIDTH
CC_OFF = CB_OFF + CONV_WIDTH
PU_OFF = CC_OFF + CONV_WIDTH

HALO = 16
V7X_VMEM_BYTES = 64 * 1024 * 1024

F32 = jnp.float32
BF16 = jnp.bfloat16


def _params(semantics, vmem_mb):
    assert vmem_mb * 1024 * 1024 < V7X_VMEM_BYTES
    return pltpu.CompilerParams(dimension_semantics=semantics,
                                vmem_limit_bytes=vmem_mb * 1024 * 1024)


def _rms(x):
    return x * lax.rsqrt(jnp.mean(x * x, axis=-1, keepdims=True) + EPS)


def _part_specs(parts, block_rows, width, block_of):
    specs = []
    for _, row0, n in parts:
        assert row0 % block_rows == 0 and n % block_rows == 0
        first, count = row0 // block_rows, n // block_rows
        specs.append(pl.BlockSpec(
            (block_rows, width), lambda *g, first=first, count=count: (jnp.clip(block_of(*g) - first, 0, count - 1), 0)))
    return specs


def _part_select(parts, block_rows, block, refs):
    value = refs[0][...]
    for (_, row0, _), ref in zip(parts[1:], refs[1:]):
        value = jnp.where(block >= row0 // block_rows, ref[...], value)
    return value


def _rowwise_call(body, name, slab_row0, n_rows, sources, gains, out_dtypes, rows=256):
    d = sources[0][0][0].shape[1]
    first_block = slab_row0 // rows
    assert slab_row0 % rows == 0 and n_rows % rows == 0
    in_specs, args = [], []
    for parts in sources:
        in_specs += _part_specs(parts, rows, d, lambda i: first_block + i)
        args += [a for a, _, _ in parts]
    in_specs += [pl.BlockSpec((1, d), lambda i: (0, 0)) for _ in gains]
    args += [g.reshape(1, d) for g in gains]

    def kernel(*refs):
        block = first_block + pl.program_id(0)
        values, at = [], 0
        for parts in sources:
            values.append(_part_select(parts, rows, block, refs[at:at + len(parts)]))
            at += len(parts)
        body(*values, *refs[at:])

    return pl.pallas_call(
        kernel,
        grid=(n_rows // rows,),
        in_specs=in_specs,
        out_specs=[pl.BlockSpec((rows, d), lambda i: (i, 0)) for _ in out_dtypes],
        out_shape=[jax.ShapeDtypeStruct((n_rows, d), dt) for dt in out_dtypes],
        compiler_params=_params(("parallel",), 48),
        name=name,
    )(*args)


def _norm_cast_body(x, g_ref, h_ref):
    h_ref[...] = (_rms(x) * g_ref[...]).astype(h_ref.dtype)


def _resnorm_body(f, x, g_post_ref, g_next_ref, xo_ref, h_ref):
    xn = x + _rms(f.astype(F32)) * g_post_ref[...]
    xo_ref[...] = xn
    h_ref[...] = (_rms(xn) * g_next_ref[...]).astype(h_ref.dtype)


def _resnorm_last_body(f, x, g_post_ref, xo_ref):
    xo_ref[...] = x + _rms(f.astype(F32)) * g_post_ref[...]


def norm_cast(x_parts, g, total):
    return _rowwise_call(_norm_cast_body, "norm_cast", 0, total, [x_parts], [g], [BF16])[0]


def resnorm(f, x_parts, g_post, g_next):
    total = f.shape[0]
    return _rowwise_call(_resnorm_body, "resnorm", 0, total, [[(f, 0, total)], x_parts], [g_post, g_next],
                         [F32, BF16])


def resnorm_last_to_parts(f, x, g_post, parts):
    total = f.shape[0]
    return [_rowwise_call(_resnorm_last_body, "resnorm_last", row0, n, [[(f, 0, total)], [(x, 0, total)]],
                          [g_post], [F32])[0] for row0, n in parts]


CAST_JOB_ROW_CHOICES = (32, 64, 128, 256, 512, 1024)


class CastJob:
    def __init__(self, w, layer, out_rows, out_cols, n_steps):
        _, n_rows, n_cols = w.shape
        fits = [r for r in CAST_JOB_ROW_CHOICES
                if n_rows % r == 0 and out_rows % r == 0 and out_rows // r <= n_steps]
        self.feasible = bool(fits)
        if not fits:
            return
        self.w, self.layer, self.rows = w, layer, fits[0]
        self.n_cols, self.out_rows, self.out_cols = n_cols, out_rows, out_cols
        self.in_blocks, self.out_blocks = n_rows // self.rows, out_rows // self.rows

    def specs(self, step_of):
        in_spec = pl.BlockSpec((None, self.rows, self.n_cols),
                               lambda *g: (self.layer, jnp.minimum(step_of(*g), self.in_blocks - 1), 0))
        out_spec = pl.BlockSpec((self.rows, self.out_cols),
                                lambda *g: (jnp.minimum(step_of(*g), self.out_blocks - 1), 0))
        return in_spec, out_spec, jax.ShapeDtypeStruct((self.out_rows, self.out_cols), BF16)

    def run(self, step, w_ref, o_ref):
        @pl.when(step < self.in_blocks)
        def _():
            o_ref[:, :self.n_cols] = w_ref[...].astype(o_ref.dtype)
            if self.out_cols > self.n_cols:
                o_ref[:, self.n_cols:] = jnp.zeros((self.rows, self.out_cols - self.n_cols), o_ref.dtype)

        if self.out_blocks > self.in_blocks:
            @pl.when((step >= self.in_blocks) & (step < self.out_blocks))
            def _():
                o_ref[...] = jnp.zeros(o_ref.shape, o_ref.dtype)


def _cast_weight_kernel(w_ref, o_ref, *, valid_rows, valid_cols):
    br, bc = o_ref.shape
    r = pl.program_id(0) * br + lax.broadcasted_iota(jnp.int32, (br, bc), 0)
    c = pl.program_id(1) * bc + lax.broadcasted_iota(jnp.int32, (br, bc), 1)
    o_ref[...] = jnp.where((r < valid_rows) & (c < valid_cols), w_ref[...], 0.0).astype(o_ref.dtype)


def cast_weight(w, layer, out_rows=None, out_cols=None, br=512, bc=1024):
    _, rows, cols = w.shape
    out_rows = out_rows or rows
    out_cols = out_cols or cols
    return pl.pallas_call(
        functools.partial(_cast_weight_kernel, valid_rows=rows, valid_cols=cols),
        grid=(out_rows // br, out_cols // bc),
        in_specs=[pl.BlockSpec((None, br, bc), lambda i, j: (layer, i, j))],
        out_specs=pl.BlockSpec((br, bc), lambda i, j: (i, j)),
        out_shape=jax.ShapeDtypeStruct((out_rows, out_cols), BF16),
        compiler_params=_params(("parallel", "parallel"), 32),
        name="cast_weight",
    )(w)


def _with_cast_jobs(body, n_in, n_out, jobs, n_j):
    n_jobs = len(jobs)

    def kernel(*refs):
        ins, job_ins = refs[:n_in], refs[n_in:n_in + n_jobs]
        outs = refs[n_in + n_jobs:n_in + n_jobs + n_out]
        job_outs = refs[n_in + n_jobs + n_out:n_in + n_jobs + n_out + n_jobs]
        body(*ins, *outs, *refs[n_in + n_jobs + n_out + n_jobs:])
        step = pl.program_id(0) * n_j + pl.program_id(1)
        for job, w_ref, o_ref in zip(jobs, job_ins, job_outs):
            job.run(step, w_ref, o_ref)

    return kernel


def _matmul_call(body, name, ins, in_specs, out_spec, out_shape, grid, job_requests, vmem_mb):
    n_i, n_j = grid
    jobs = [CastJob(w, layer, r, c, n_i * n_j) for w, layer, r, c in job_requests]
    riding = [job for job in jobs if job.feasible]
    job_specs = [job.specs(lambda i, j: i * n_j + j) for job in riding]
    outs = pl.pallas_call(
        _with_cast_jobs(body, len(ins), 1, riding, n_j),
        grid=grid,
        in_specs=list(in_specs) + [s[0] for s in job_specs],
        out_specs=[out_spec] + [s[1] for s in job_specs],
        out_shape=[out_shape] + [s[2] for s in job_specs],
        compiler_params=_params(("arbitrary", "arbitrary"), vmem_mb),
        name=name,
    )(*ins, *[job.w for job in riding])
    cast = iter(outs[1:])
    weights = [next(cast) if job.feasible else cast_weight(w, layer, r, c)
               for job, (w, layer, r, c) in zip(jobs, job_requests)]
    return outs[0], weights


def _mm_kernel(a_ref, b_ref, o_ref):
    o_ref[...] = jnp.dot(a_ref[...], b_ref[...], preferred_element_type=F32).astype(o_ref.dtype)


def matmul(a, b, out_dtype, job_requests=(), bm=1024, bn=1024):
    m, k = a.shape
    _, n = b.shape
    bm = min(bm, m)
    return _matmul_call(
        _mm_kernel, "matmul_full_k", [a, b],
        [pl.BlockSpec((bm, k), lambda i, j: (i, 0)), pl.BlockSpec((k, bn), lambda i, j: (0, j))],
        pl.BlockSpec((bm, bn), lambda i, j: (i, j)), jax.ShapeDtypeStruct((m, n), out_dtype),
        (m // bm, n // bn), job_requests, 56)


def matmul_two_lhs(a1_parts, a2, b, bm=1024, bn=1024):
    m, k2 = a2.shape
    k1 = a1_parts[0][0].shape[1]
    assert k1 == k2 and b.shape[0] == k1 + k2
    n = b.shape[1]
    bm = min([bm] + [rows for _, _, rows in a1_parts])
    n_parts = len(a1_parts)
    bounds = [row0 // bm for _, row0, _ in a1_parts] + [m // bm]

    def kernel(*refs):
        a2_ref, b1_ref, b2_ref, o_ref = refs[n_parts:]
        i = pl.program_id(0)
        for p in range(n_parts):
            @pl.when((i >= bounds[p]) & (i < bounds[p + 1]))
            def _(a1_ref=refs[p]):
                o_ref[...] = (jnp.dot(a1_ref[...], b1_ref[...], preferred_element_type=F32)
                              + jnp.dot(a2_ref[...], b2_ref[...], preferred_element_type=F32)).astype(o_ref.dtype)

    return pl.pallas_call(
        kernel,
        grid=(m // bm, n // bn),
        in_specs=_part_specs(a1_parts, bm, k1, lambda i, j: i) + [
            pl.BlockSpec((bm, k2), lambda i, j: (i, 0)),
            pl.BlockSpec((k1, bn), lambda i, j: (0, j)),
            pl.BlockSpec((k2, bn), lambda i, j: (1, j))],
        out_specs=pl.BlockSpec((bm, bn), lambda i, j: (i, j)),
        out_shape=jax.ShapeDtypeStruct((m, n), BF16),
        compiler_params=_params(("parallel", "parallel"), 56),
        name="matmul_two_lhs",
    )(*[a for a, _, _ in a1_parts], a2, b, b)


def _mm_acc_kernel(a_ref, b_ref, o_ref, acc_ref, *, nk):
    k = pl.program_id(2)

    @pl.when(k == 0)
    def _():
        acc_ref[...] = jnp.dot(a_ref[...], b_ref[...], preferred_element_type=F32)

    @pl.when((k > 0) & (k < nk - 1))
    def _():
        acc_ref[...] += jnp.dot(a_ref[...], b_ref[...], preferred_element_type=F32)

    @pl.when(k == nk - 1)
    def _():
        o_ref[...] = (acc_ref[...] + jnp.dot(a_ref[...], b_ref[...], preferred_element_type=F32)
                      ).astype(o_ref.dtype)


def matmul_split_k(a, b, bk, bm=1024, bn=1024):
    m, k = a.shape
    _, n = b.shape
    bm = min(bm, m)
    nk = k // bk
    assert nk >= 2
    return pl.pallas_call(
        functools.partial(_mm_acc_kernel, nk=nk),
        grid=(m // bm, n // bn, nk),
        in_specs=[pl.BlockSpec((bm, bk), lambda i, j, kk: (i, kk)),
                  pl.BlockSpec((bk, bn), lambda i, j, kk: (kk, j))],
        out_specs=pl.BlockSpec((bm, bn), lambda i, j, kk: (i, j)),
        out_shape=jax.ShapeDtypeStruct((m, n), BF16),
        scratch_shapes=[pltpu.VMEM((bm, bn), F32)],
        compiler_params=_params(("parallel", "parallel", "arbitrary"), 48),
        name="matmul_split_k",
    )(a, b)


def _gateup_kernel(h_ref, wg_ref, wu_ref, o_ref):
    h = h_ref[...]
    g = jnp.dot(h, wg_ref[...], preferred_element_type=F32)
    u = jnp.dot(h, wu_ref[...], preferred_element_type=F32)
    o_ref[...] = (g * jax.nn.sigmoid(g) * u).astype(o_ref.dtype)


def gate_up(h, wg, wu, job_requests=(), bm=1024, bn=512):
    m, k = h.shape
    _, n = wg.shape
    bm = min(bm, m)
    w_spec = pl.BlockSpec((k, bn), lambda i, j: (0, j))
    return _matmul_call(
        _gateup_kernel, "gate_up", [h, wg, wu],
        [pl.BlockSpec((bm, k), lambda i, j: (i, 0)), w_spec, w_spec],
        pl.BlockSpec((bm, bn), lambda i, j: (i, j)), jax.ShapeDtypeStruct((m, n), BF16),
        (m // bm, n // bn), job_requests, 56)


ATTN_KEY_BLOCK = 1024
ATTN_KEYS_PER_LOOP_TRIP = 4096
V_ROWS = HEAD_DIM + 16
LOG2E = 1.4426950408889634


def _rope_tables(max_len):
    pos = np.arange(max_len)
    r = (pos // GRID_W).astype(np.float32)
    c = (pos % GRID_W).astype(np.float32)
    inv = (1.0 / (ROPE_THETA ** (np.arange(0, ROPE_HALF, 2, dtype=np.float32) / ROPE_HALF))).astype(np.float32)
    ang_r = r[:, None] * inv
    ang_c = c[:, None] * inv
    cos = np.concatenate([np.cos(ang_r), np.cos(ang_r), np.cos(ang_c), np.cos(ang_c)], axis=-1)
    sin = np.concatenate([-np.sin(ang_r), np.sin(ang_r), -np.sin(ang_c), np.sin(ang_c)], axis=-1)
    return jnp.asarray(cos, F32), jnp.asarray(sin, F32)


def _norm_rope(x, gain, cos, sin, upper):
    xn = _rms(x) * gain
    quarter = ROPE_HALF // 2
    partner = jnp.where(upper, pltpu.roll(xn, quarter, axis=1), pltpu.roll(xn, HEAD_DIM - quarter, axis=1))
    return xn * cos + partner * sin


def _qkv_prep_kernel(q_ref, k_ref, v_ref, cos_ref, sin_ref, qg_ref, kg_ref, qo_ref, ko_ref, vt_ref):
    cos = cos_ref[...]
    sin = sin_ref[...]
    rows = cos.shape[0]
    lane = lax.broadcasted_iota(jnp.int32, cos.shape, 1)
    upper = (lane & (ROPE_HALF // 2)) != 0
    q_scale = HEAD_DIM ** -0.5 * LOG2E
    qg = qg_ref[...]
    kg = kg_ref[...]
    for h in range(N_Q_HEADS):
        sl = slice(h * HEAD_DIM, (h + 1) * HEAD_DIM)
        y = _norm_rope(q_ref[:, sl].astype(F32), qg, cos, sin, upper) * q_scale
        qo_ref[:, sl] = y.astype(qo_ref.dtype)
    for h in range(N_KV_HEADS):
        sl = slice(h * HEAD_DIM, (h + 1) * HEAD_DIM)
        y = _norm_rope(k_ref[:, sl].astype(F32), kg, cos, sin, upper)
        ko_ref[:, sl] = y.astype(ko_ref.dtype)
        vt_ref[h, 0, :HEAD_DIM, :] = v_ref[:, sl].astype(F32).T.astype(vt_ref.dtype)
        vt_ref[h, 0, HEAD_DIM:, :] = jnp.ones((V_ROWS - HEAD_DIM, rows), vt_ref.dtype)


def qkv_prep(proj, cos, sin, q_gain, k_gain, key_block, segments, rows=512):
    t = proj.shape[0]
    rows = min(rows, key_block)
    per_key_block = key_block // rows
    assert all(s0 % rows == 0 for s0, _ in segments)

    def position_block(i):
        block = i
        for s0, _ in segments:
            block = jnp.where(i * rows >= s0, i - s0 // rows, block)
        return block

    return pl.pallas_call(
        _qkv_prep_kernel,
        grid=(t // rows,),
        in_specs=[pl.BlockSpec((rows, ATTN_WIDTH), lambda i: (i, Q_OFF // ATTN_WIDTH)),
                  pl.BlockSpec((rows, KV_WIDTH), lambda i: (i, K_OFF // KV_WIDTH)),
                  pl.BlockSpec((rows, KV_WIDTH), lambda i: (i, V_OFF // KV_WIDTH)),
                  pl.BlockSpec((rows, HEAD_DIM), lambda i: (position_block(i), 0)),
                  pl.BlockSpec((rows, HEAD_DIM), lambda i: (position_block(i), 0)),
                  pl.BlockSpec((1, HEAD_DIM), lambda i: (0, 0)),
                  pl.BlockSpec((1, HEAD_DIM), lambda i: (0, 0))],
        out_specs=[pl.BlockSpec((rows, ATTN_WIDTH), lambda i: (i, 0)),
                   pl.BlockSpec((rows, KV_WIDTH), lambda i: (i, 0)),
                   pl.BlockSpec((N_KV_HEADS, 1, V_ROWS, rows),
                                lambda i: (0, i // per_key_block, 0, i % per_key_block))],
        out_shape=[jax.ShapeDtypeStruct((t, ATTN_WIDTH), BF16),
                   jax.ShapeDtypeStruct((t, KV_WIDTH), BF16),
                   jax.ShapeDtypeStruct((N_KV_HEADS, t // key_block, V_ROWS, key_block), BF16)],
        compiler_params=_params(("parallel",), 40),
        name="qkv_prep",
    )(proj, proj, proj, cos, sin, q_gain.reshape(1, HEAD_DIM), k_gain.reshape(1, HEAD_DIM))


def _attn_kernel(q_ref, k_ref, vt_ref, *rest, bk, nk):
    o_ref, m_scr, acc_scr, s_scr, smax_scr = rest[-5:]
    m_scr[...] = jnp.full(m_scr.shape, -jnp.inf, F32)
    acc_scr[...] = jnp.zeros(acc_scr.shape, F32)

    def scores(kb, h, slot):
        s = lax.dot_general(kb, q_ref[:, h * HEAD_DIM:(h + 1) * HEAD_DIM], (((1,), (1,)), ((), ())),
                            preferred_element_type=F32)
        s_scr[slot] = s
        smax_scr[slot] = jnp.max(s, axis=0, keepdims=True)

    def key_block(j):
        return k_ref[pl.ds(pl.multiple_of(j * bk, bk), bk), :]

    scores(key_block(0), 0, 0)

    def body(j, carry):
        kb = key_block(j)
        kb_next = key_block(jnp.minimum(j + 1, nk - 1))
        vb = vt_ref[j]
        for h in range(Q_PER_KV):
            slot = h % 2
            if h + 1 < Q_PER_KV:
                scores(kb, h + 1, 1 - slot)
            else:
                scores(kb_next, 0, 1 - slot)
            m_prev = m_scr[h]
            m_new = jnp.maximum(m_prev, smax_scr[slot])
            alpha = jnp.exp2(m_prev - m_new)
            p = jnp.exp2(s_scr[slot] - m_new).astype(vb.dtype)
            acc_scr[h] = acc_scr[h] * alpha + jnp.dot(vb, p, preferred_element_type=F32)
            m_scr[h] = m_new
        return carry

    lax.fori_loop(0, nk, body, 0, unroll=max(1, min(ATTN_KEYS_PER_LOOP_TRIP // bk, nk // 2)))
    for h in range(Q_PER_KV):
        acc = acc_scr[h]
        o = acc[:HEAD_DIM, :] / acc[HEAD_DIM:HEAD_DIM + 1, :]
        o_ref[:, h * HEAD_DIM:(h + 1) * HEAD_DIM] = o.T.astype(o_ref.dtype)


def attention(q_rot, k_rot, vt, row0, batch, seq, bq=512):
    bk = vt.shape[-1]
    bq = min(bq, seq)
    assert row0 % seq == 0 and seq % bq == 0 and seq % bk == 0
    group_w = Q_PER_KV * HEAD_DIM
    nq = seq // bq
    nk = seq // bk
    in_specs = [
        pl.BlockSpec((bq, group_w), lambda b, h, i: (row0 // bq + b * nq + i, h)),
        pl.BlockSpec((seq, HEAD_DIM), lambda b, h, i: (row0 // seq + b, h)),
        pl.BlockSpec((None, nk, V_ROWS, bk), lambda b, h, i: (h, row0 // seq + b, 0, 0)),
    ]
    return pl.pallas_call(
        functools.partial(_attn_kernel, bk=bk, nk=nk),
        grid=(batch, N_KV_HEADS, nq),
        in_specs=in_specs,
        out_specs=pl.BlockSpec((bq, group_w), lambda b, h, i: (b * nq + i, h)),
        out_shape=jax.ShapeDtypeStruct((batch * seq, ATTN_WIDTH), BF16),
        scratch_shapes=[pltpu.VMEM((Q_PER_KV, 1, bq), F32),
                        pltpu.VMEM((Q_PER_KV, V_ROWS, bq), F32),
                        pltpu.VMEM((2, bk, bq), F32),
                        pltpu.VMEM((2, 1, bq), F32)],
        compiler_params=_params(("parallel", "parallel", "parallel"), 48),
        name="attention",
    )(q_rot, k_rot, vt)


def _convpool_kernel(ch_ref, cb_ref, cc_ref, pu_ref,
                     chp_ref, ccp_ref, pup_ref, chn_ref, ccn_ref, pun_ref,
                     cw_ref, pw_ref, ps_ref, o_ref, *, rows, segments):
    i = pl.program_id(0)
    start = i * rows
    pos0 = start
    seq_len = jnp.int32(0)
    has_prev = jnp.bool_(True)
    has_next = jnp.bool_(True)
    for seg_start, seg_len in segments:
        inside = (start >= seg_start) & (start < seg_start + seg_len)
        pos0 = jnp.where(inside, start - seg_start, pos0)
        seq_len = jnp.where(inside, seg_len, seq_len)
        has_prev = has_prev & (start != seg_start)
        has_next = has_next & (start + rows != seg_start + seg_len)
    prev_on = has_prev.astype(F32)
    next_on = has_next.astype(F32)

    def extended(cur_ref, prev_ref, next_ref):
        return jnp.concatenate([prev_ref[...].astype(F32) * prev_on,
                                cur_ref[...].astype(F32),
                                next_ref[...].astype(F32) * next_on], axis=0)

    ext_rows = rows + 2 * HALO

    def shifted(x, d):
        return pltpu.roll(x, d % ext_rows, axis=0)

    u = extended(cc_ref, ccp_ref, ccn_ref) * extended(ch_ref, chp_ref, chn_ref)
    cw = cw_ref[...]
    y = shifted(u, 1) * cw[0:1, :] + u * cw[1:2, :] + shifted(u, -1) * cw[2:3, :]
    o_ref[:, :CONV_WIDTH] = (cb_ref[...].astype(F32) * y[HALO:HALO + rows, :]).astype(o_ref.dtype)

    x = extended(pu_ref, pup_ref, pun_ref)
    pos = (pos0 + lax.broadcasted_iota(jnp.int32, (rows, 1), 0))
    for gi, w in enumerate(POOL_WINDOWS):
        sl = slice(gi * POOL_GROUP, (gi + 1) * POOL_GROUP)
        xg = x[:, sl]
        tot = xg + shifted(xg, 1)
        half = 1
        while 2 * half < w:
            tot = shifted(tot, half) + shifted(tot, -half)
            half *= 2
        hi = jnp.minimum(pos + w // 2, seq_len)
        lo = jnp.maximum(pos - w // 2, 0)
        cnt = (hi - lo).astype(F32)
        m = tot[HALO:HALO + rows, :] / cnt - xg[HALO:HALO + rows, :]
        yg = jnp.dot(m.astype(BF16), pw_ref[gi], preferred_element_type=F32)
        o_ref[:, CONV_WIDTH + gi * POOL_GROUP:CONV_WIDTH + (gi + 1) * POOL_GROUP] = (
            yg * ps_ref[:, sl]).astype(o_ref.dtype)


def conv_pool(proj, conv_w, pool_w, pool_scale, segments, rows=512):
    t = proj.shape[0]
    rows = min(rows, min(n for _, n in segments))
    assert all(s % rows == 0 and n % rows == 0 for s, n in segments)
    w = CONV_WIDTH
    per = rows // HALO
    last = t // HALO - 1

    def cur(off):
        return pl.BlockSpec((rows, w), lambda i: (i, off // w))

    def prev(off):
        return pl.BlockSpec((HALO, w), lambda i: (jnp.maximum(i * per - 1, 0), off // w))

    def nxt(off):
        return pl.BlockSpec((HALO, w), lambda i: (jnp.minimum((i + 1) * per, last), off // w))

    kernel = functools.partial(_convpool_kernel, rows=rows, segments=segments)
    return pl.pallas_call(
        kernel,
        grid=(t // rows,),
        in_specs=[cur(CH_OFF), cur(CB_OFF), cur(CC_OFF), cur(PU_OFF),
                  prev(CH_OFF), prev(CC_OFF), prev(PU_OFF),
                  nxt(CH_OFF), nxt(CC_OFF), nxt(PU_OFF),
                  pl.BlockSpec((3, w), lambda i: (0, 0)),
                  pl.BlockSpec((len(POOL_WINDOWS), POOL_GROUP, POOL_GROUP), lambda i: (0, 0, 0)),
                  pl.BlockSpec((1, POOL_WIDTH), lambda i: (0, 0))],
        out_specs=pl.BlockSpec((rows, CONV_WIDTH + POOL_WIDTH), lambda i: (i, 0)),
        out_shape=jax.ShapeDtypeStruct((t, CONV_WIDTH + POOL_WIDTH), BF16),
        compiler_params=_params(("parallel",), 48),
        name="conv_pool",
    )(proj, proj, proj, proj, proj, proj, proj, proj, proj, proj,
      conv_w, pool_w, pool_scale.reshape(1, POOL_WIDTH))


def _trunk(x_parts, segments, attn_calls, ln_mix_pre, ln_mix_post, ln_ffn_pre, ln_ffn_post, q_norm, k_norm,
           w_in, conv_w, pool_w, pool_scale, w_out, w_gate, w_up, w_down):
    total = sum(n for _, _, n in x_parts)
    cos, sin = _rope_tables(max(n for _, n in segments))
    key_block = min(ATTN_KEY_BLOCK, min(n for _, n in segments))
    out_parts = [(row0, n) for _, row0, n in x_parts]
    h = norm_cast(x_parts, ln_mix_pre[0], total)
    w_in_bf16 = cast_weight(w_in, 0)
    d_model = w_out.shape[2]
    for l in range(DEPTH):
        proj, (w_out_bf16, w_gate_bf16, w_up_bf16) = matmul(
            h, w_in_bf16, BF16,
            [(w_out, l, w_out.shape[1], d_model), (w_gate, l, d_model, D_FF_PAD), (w_up, l, d_model, D_FF_PAD)])
        q_rot, k_rot, vt = qkv_prep(proj, cos, sin, q_norm[l], k_norm[l], key_block, segments)
        a_parts = [(attention(q_rot, k_rot, vt, row0, batch, seq), row0, batch * seq)
                   for row0, batch, seq in attn_calls]
        cp_out = conv_pool(proj, conv_w[l], pool_w[l].astype(BF16), pool_scale[l], segments)
        mix = matmul_two_lhs(a_parts, cp_out, w_out_bf16)
        x, h = resnorm(mix, x_parts, ln_mix_post[l], ln_ffn_pre[l])
        x_parts = [(x, 0, total)]
        jobs = [(w_down, l, D_FF_PAD, d_model)]
        if l + 1 < DEPTH:
            jobs.append((w_in, l + 1, d_model, w_in.shape[2]))
        act, cast = gate_up(h, w_gate_bf16, w_up_bf16, jobs)
        w_down_bf16 = cast[0]
        w_in_bf16 = cast[1] if l + 1 < DEPTH else None
        f = matmul_split_k(act, w_down_bf16, bk=D_FF_PAD // 4)
        if l + 1 < DEPTH:
            x, h = resnorm(f, x_parts, ln_ffn_post[l], ln_mix_pre[l + 1])
            x_parts = [(x, 0, total)]
    return resnorm_last_to_parts(f, x, ln_ffn_post[DEPTH - 1], out_parts)


def kernel(x_prompt, x_sample, ln_mix_pre, ln_mix_post, ln_ffn_pre, ln_ffn_post, q_norm, k_norm,
           w_in, conv_w, pool_w, pool_scale, w_out, w_gate, w_up, w_down):
    pb, ps, d = x_prompt.shape
    sb, ss, _ = x_sample.shape
    x_parts = [(x_prompt.reshape(pb * ps, d), 0, pb * ps), (x_sample.reshape(sb * ss, d), pb * ps, sb * ss)]
    segments = tuple((b * ps, ps) for b in range(pb)) + tuple((pb * ps + b * ss, ss) for b in range(sb))
    attn_calls = ((0, pb, ps), (pb * ps, sb, ss))
    y_prompt, y_sample = _trunk(x_parts, segments, attn_calls, ln_mix_pre, ln_mix_post, ln_ffn_pre, ln_ffn_post,
                                q_norm, k_norm, w_in, conv_w, pool_w, pool_scale, w_out, w_gate, w_up, w_down)
    return (y_prompt.reshape(pb, ps, d), y_sample.reshape(sb, ss, d))
```

```python
import functools

import numpy as np
import jax
import jax.numpy as jnp
from jax import lax
from jax.experimental import pallas as pl
from jax.experimental.pallas import tpu as pltpu

D_MODEL = 4096
DEPTH = 2
GRID_W = 64
HEAD_DIM = 128
N_Q_HEADS = 16
N_KV_HEADS = 4
Q_PER_KV = N_Q_HEADS // N_KV_HEADS
ATTN_WIDTH = N_Q_HEADS * HEAD_DIM
KV_WIDTH = N_KV_HEADS * HEAD_DIM
CONV_WIDTH = D_MODEL // 4
POOL_WIDTH = D_MODEL // 4
POOL_WINDOWS = (2, 4, 8, 16)
POOL_GROUP = POOL_WIDTH // len(POOL_WINDOWS)
D_FF = 11008
D_FF_PAD = 11264
ROPE_HALF = HEAD_DIM // 2
ROPE_THETA = 10000.0
EPS = 1e-6

Q_OFF = 0
K_OFF = ATTN_WIDTH
V_OFF = K_OFF + KV_WIDTH
CH_OFF = V_OFF + KV_WIDTH
CB_OFF = CH_OFF + CONV_WIDTH
CC_OFF = CB_OFF + CONV_WIDTH
PU_OFF = CC_OFF + CONV_WIDTH

HALO = 16
V7X_VMEM_BYTES = 64 * 1024 * 1024

F32 = jnp.float32
BF16 = jnp.bfloat16


def _params(semantics, vmem_mb):
    assert vmem_mb * 1024 * 1024 < V7X_VMEM_BYTES
    return pltpu.CompilerParams(dimension_semantics=semantics,
                                vmem_limit_bytes=vmem_mb * 1024 * 1024)


def _rms(x):
    return x * lax.rsqrt(jnp.mean(x * x, axis=-1, keepdims=True) + EPS)


def _part_specs(parts, block_rows, width, block_of):
    specs = []
    for _, row0, n in parts:
        assert row0 % block_rows == 0 and n % block_rows == 0
        first, count = row0 // block_rows, n // block_rows
        specs.append(pl.BlockSpec(
            (block_rows, width), lambda *g, first=first, count=count: (jnp.clip(block_of(*g) - first, 0, count - 1), 0)))
    return specs


def _part_select(parts, block_rows, block, refs):
    value = refs[0][...]
    for (_, row0, _), ref in zip(parts[1:], refs[1:]):
        value = jnp.where(block >= row0 // block_rows, ref[...], value)
    return value


def _rowwise_call(body, name, slab_row0, n_rows, sources, gains, out_dtypes, rows=256):
    d = sources[0][0][0].shape[1]
    first_block = slab_row0 // rows
    assert slab_row0 % rows == 0 and n_rows % rows == 0
    in_specs, args = [], []
    for parts in sources:
        in_specs += _part_specs(parts, rows, d, lambda i: first_block + i)
        args += [a for a, _, _ in parts]
    in_specs += [pl.BlockSpec((1, d), lambda i: (0, 0)) for _ in gains]
    args += [g.reshape(1, d) for g in gains]

    def kernel(*refs):
        block = first_block + pl.program_id(0)
        values, at = [], 0
        for parts in sources:
            values.append(_part_select(parts, rows, block, refs[at:at + len(parts)]))
            at += len(parts)
        body(*values, *refs[at:])

    return pl.pallas_call(
        kernel,
        grid=(n_rows // rows,),
        in_specs=in_specs,
        out_specs=[pl.BlockSpec((rows, d), lambda i: (i, 0)) for _ in out_dtypes],
        out_shape=[jax.ShapeDtypeStruct((n_rows, d), dt) for dt in out_dtypes],
        compiler_params=_params(("parallel",), 48),
        name=name,
    )(*args)


def _norm_cast_body(x, g_ref, h_ref):
    h_ref[...] = (_rms(x) * g_ref[...]).astype(h_ref.dtype)


def _resnorm_body(f, x, g_post_ref, g_next_ref, xo_ref, h_ref):
    xn = x + _rms(f.astype(F32)) * g_post_ref[...]
    xo_ref[...] = xn
    h_ref[...] = (_rms(xn) * g_next_ref[...]).astype(h_ref.dtype)


def _resnorm_last_body(f, x, g_post_ref, xo_ref):
    xo_ref[...] = x + _rms(f.astype(F32)) * g_post_ref[...]


def norm_cast(x_parts, g, total):
    return _rowwise_call(_norm_cast_body, "norm_cast", 0, total, [x_parts], [g], [BF16])[0]


def resnorm(f, x_parts, g_post, g_next):
    total = f.shape[0]
    return _rowwise_call(_resnorm_body, "resnorm", 0, total, [[(f, 0, total)], x_parts], [g_post, g_next],
                         [F32, BF16])


def resnorm_last_to_parts(f, x, g_post, parts):
    total = f.shape[0]
    return [_rowwise_call(_resnorm_last_body, "resnorm_last", row0, n, [[(f, 0, total)], [(x, 0, total)]],
                          [g_post], [F32])[0] for row0, n in parts]


CAST_JOB_ROW_CHOICES = (32, 64, 128, 256, 512, 1024)


class CastJob:
    def __init__(self, w, layer, out_rows, out_cols, n_steps):
        _, n_rows, n_cols = w.shape
        fits = [r for r in CAST_JOB_ROW_CHOICES
                if n_rows % r == 0 and out_rows % r == 0 and out_rows // r <= n_steps]
        self.feasible = bool(fits)
        if not fits:
            return
        self.w, self.layer, self.rows = w, layer, fits[0]
        self.n_cols, self.out_rows, self.out_cols = n_cols, out_rows, out_cols
        self.in_blocks, self.out_blocks = n_rows // self.rows, out_rows // self.rows

    def specs(self, step_of):
        in_spec = pl.BlockSpec((None, self.rows, self.n_cols),
                               lambda *g: (self.layer, jnp.minimum(step_of(*g), self.in_blocks - 1), 0))
        out_spec = pl.BlockSpec((self.rows, self.out_cols),
                                lambda *g: (jnp.minimum(step_of(*g), self.out_blocks - 1), 0))
        return in_spec, out_spec, jax.ShapeDtypeStruct((self.out_rows, self.out_cols), BF16)

    def run(self, step, w_ref, o_ref):
        @pl.when(step < self.in_blocks)
        def _():
            o_ref[:, :self.n_cols] = w_ref[...].astype(o_ref.dtype)
            if self.out_cols > self.n_cols:
                o_ref[:, self.n_cols:] = jnp.zeros((self.rows, self.out_cols - self.n_cols), o_ref.dtype)

        if self.out_blocks > self.in_blocks:
            @pl.when((step >= self.in_blocks) & (step < self.out_blocks))
            def _():
                o_ref[...] = jnp.zeros(o_ref.shape, o_ref.dtype)


def _cast_weight_kernel(w_ref, o_ref, *, valid_rows, valid_cols):
    br, bc = o_ref.shape
    r = pl.program_id(0) * br + lax.broadcasted_iota(jnp.int32, (br, bc), 0)
    c = pl.program_id(1) * bc + lax.broadcasted_iota(jnp.int32, (br, bc), 1)
    o_ref[...] = jnp.where((r < valid_rows) & (c < valid_cols), w_ref[...], 0.0).astype(o_ref.dtype)


def cast_weight(w, layer, out_rows=None, out_cols=None, br=512, bc=1024):
    _, rows, cols = w.shape
    out_rows = out_rows or rows
    out_cols = out_cols or cols
    return pl.pallas_call(
        functools.partial(_cast_weight_kernel, valid_rows=rows, valid_cols=cols),
        grid=(out_rows // br, out_cols // bc),
        in_specs=[pl.BlockSpec((None, br, bc), lambda i, j: (layer, i, j))],
        out_specs=pl.BlockSpec((br, bc), lambda i, j: (i, j)),
        out_shape=jax.ShapeDtypeStruct((out_rows, out_cols), BF16),
        compiler_params=_params(("parallel", "parallel"), 32),
        name="cast_weight",
    )(w)


def _with_cast_jobs(body, n_in, n_out, jobs, n_j):
    n_jobs = len(jobs)

    def kernel(*refs):
        ins, job_ins = refs[:n_in], refs[n_in:n_in + n_jobs]
        outs = refs[n_in + n_jobs:n_in + n_jobs + n_out]
        job_outs = refs[n_in + n_jobs + n_out:n_in + n_jobs + n_out + n_jobs]
        body(*ins, *outs, *refs[n_in + n_jobs + n_out + n_jobs:])
        step = pl.program_id(0) * n_j + pl.program_id(1)
        for job, w_ref, o_ref in zip(jobs, job_ins, job_outs):
            job.run(step, w_ref, o_ref)

    return kernel


def _matmul_call(body, name, ins, in_specs, out_spec, out_shape, grid, job_requests, vmem_mb):
    n_i, n_j = grid
    jobs = [CastJob(w, layer, r, c, n_i * n_j) for w, layer, r, c in job_requests]
    riding = [job for job in jobs if job.feasible]
    job_specs = [job.specs(lambda i, j: i * n_j + j) for job in riding]
    outs = pl.pallas_call(
        _with_cast_jobs(body, len(ins), 1, riding, n_j),
        grid=grid,
        in_specs=list(in_specs) + [s[0] for s in job_specs],
        out_specs=[out_spec] + [s[1] for s in job_specs],
        out_shape=[out_shape] + [s[2] for s in job_specs],
        compiler_params=_params(("arbitrary", "arbitrary"), vmem_mb),
        name=name,
    )(*ins, *[job.w for job in riding])
    cast = iter(outs[1:])
    weights = [next(cast) if job.feasible else cast_weight(w, layer, r, c)
               for job, (w, layer, r, c) in zip(jobs, job_requests)]
    return outs[0], weights


def _mm_kernel(a_ref, b_ref, o_ref):
    o_ref[...] = jnp.dot(a_ref[...], b_ref[...], preferred_element_type=F32).astype(o_ref.dtype)


def matmul(a, b, out_dtype, job_requests=(), bm=1024, bn=1024):
    m, k = a.shape
    _, n = b.shape
    bm = min(bm, m)
    return _matmul_call(
        _mm_kernel, "matmul_full_k", [a, b],
        [pl.BlockSpec((bm, k), lambda i, j: (i, 0)), pl.BlockSpec((k, bn), lambda i, j: (0, j))],
        pl.BlockSpec((bm, bn), lambda i, j: (i, j)), jax.ShapeDtypeStruct((m, n), out_dtype),
        (m // bm, n // bn), job_requests, 56)


def matmul_two_lhs(a1_parts, a2, b, bm=1024, bn=1024):
    m, k2 = a2.shape
    k1 = a1_parts[0][0].shape[1]
    assert k1 == k2 and b.shape[0] == k1 + k2
    n = b.shape[1]
    bm = min([bm] + [rows for _, _, rows in a1_parts])
    n_parts = len(a1_parts)
    bounds = [row0 // bm for _, row0, _ in a1_parts] + [m // bm]

    def kernel(*refs):
        a2_ref, b1_ref, b2_ref, o_ref = refs[n_parts:]
        i = pl.program_id(0)
        for p in range(n_parts):
            @pl.when((i >= bounds[p]) & (i < bounds[p + 1]))
            def _(a1_ref=refs[p]):
                o_ref[...] = (jnp.dot(a1_ref[...], b1_ref[...], preferred_element_type=F32)
                              + jnp.dot(a2_ref[...], b2_ref[...], preferred_element_type=F32)).astype(o_ref.dtype)

    return pl.pallas_call(
        kernel,
        grid=(m // bm, n // bn),
        in_specs=_part_specs(a1_parts, bm, k1, lambda i, j: i) + [
            pl.BlockSpec((bm, k2), lambda i, j: (i, 0)),
            pl.BlockSpec((k1, bn), lambda i, j: (0, j)),
            pl.BlockSpec((k2, bn), lambda i, j: (1, j))],
        out_specs=pl.BlockSpec((bm, bn), lambda i, j: (i, j)),
        out_shape=jax.ShapeDtypeStruct((m, n), BF16),
        compiler_params=_params(("parallel", "parallel"), 56),
        name="matmul_two_lhs",
    )(*[a for a, _, _ in a1_parts], a2, b, b)


def _mm_acc_kernel(a_ref, b_ref, o_ref, acc_ref, *, nk):
    k = pl.program_id(2)

    @pl.when(k == 0)
    def _():
        acc_ref[...] = jnp.dot(a_ref[...], b_ref[...], preferred_element_type=F32)

    @pl.when((k > 0) & (k < nk - 1))
    def _():
        acc_ref[...] += jnp.dot(a_ref[...], b_ref[...], preferred_element_type=F32)

    @pl.when(k == nk - 1)
    def _():
        o_ref[...] = (acc_ref[...] + jnp.dot(a_ref[...], b_ref[...], preferred_element_type=F32)
                      ).astype(o_ref.dtype)


def matmul_split_k(a, b, bk, bm=1024, bn=1024):
    m, k = a.shape
    _, n = b.shape
    bm = min(bm, m)
    nk = k // bk
    assert nk >= 2
    return pl.pallas_call(
        functools.partial(_mm_acc_kernel, nk=nk),
        grid=(m // bm, n // bn, nk),
        in_specs=[pl.BlockSpec((bm, bk), lambda i, j, kk: (i, kk)),
                  pl.BlockSpec((bk, bn), lambda i, j, kk: (kk, j))],
        out_specs=pl.BlockSpec((bm, bn), lambda i, j, kk: (i, j)),
        out_shape=jax.ShapeDtypeStruct((m, n), BF16),
        scratch_shapes=[pltpu.VMEM((bm, bn), F32)],
        compiler_params=_params(("parallel", "parallel", "arbitrary"), 48),
        name="matmul_split_k",
    )(a, b)


def _gateup_kernel(h_ref, wg_ref, wu_ref, o_ref):
    h = h_ref[...]
    g = jnp.dot(h, wg_ref[...], preferred_element_type=F32)
    u = jnp.dot(h, wu_ref[...], preferred_element_type=F32)
    o_ref[...] = (g * jax.nn.sigmoid(g) * u).astype(o_ref.dtype)


def gate_up(h, wg, wu, job_requests=(), bm=1024, bn=512):
    m, k = h.shape
    _, n = wg.shape
    bm = min(bm, m)
    w_spec = pl.BlockSpec((k, bn), lambda i, j: (0, j))
    return _matmul_call(
        _gateup_kernel, "gate_up", [h, wg, wu],
        [pl.BlockSpec((bm, k), lambda i, j: (i, 0)), w_spec, w_spec],
        pl.BlockSpec((bm, bn), lambda i, j: (i, j)), jax.ShapeDtypeStruct((m, n), BF16),
        (m // bm, n // bn), job_requests, 56)


ATTN_KEY_BLOCK = 1024
ATTN_KEYS_PER_LOOP_TRIP = 4096
V_ROWS = HEAD_DIM + 16
LOG2E = 1.4426950408889634


def _rope_tables(max_len):
    pos = np.arange(max_len)
    r = (pos // GRID_W).astype(np.float32)
    c = (pos % GRID_W).astype(np.float32)
    inv = (1.0 / (ROPE_THETA ** (np.arange(0, ROPE_HALF, 2, dtype=np.float32) / ROPE_HALF))).astype(np.float32)
    ang_r = r[:, None] * inv
    ang_c = c[:, None] * inv
    cos = np.concatenate([np.cos(ang_r), np.cos(ang_r), np.cos(ang_c), np.cos(ang_c)], axis=-1)
    sin = np.concatenate([-np.sin(ang_r), np.sin(ang_r), -np.sin(ang_c), np.sin(ang_c)], axis=-1)
    return jnp.asarray(cos, F32), jnp.asarray(sin, F32)


PAIR_W = 2 * HEAD_DIM


def _rope_partner_matrix():
    idx = np.arange(PAIR_W)
    perm = np.zeros((PAIR_W, PAIR_W), np.float32)
    perm[idx ^ (ROPE_HALF // 2), idx] = 1.0
    return jnp.asarray(perm, BF16)


def _norm_rope_pair(x, gain, cos, sin, perm):
    xn = jnp.concatenate([_rms(x[:, :HEAD_DIM]), _rms(x[:, HEAD_DIM:])], axis=1) * gain
    hi = xn.astype(BF16)
    lo = (xn - hi.astype(F32)).astype(BF16)
    partner = (jnp.dot(hi, perm, preferred_element_type=F32) + jnp.dot(lo, perm, preferred_element_type=F32))
    return xn * cos + partner * sin


def _qkv_prep_kernel(q_ref, k_ref, v_ref, cos_ref, sin_ref, qg_ref, kg_ref, perm_ref,
                     qo_ref, ko_ref, vt_ref):
    rows = cos_ref.shape[0]
    cos = jnp.concatenate([cos_ref[...]] * 2, axis=1)
    sin = jnp.concatenate([sin_ref[...]] * 2, axis=1)
    perm = perm_ref[...]
    q_scale = HEAD_DIM ** -0.5 * LOG2E
    qg = jnp.concatenate([qg_ref[...]] * 2, axis=1)
    kg = jnp.concatenate([kg_ref[...]] * 2, axis=1)
    for p in range(N_Q_HEADS // 2):
        sl = slice(p * PAIR_W, (p + 1) * PAIR_W)
        y = _norm_rope_pair(q_ref[:, sl].astype(F32), qg, cos, sin, perm) * q_scale
        qo_ref[:, sl] = y.astype(qo_ref.dtype)
    for p in range(N_KV_HEADS // 2):
        sl = slice(p * PAIR_W, (p + 1) * PAIR_W)
        y = _norm_rope_pair(k_ref[:, sl].astype(F32), kg, cos, sin, perm)
        ko_ref[:, sl] = y.astype(ko_ref.dtype)
    for h in range(N_KV_HEADS):
        sl = slice(h * HEAD_DIM, (h + 1) * HEAD_DIM)
        vt_ref[h, 0, :HEAD_DIM, :] = v_ref[:, sl].astype(F32).T.astype(vt_ref.dtype)
        vt_ref[h, 0, HEAD_DIM:, :] = jnp.ones((V_ROWS - HEAD_DIM, rows), vt_ref.dtype)


def qkv_prep(proj, cos, sin, q_gain, k_gain, key_block, segments, rows=512):
    t = proj.shape[0]
    rows = min(rows, key_block)
    per_key_block = key_block // rows
    assert all(s0 % rows == 0 for s0, _ in segments)

    def position_block(i):
        block = i
        for s0, _ in segments:
            block = jnp.where(i * rows >= s0, i - s0 // rows, block)
        return block

    return pl.pallas_call(
        _qkv_prep_kernel,
        grid=(t // rows,),
        in_specs=[pl.BlockSpec((rows, ATTN_WIDTH), lambda i: (i, Q_OFF // ATTN_WIDTH)),
                  pl.BlockSpec((rows, KV_WIDTH), lambda i: (i, K_OFF // KV_WIDTH)),
                  pl.BlockSpec((rows, KV_WIDTH), lambda i: (i, V_OFF // KV_WIDTH)),
                  pl.BlockSpec((rows, HEAD_DIM), lambda i: (position_block(i), 0)),
                  pl.BlockSpec((rows, HEAD_DIM), lambda i: (position_block(i), 0)),
                  pl.BlockSpec((1, HEAD_DIM), lambda i: (0, 0)),
                  pl.BlockSpec((1, HEAD_DIM), lambda i: (0, 0)),
                  pl.BlockSpec((PAIR_W, PAIR_W), lambda i: (0, 0))],
        out_specs=[pl.BlockSpec((rows, ATTN_WIDTH), lambda i: (i, 0)),
                   pl.BlockSpec((rows, KV_WIDTH), lambda i: (i, 0)),
                   pl.BlockSpec((N_KV_HEADS, 1, V_ROWS, rows),
                                lambda i: (0, i // per_key_block, 0, i % per_key_block))],
        out_shape=[jax.ShapeDtypeStruct((t, ATTN_WIDTH), BF16),
                   jax.ShapeDtypeStruct((t, KV_WIDTH), BF16),
                   jax.ShapeDtypeStruct((N_KV_HEADS, t // key_block, V_ROWS, key_block), BF16)],
        compiler_params=_params(("parallel",), 40),
        name="qkv_prep",
    )(proj, proj, proj, cos, sin, q_gain.reshape(1, HEAD_DIM), k_gain.reshape(1, HEAD_DIM),
      _rope_partner_matrix())


def _attn_kernel(q_ref, k_ref, vt_ref, *rest, bk, nk):
    o_ref, m_scr, acc_scr, s_scr, smax_scr = rest[-5:]
    m_scr[...] = jnp.full(m_scr.shape, -jnp.inf, F32)
    acc_scr[...] = jnp.zeros(acc_scr.shape, F32)

    def scores(kb, h, slot):
        s = lax.dot_general(kb, q_ref[:, h * HEAD_DIM:(h + 1) * HEAD_DIM], (((1,), (1,)), ((), ())),
                            preferred_element_type=F32)
        s_scr[slot] = s
        smax_scr[slot] = jnp.max(s, axis=0, keepdims=True)

    def key_block(j):
        return k_ref[pl.ds(pl.multiple_of(j * bk, bk), bk), :]

    scores(key_block(0), 0, 0)

    def body(j, carry):
        kb = key_block(j)
        kb_next = key_block(jnp.minimum(j + 1, nk - 1))
        vb = vt_ref[j]
        for h in range(Q_PER_KV):
            slot = h % 2
            if h + 1 < Q_PER_KV:
                scores(kb, h + 1, 1 - slot)
            else:
                scores(kb_next, 0, 1 - slot)
            m_prev = m_scr[h]
            m_new = jnp.maximum(m_prev, smax_scr[slot])
            alpha = jnp.exp2(m_prev - m_new)
            p = jnp.exp2(s_scr[slot] - m_new).astype(vb.dtype)
            acc_scr[h] = acc_scr[h] * alpha + jnp.dot(vb, p, preferred_element_type=F32)
            m_scr[h] = m_new
        return carry

    lax.fori_loop(0, nk, body, 0, unroll=max(1, min(ATTN_KEYS_PER_LOOP_TRIP // bk, nk // 2)))
    for h in range(Q_PER_KV):
        acc = acc_scr[h]
        o = acc[:HEAD_DIM, :] / acc[HEAD_DIM:HEAD_DIM + 1, :]
        o_ref[:, h * HEAD_DIM:(h + 1) * HEAD_DIM] = o.T.astype(o_ref.dtype)


def attention(q_rot, k_rot, vt, row0, batch, seq, bq=512):
    bk = vt.shape[-1]
    bq = min(bq, seq)
    assert row0 % seq == 0 and seq % bq == 0 and seq % bk == 0
    group_w = Q_PER_KV * HEAD_DIM
    nq = seq // bq
    nk = seq // bk
    in_specs = [
        pl.BlockSpec((bq, group_w), lambda b, h, i: (row0 // bq + b * nq + i, h)),
        pl.BlockSpec((seq, HEAD_DIM), lambda b, h, i: (row0 // seq + b, h)),
        pl.BlockSpec((None, nk, V_ROWS, bk), lambda b, h, i: (h, row0 // seq + b, 0, 0)),
    ]
    return pl.pallas_call(
        functools.partial(_attn_kernel, bk=bk, nk=nk),
        grid=(batch, N_KV_HEADS, nq),
        in_specs=in_specs,
        out_specs=pl.BlockSpec((bq, group_w), lambda b, h, i: (b * nq + i, h)),
        out_shape=jax.ShapeDtypeStruct((batch * seq, ATTN_WIDTH), BF16),
        scratch_shapes=[pltpu.VMEM((Q_PER_KV, 1, bq), F32),
                        pltpu.VMEM((Q_PER_KV, V_ROWS, bq), F32),
                        pltpu.VMEM((2, bk, bq), F32),
                        pltpu.VMEM((2, 1, bq), F32)],
        compiler_params=_params(("parallel", "parallel", "parallel"), 48),
        name="attention",
    )(q_rot, k_rot, vt)


def _convpool_kernel(ch_ref, cb_ref, cc_ref, pu_ref,
                     chp_ref, ccp_ref, pup_ref, chn_ref, ccn_ref, pun_ref,
                     cw_ref, pw_ref, ps_ref, o_ref, *, rows, segments):
    i = pl.program_id(0)
    start = i * rows
    pos0 = start
    seq_len = jnp.int32(0)
    has_prev = jnp.bool_(True)
    has_next = jnp.bool_(True)
    for seg_start, seg_len in segments:
        inside = (start >= seg_start) & (start < seg_start + seg_len)
        pos0 = jnp.where(inside, start - seg_start, pos0)
        seq_len = jnp.where(inside, seg_len, seq_len)
        has_prev = has_prev & (start != seg_start)
        has_next = has_next & (start + rows != seg_start + seg_len)
    prev_on = has_prev.astype(F32)
    next_on = has_next.astype(F32)

    def extended(cur_ref, prev_ref, next_ref):
        return jnp.concatenate([prev_ref[...].astype(F32) * prev_on,
                                cur_ref[...].astype(F32),
                                next_ref[...].astype(F32) * next_on], axis=0)

    ext_rows = rows + 2 * HALO

    def shifted(x, d):
        return pltpu.roll(x, d % ext_rows, axis=0)

    u = extended(cc_ref, ccp_ref, ccn_ref) * extended(ch_ref, chp_ref, chn_ref)
    cw = cw_ref[...]
    y = shifted(u, 1) * cw[0:1, :] + u * cw[1:2, :] + shifted(u, -1) * cw[2:3, :]
    o_ref[:, :CONV_WIDTH] = (cb_ref[...].astype(F32) * y[HALO:HALO + rows, :]).astype(o_ref.dtype)

    x = extended(pu_ref, pup_ref, pun_ref)
    pos = (pos0 + lax.broadcasted_iota(jnp.int32, (rows, 1), 0))
    for gi, w in enumerate(POOL_WINDOWS):
        sl = slice(gi * POOL_GROUP, (gi + 1) * POOL_GROUP)
        xg = x[:, sl]
        tot = xg + shifted(xg, 1)
        half = 1
        while 2 * half < w:
            tot = shifted(tot, half) + shifted(tot, -half)
            half *= 2
        hi = jnp.minimum(pos + w // 2, seq_len)
        lo = jnp.maximum(pos - w // 2, 0)
        cnt = (hi - lo).astype(F32)
        m = tot[HALO:HALO + rows, :] / cnt - xg[HALO:HALO + rows, :]
        yg = jnp.dot(m.astype(BF16), pw_ref[gi], preferred_element_type=F32)
        o_ref[:, CONV_WIDTH + gi * POOL_GROUP:CONV_WIDTH + (gi + 1) * POOL_GROUP] = (
            yg * ps_ref[:, sl]).astype(o_ref.dtype)


def conv_pool(proj, conv_w, pool_w, pool_scale, segments, rows=512):
    t = proj.shape[0]
    rows = min(rows, min(n for _, n in segments))
    assert all(s % rows == 0 and n % rows == 0 for s, n in segments)
    w = CONV_WIDTH
    per = rows // HALO
    last = t // HALO - 1

    def cur(off):
        return pl.BlockSpec((rows, w), lambda i: (i, off // w))

    def prev(off):
        return pl.BlockSpec((HALO, w), lambda i: (jnp.maximum(i * per - 1, 0), off // w))

    def nxt(off):
        return pl.BlockSpec((HALO, w), lambda i: (jnp.minimum((i + 1) * per, last), off // w))

    kernel = functools.partial(_convpool_kernel, rows=rows, segments=segments)
    return pl.pallas_call(
        kernel,
        grid=(t // rows,),
        in_specs=[cur(CH_OFF), cur(CB_OFF), cur(CC_OFF), cur(PU_OFF),
                  prev(CH_OFF), prev(CC_OFF), prev(PU_OFF),
                  nxt(CH_OFF), nxt(CC_OFF), nxt(PU_OFF),
                  pl.BlockSpec((3, w), lambda i: (0, 0)),
                  pl.BlockSpec((len(POOL_WINDOWS), POOL_GROUP, POOL_GROUP), lambda i: (0, 0, 0)),
                  pl.BlockSpec((1, POOL_WIDTH), lambda i: (0, 0))],
        out_specs=pl.BlockSpec((rows, CONV_WIDTH + POOL_WIDTH), lambda i: (i, 0)),
        out_shape=jax.ShapeDtypeStruct((t, CONV_WIDTH + POOL_WIDTH), BF16),
        compiler_params=_params(("parallel",), 48),
        name="conv_pool",
    )(proj, proj, proj, proj, proj, proj, proj, proj, proj, proj,
      conv_w, pool_w, pool_scale.reshape(1, POOL_WIDTH))


def _trunk(x_parts, segments, attn_calls, ln_mix_pre, ln_mix_post, ln_ffn_pre, ln_ffn_post, q_norm, k_norm,
           w_in, conv_w, pool_w, pool_scale, w_out, w_gate, w_up, w_down):
    total = sum(n for _, _, n in x_parts)
    cos, sin = _rope_tables(max(n for _, n in segments))
    key_block = min(ATTN_KEY_BLOCK, min(n for _, n in segments))
    out_parts = [(row0, n) for _, row0, n in x_parts]
    h = norm_cast(x_parts, ln_mix_pre[0], total)
    w_in_bf16 = cast_weight(w_in, 0)
    d_model = w_out.shape[2]
    for l in range(DEPTH):
        proj, (w_out_bf16, w_gate_bf16, w_up_bf16) = matmul(
            h, w_in_bf16, BF16,
            [(w_out, l, w_out.shape[1], d_model), (w_gate, l, d_model, D_FF_PAD), (w_up, l, d_model, D_FF_PAD)])
        q_rot, k_rot, vt = qkv_prep(proj, cos, sin, q_norm[l], k_norm[l], key_block, segments)
        a_parts = [(attention(q_rot, k_rot, vt, row0, batch, seq), row0, batch * seq)
                   for row0, batch, seq in attn_calls]
        cp_out = conv_pool(proj, conv_w[l], pool_w[l].astype(BF16), pool_scale[l], segments)
        mix = matmul_two_lhs(a_parts, cp_out, w_out_bf16)
        x, h = resnorm(mix, x_parts, ln_mix_post[l], ln_ffn_pre[l])
        x_parts = [(x, 0, total)]
        jobs = [(w_down, l, D_FF_PAD, d_model)]
        if l + 1 < DEPTH:
            jobs.append((w_in, l + 1, d_model, w_in.shape[2]))
        act, cast = gate_up(h, w_gate_bf16, w_up_bf16, jobs)
        w_down_bf16 = cast[0]
        w_in_bf16 = cast[1] if l + 1 < DEPTH else None
        f = matmul_split_k(act, w_down_bf16, bk=D_FF_PAD // 4)
        if l + 1 < DEPTH:
            x, h = resnorm(f, x_parts, ln_ffn_post[l], ln_mix_pre[l + 1])
            x_parts = [(x, 0, total)]
    return resnorm_last_to_parts(f, x, ln_ffn_post[DEPTH - 1], out_parts)


def kernel(x_prompt, x_sample, ln_mix_pre, ln_mix_post, ln_ffn_pre, ln_ffn_post, q_norm, k_norm,
           w_in, conv_w, pool_w, pool_scale, w_out, w_gate, w_up, w_down):
    pb, ps, d = x_prompt.shape
    sb, ss, _ = x_sample.shape
    x_parts = [(x_prompt.reshape(pb * ps, d), 0, pb * ps), (x_sample.reshape(sb * ss, d), pb * ps, sb * ss)]
    segments = tuple((b * ps, ps) for b in range(pb)) + tuple((pb * ps + b * ss, ss) for b in range(sb))
    attn_calls = ((0, pb, ps), (pb * ps, sb, ss))
    y_prompt, y_sample = _trunk(x_parts, segments, attn_calls, ln_mix_pre, ln_mix_post, ln_ffn_pre, ln_ffn_post,
                                q_norm, k_norm, w_in, conv_w, pool_w, pool_scale, w_out, w_gate, w_up, w_down)
    return (y_prompt.reshape(pb, ps, d), y_sample.reshape(sb, ss, d))
```

```python
import functools

import numpy as np
import jax
import jax.numpy as jnp
from jax import lax
from jax.experimental import pallas as pl
from jax.experimental.pallas import tpu as pltpu

D_MODEL = 4096
DEPTH = 2
GRID_W = 64
HEAD_DIM = 128
N_Q_HEADS = 16
N_KV_HEADS = 4
Q_PER_KV = N_Q_HEADS // N_KV_HEADS
ATTN_WIDTH = N_Q_HEADS * HEAD_DIM
KV_WIDTH = N_KV_HEADS * HEAD_DIM
CONV_WIDTH = D_MODEL // 4
POOL_WIDTH = D_MODEL // 4
POOL_WINDOWS = (2, 4, 8, 16)
POOL_GROUP = POOL_WIDTH // len(POOL_WINDOWS)
D_FF = 11008
D_FF_PAD = 11264
ROPE_HALF = HEAD_DIM // 2
ROPE_THETA = 10000.0
EPS = 1e-6

Q_OFF = 0
K_OFF = ATTN_WIDTH
V_OFF = K_OFF + KV_WIDTH
CH_OFF = V_OFF + KV_WIDTH
CB_OFF = CH_OFF + CONV_WIDTH
CC_OFF = CB_OFF + CONV_WIDTH
PU_OFF = CC_OFF + CONV_WIDTH

HALO = 16
V7X_VMEM_BYTES = 64 * 1024 * 1024
MXU_DEPTH = 256

F32 = jnp.float32
BF16 = jnp.bfloat16


def _params(semantics, vmem_mb):
    assert vmem_mb * 1024 * 1024 < V7X_VMEM_BYTES
    return pltpu.CompilerParams(dimension_semantics=semantics,
                                vmem_limit_bytes=vmem_mb * 1024 * 1024)


def _rms(x):
    return x * lax.rsqrt(jnp.mean(x * x, axis=-1, keepdims=True) + EPS)


def _part_specs(parts, block_rows, width, block_of):
    specs = []
    for _, row0, n in parts:
        assert row0 % block_rows == 0 and n % block_rows == 0
        first, count = row0 // block_rows, n // block_rows
        specs.append(pl.BlockSpec(
            (block_rows, width), lambda *g, first=first, count=count: (jnp.clip(block_of(*g) - first, 0, count - 1), 0)))
    return specs


def _part_select(parts, block_rows, block, refs):
    value = refs[0][...]
    for (_, row0, _), ref in zip(parts[1:], refs[1:]):
        value = jnp.where(block >= row0 // block_rows, ref[...], value)
    return value


def _rowwise_call(body, name, slab_row0, n_rows, sources, gains, out_dtypes, rows=256):
    d = sources[0][0][0].shape[1]
    first_block = slab_row0 // rows
    assert slab_row0 % rows == 0 and n_rows % rows == 0
    in_specs, args = [], []
    for parts in sources:
        in_specs += _part_specs(parts, rows, d, lambda i: first_block + i)
        args += [a for a, _, _ in parts]
    in_specs += [pl.BlockSpec((1, d), lambda i: (0, 0)) for _ in gains]
    args += [g.reshape(1, d) for g in gains]

    def kernel(*refs):
        block = first_block + pl.program_id(0)
        values, at = [], 0
        for parts in sources:
            values.append(_part_select(parts, rows, block, refs[at:at + len(parts)]))
            at += len(parts)
        body(*values, *refs[at:])

    return pl.pallas_call(
        kernel,
        grid=(n_rows // rows,),
        in_specs=in_specs,
        out_specs=[pl.BlockSpec((rows, d), lambda i: (i, 0)) for _ in out_dtypes],
        out_shape=[jax.ShapeDtypeStruct((n_rows, d), dt) for dt in out_dtypes],
        compiler_params=_params(("parallel",), 48),
        name=name,
    )(*args)


def _norm_cast_body(x, g_ref, h_ref):
    h_ref[...] = (_rms(x) * g_ref[...]).astype(h_ref.dtype)


def _resnorm_body(f, x, g_post_ref, g_next_ref, xo_ref, h_ref):
    xn = x + _rms(f.astype(F32)) * g_post_ref[...]
    xo_ref[...] = xn
    h_ref[...] = (_rms(xn) * g_next_ref[...]).astype(h_ref.dtype)


def _resnorm_last_body(f, x, g_post_ref, xo_ref):
    xo_ref[...] = x + _rms(f.astype(F32)) * g_post_ref[...]


def norm_cast(x_parts, g, total):
    return _rowwise_call(_norm_cast_body, "norm_cast", 0, total, [x_parts], [g], [BF16])[0]


def resnorm(f, x_parts, g_post, g_next):
    total = f.shape[0]
    return _rowwise_call(_resnorm_body, "resnorm", 0, total, [[(f, 0, total)], x_parts], [g_post, g_next],
                         [F32, BF16])


def resnorm_last_to_parts(f, x, g_post, parts):
    total = f.shape[0]
    return [_rowwise_call(_resnorm_last_body, "resnorm_last", row0, n, [[(f, 0, total)], [(x, 0, total)]],
                          [g_post], [F32])[0] for row0, n in parts]


CAST_JOB_ROW_CHOICES = (32, 64, 128, 256, 512, 1024)


class CastJob:
    def __init__(self, w, layer, out_rows, out_cols, n_steps):
        _, n_rows, n_cols = w.shape
        fits = [r for r in CAST_JOB_ROW_CHOICES
                if n_rows % r == 0 and out_rows % r == 0 and out_rows // r <= n_steps]
        self.feasible = bool(fits)
        if not fits:
            return
        self.w, self.layer, self.rows = w, layer, fits[0]
        self.n_cols, self.out_rows, self.out_cols = n_cols, out_rows, out_cols
        self.in_blocks, self.out_blocks = n_rows // self.rows, out_rows // self.rows

    def specs(self, step_of):
        in_spec = pl.BlockSpec((None, self.rows, self.n_cols),
                               lambda *g: (self.layer, jnp.minimum(step_of(*g), self.in_blocks - 1), 0))
        out_spec = pl.BlockSpec((self.rows, self.out_cols),
                                lambda *g: (jnp.minimum(step_of(*g), self.out_blocks - 1), 0))
        return in_spec, out_spec, jax.ShapeDtypeStruct((self.out_rows, self.out_cols), BF16)

    def run(self, step, w_ref, o_ref):
        @pl.when(step < self.in_blocks)
        def _():
            o_ref[:, :self.n_cols] = w_ref[...].astype(o_ref.dtype)
            if self.out_cols > self.n_cols:
                o_ref[:, self.n_cols:] = jnp.zeros((self.rows, self.out_cols - self.n_cols), o_ref.dtype)

        if self.out_blocks > self.in_blocks:
            @pl.when((step >= self.in_blocks) & (step < self.out_blocks))
            def _():
                o_ref[...] = jnp.zeros(o_ref.shape, o_ref.dtype)


def _cast_weight_kernel(w_ref, o_ref, *, valid_rows, valid_cols):
    br, bc = o_ref.shape
    r = pl.program_id(0) * br + lax.broadcasted_iota(jnp.int32, (br, bc), 0)
    c = pl.program_id(1) * bc + lax.broadcasted_iota(jnp.int32, (br, bc), 1)
    o_ref[...] = jnp.where((r < valid_rows) & (c < valid_cols), w_ref[...], 0.0).astype(o_ref.dtype)


def cast_weight(w, layer, out_rows=None, out_cols=None, br=512, bc=1024):
    _, rows, cols = w.shape
    out_rows = out_rows or rows
    out_cols = out_cols or cols
    return pl.pallas_call(
        functools.partial(_cast_weight_kernel, valid_rows=rows, valid_cols=cols),
        grid=(out_rows // br, out_cols // bc),
        in_specs=[pl.BlockSpec((None, br, bc), lambda i, j: (layer, i, j))],
        out_specs=pl.BlockSpec((br, bc), lambda i, j: (i, j)),
        out_shape=jax.ShapeDtypeStruct((out_rows, out_cols), BF16),
        compiler_params=_params(("parallel", "parallel"), 32),
        name="cast_weight",
    )(w)


def _with_cast_jobs(body, n_in, n_out, jobs, n_j):
    n_jobs = len(jobs)

    def kernel(*refs):
        ins, job_ins = refs[:n_in], refs[n_in:n_in + n_jobs]
        outs = refs[n_in + n_jobs:n_in + n_jobs + n_out]
        job_outs = refs[n_in + n_jobs + n_out:n_in + n_jobs + n_out + n_jobs]
        body(*ins, *outs, *refs[n_in + n_jobs + n_out + n_jobs:])
        step = pl.program_id(0) * n_j + pl.program_id(1)
        for job, w_ref, o_ref in zip(jobs, job_ins, job_outs):
            job.run(step, w_ref, o_ref)

    return kernel


def _matmul_call(body, name, ins, in_specs, out_spec, out_shape, grid, job_requests, vmem_mb):
    n_i, n_j = grid
    jobs = [CastJob(w, layer, r, c, n_i * n_j) for w, layer, r, c in job_requests]
    riding = [job for job in jobs if job.feasible]
    job_specs = [job.specs(lambda i, j: i * n_j + j) for job in riding]
    outs = pl.pallas_call(
        _with_cast_jobs(body, len(ins), 1, riding, n_j),
        grid=grid,
        in_specs=list(in_specs) + [s[0] for s in job_specs],
        out_specs=[out_spec] + [s[1] for s in job_specs],
        out_shape=[out_shape] + [s[2] for s in job_specs],
        compiler_params=_params(("arbitrary", "arbitrary"), vmem_mb),
        name=name,
    )(*ins, *[job.w for job in riding])
    cast = iter(outs[1:])
    weights = [next(cast) if job.feasible else cast_weight(w, layer, r, c)
               for job, (w, layer, r, c) in zip(jobs, job_requests)]
    return outs[0], weights


def _mm_kernel(a_ref, b_ref, o_ref):
    o_ref[...] = jnp.dot(a_ref[...], b_ref[...], preferred_element_type=F32).astype(o_ref.dtype)


def matmul(a, b, out_dtype, job_requests=(), bm=1024, bn=1024):
    m, k = a.shape
    _, n = b.shape
    bm = min(bm, m)
    return _matmul_call(
        _mm_kernel, "matmul_full_k", [a, b],
        [pl.BlockSpec((bm, k), lambda i, j: (i, 0)), pl.BlockSpec((k, bn), lambda i, j: (0, j))],
        pl.BlockSpec((bm, bn), lambda i, j: (i, j)), jax.ShapeDtypeStruct((m, n), out_dtype),
        (m // bm, n // bn), job_requests, 56)


def matmul_two_lhs(a1_parts, a2, b, bm=1024, bn=1024):
    m, k2 = a2.shape
    k1 = a1_parts[0][0].shape[1]
    assert k1 == k2 and b.shape[0] == k1 + k2
    n = b.shape[1]
    bm = min([bm] + [rows for _, _, rows in a1_parts])
    n_parts = len(a1_parts)
    bounds = [row0 // bm for _, row0, _ in a1_parts] + [m // bm]

    def kernel(*refs):
        a2_ref, b1_ref, b2_ref, o_ref = refs[n_parts:]
        i = pl.program_id(0)
        for p in range(n_parts):
            @pl.when((i >= bounds[p]) & (i < bounds[p + 1]))
            def _(a1_ref=refs[p]):
                o_ref[...] = (jnp.dot(a1_ref[...], b1_ref[...], preferred_element_type=F32)
                              + jnp.dot(a2_ref[...], b2_ref[...], preferred_element_type=F32)).astype(o_ref.dtype)

    return pl.pallas_call(
        kernel,
        grid=(m // bm, n // bn),
        in_specs=_part_specs(a1_parts, bm, k1, lambda i, j: i) + [
            pl.BlockSpec((bm, k2), lambda i, j: (i, 0)),
            pl.BlockSpec((k1, bn), lambda i, j: (0, j)),
            pl.BlockSpec((k2, bn), lambda i, j: (1, j))],
        out_specs=pl.BlockSpec((bm, bn), lambda i, j: (i, j)),
        out_shape=jax.ShapeDtypeStruct((m, n), BF16),
        compiler_params=_params(("parallel", "parallel"), 56),
        name="matmul_two_lhs",
    )(*[a for a, _, _ in a1_parts], a2, b, b)


def _mm_acc_kernel(a_ref, b_ref, o_ref, acc_ref, *, nk, last_live):
    k = pl.program_id(2)

    @pl.when(k == 0)
    def _():
        acc_ref[...] = jnp.dot(a_ref[...], b_ref[...], preferred_element_type=F32)

    @pl.when((k > 0) & (k < nk - 1))
    def _():
        acc_ref[...] += jnp.dot(a_ref[...], b_ref[...], preferred_element_type=F32)

    @pl.when(k == nk - 1)
    def _():
        o_ref[...] = (acc_ref[...] + jnp.dot(a_ref[:, :last_live], b_ref[:last_live, :],
                                             preferred_element_type=F32)).astype(o_ref.dtype)


def matmul_split_k(a, b, bk, live_k, bm=1024, bn=1024):
    m, k = a.shape
    _, n = b.shape
    bm = min(bm, m)
    nk = k // bk
    last_live = live_k - (nk - 1) * bk
    assert nk >= 2 and 0 < last_live <= bk and last_live % MXU_DEPTH == 0
    return pl.pallas_call(
        functools.partial(_mm_acc_kernel, nk=nk, last_live=last_live),
        grid=(m // bm, n // bn, nk),
        in_specs=[pl.BlockSpec((bm, bk), lambda i, j, kk: (i, kk)),
                  pl.BlockSpec((bk, bn), lambda i, j, kk: (kk, j))],
        out_specs=pl.BlockSpec((bm, bn), lambda i, j, kk: (i, j)),
        out_shape=jax.ShapeDtypeStruct((m, n), BF16),
        scratch_shapes=[pltpu.VMEM((bm, bn), F32)],
        compiler_params=_params(("parallel", "parallel", "arbitrary"), 48),
        name="matmul_split_k",
    )(a, b)


def _gateup_kernel(h_ref, wg_ref, wu_ref, o_ref, *, n_j, last_live):
    def tile(width):
        h = h_ref[...]
        g = jnp.dot(h, wg_ref[:, :width], preferred_element_type=F32)
        u = jnp.dot(h, wu_ref[:, :width], preferred_element_type=F32)
        o_ref[:, :width] = (g * jax.nn.sigmoid(g) * u).astype(o_ref.dtype)

    j = pl.program_id(1)
    bn = o_ref.shape[1]
    if last_live == bn:
        tile(bn)
        return

    @pl.when(j < n_j - 1)
    def _():
        tile(bn)

    @pl.when(j == n_j - 1)
    def _():
        tile(last_live)
        o_ref[:, last_live:] = jnp.zeros((o_ref.shape[0], bn - last_live), o_ref.dtype)


def gate_up(h, wg, wu, live_cols, job_requests=(), bm=1024, bn=512):
    m, k = h.shape
    _, n = wg.shape
    bm = min(bm, m)
    n_j = n // bn
    last_live = live_cols - (n_j - 1) * bn
    assert 0 < last_live <= bn and last_live % MXU_DEPTH == 0
    w_spec = pl.BlockSpec((k, bn), lambda i, j: (0, j))
    return _matmul_call(
        functools.partial(_gateup_kernel, n_j=n_j, last_live=last_live), "gate_up", [h, wg, wu],
        [pl.BlockSpec((bm, k), lambda i, j: (i, 0)), w_spec, w_spec],
        pl.BlockSpec((bm, bn), lambda i, j: (i, j)), jax.ShapeDtypeStruct((m, n), BF16),
        (m // bm, n // bn), job_requests, 56)


ATTN_KEY_BLOCK = 1024
ATTN_KEYS_PER_LOOP_TRIP = 4096
V_ROWS = HEAD_DIM + 16
LOG2E = 1.4426950408889634


def _rope_tables(max_len):
    pos = np.arange(max_len)
    r = (pos // GRID_W).astype(np.float32)
    c = (pos % GRID_W).astype(np.float32)
    inv = (1.0 / (ROPE_THETA ** (np.arange(0, ROPE_HALF, 2, dtype=np.float32) / ROPE_HALF))).astype(np.float32)
    ang_r = r[:, None] * inv
    ang_c = c[:, None] * inv
    cos = np.concatenate([np.cos(ang_r), np.cos(ang_r), np.cos(ang_c), np.cos(ang_c)], axis=-1)
    sin = np.concatenate([-np.sin(ang_r), np.sin(ang_r), -np.sin(ang_c), np.sin(ang_c)], axis=-1)
    return jnp.asarray(cos, F32), jnp.asarray(sin, F32)


PAIR_W = 2 * HEAD_DIM


def _rope_partner_matrix():
    idx = np.arange(PAIR_W)
    perm = np.zeros((PAIR_W, PAIR_W), np.float32)
    perm[idx ^ (ROPE_HALF // 2), idx] = 1.0
    return jnp.asarray(perm, BF16)


def _norm_rope_pair(x, gain, cos, sin, perm):
    xn = jnp.concatenate([_rms(x[:, :HEAD_DIM]), _rms(x[:, HEAD_DIM:])], axis=1) * gain
    hi = xn.astype(BF16)
    lo = (xn - hi.astype(F32)).astype(BF16)
    partner = (jnp.dot(hi, perm, preferred_element_type=F32) + jnp.dot(lo, perm, preferred_element_type=F32))
    return xn * cos + partner * sin


def _qkv_prep_kernel(q_ref, k_ref, v_ref, cos_ref, sin_ref, qg_ref, kg_ref, perm_ref,
                     qo_ref, ko_ref, vt_ref):
    rows = cos_ref.shape[0]
    cos = jnp.concatenate([cos_ref[...]] * 2, axis=1)
    sin = jnp.concatenate([sin_ref[...]] * 2, axis=1)
    perm = perm_ref[...]
    q_scale = HEAD_DIM ** -0.5 * LOG2E
    qg = jnp.concatenate([qg_ref[...]] * 2, axis=1)
    kg = jnp.concatenate([kg_ref[...]] * 2, axis=1)
    for p in range(N_Q_HEADS // 2):
        sl = slice(p * PAIR_W, (p + 1) * PAIR_W)
        y = _norm_rope_pair(q_ref[:, sl].astype(F32), qg, cos, sin, perm) * q_scale
        qo_ref[:, sl] = y.astype(qo_ref.dtype)
    for p in range(N_KV_HEADS // 2):
        sl = slice(p * PAIR_W, (p + 1) * PAIR_W)
        y = _norm_rope_pair(k_ref[:, sl].astype(F32), kg, cos, sin, perm)
        ko_ref[:, sl] = y.astype(ko_ref.dtype)
    for h in range(N_KV_HEADS):
        sl = slice(h * HEAD_DIM, (h + 1) * HEAD_DIM)
        vt_ref[h, 0, :HEAD_DIM, :] = v_ref[:, sl].astype(F32).T.astype(vt_ref.dtype)
        vt_ref[h, 0, HEAD_DIM:, :] = jnp.ones((V_ROWS - HEAD_DIM, rows), vt_ref.dtype)


def qkv_prep(proj, cos, sin, q_gain, k_gain, key_block, segments, rows=512):
    t = proj.shape[0]
    rows = min(rows, key_block)
    per_key_block = key_block // rows
    assert all(s0 % rows == 0 for s0, _ in segments)

    def position_block(i):
        block = i
        for s0, _ in segments:
            block = jnp.where(i * rows >= s0, i - s0 // rows, block)
        return block

    return pl.pallas_call(
        _qkv_prep_kernel,
        grid=(t // rows,),
        in_specs=[pl.BlockSpec((rows, ATTN_WIDTH), lambda i: (i, Q_OFF // ATTN_WIDTH)),
                  pl.BlockSpec((rows, KV_WIDTH), lambda i: (i, K_OFF // KV_WIDTH)),
                  pl.BlockSpec((rows, KV_WIDTH), lambda i: (i, V_OFF // KV_WIDTH)),
                  pl.BlockSpec((rows, HEAD_DIM), lambda i: (position_block(i), 0)),
                  pl.BlockSpec((rows, HEAD_DIM), lambda i: (position_block(i), 0)),
                  pl.BlockSpec((1, HEAD_DIM), lambda i: (0, 0)),
                  pl.BlockSpec((1, HEAD_DIM), lambda i: (0, 0)),
                  pl.BlockSpec((PAIR_W, PAIR_W), lambda i: (0, 0))],
        out_specs=[pl.BlockSpec((rows, ATTN_WIDTH), lambda i: (i, 0)),
                   pl.BlockSpec((rows, KV_WIDTH), lambda i: (i, 0)),
                   pl.BlockSpec((N_KV_HEADS, 1, V_ROWS, rows),
                                lambda i: (0, i // per_key_block, 0, i % per_key_block))],
        out_shape=[jax.ShapeDtypeStruct((t, ATTN_WIDTH), BF16),
                   jax.ShapeDtypeStruct((t, KV_WIDTH), BF16),
                   jax.ShapeDtypeStruct((N_KV_HEADS, t // key_block, V_ROWS, key_block), BF16)],
        compiler_params=_params(("parallel",), 40),
        name="qkv_prep",
    )(proj, proj, proj, cos, sin, q_gain.reshape(1, HEAD_DIM), k_gain.reshape(1, HEAD_DIM),
      _rope_partner_matrix())


def _attn_kernel(q_ref, k_ref, vt_ref, *rest, bk, nk):
    o_ref, m_scr, acc_scr, s_scr, smax_scr = rest[-5:]
    m_scr[...] = jnp.full(m_scr.shape, -jnp.inf, F32)
    acc_scr[...] = jnp.zeros(acc_scr.shape, F32)

    def scores(kb, h, slot):
        s = lax.dot_general(kb, q_ref[:, h * HEAD_DIM:(h + 1) * HEAD_DIM], (((1,), (1,)), ((), ())),
                            preferred_element_type=F32)
        s_scr[slot] = s
        smax_scr[slot] = jnp.max(s, axis=0, keepdims=True)

    def key_block(j):
        return k_ref[pl.ds(pl.multiple_of(j * bk, bk), bk), :]

    scores(key_block(0), 0, 0)

    def body(j, carry):
        kb = key_block(j)
        kb_next = key_block(jnp.minimum(j + 1, nk - 1))
        vb = vt_ref[j]
        for h in range(Q_PER_KV):
            slot = h % 2
            if h + 1 < Q_PER_KV:
                scores(kb, h + 1, 1 - slot)
            else:
                scores(kb_next, 0, 1 - slot)
            m_prev = m_scr[h]
            m_new = jnp.maximum(m_prev, smax_scr[slot])
            alpha = jnp.exp2(m_prev - m_new)
            p = jnp.exp2(s_scr[slot] - m_new).astype(vb.dtype)
            acc_scr[h] = acc_scr[h] * alpha + jnp.dot(vb, p, preferred_element_type=F32)
            m_scr[h] = m_new
        return carry

    lax.fori_loop(0, nk, body, 0, unroll=max(1, min(ATTN_KEYS_PER_LOOP_TRIP // bk, nk // 2)))
    for h in range(Q_PER_KV):
        acc = acc_scr[h]
        o = acc[:HEAD_DIM, :] / acc[HEAD_DIM:HEAD_DIM + 1, :]
        o_ref[:, h * HEAD_DIM:(h + 1) * HEAD_DIM] = o.T.astype(o_ref.dtype)


def attention(q_rot, k_rot, vt, row0, batch, seq, bq=512):
    bk = vt.shape[-1]
    bq = min(bq, seq)
    assert row0 % seq == 0 and seq % bq == 0 and seq % bk == 0
    group_w = Q_PER_KV * HEAD_DIM
    nq = seq // bq
    nk = seq // bk
    in_specs = [
        pl.BlockSpec((bq, group_w), lambda b, h, i: (row0 // bq + b * nq + i, h)),
        pl.BlockSpec((seq, HEAD_DIM), lambda b, h, i: (row0 // seq + b, h)),
        pl.BlockSpec((None, nk, V_ROWS, bk), lambda b, h, i: (h, row0 // seq + b, 0, 0)),
    ]
    return pl.pallas_call(
        functools.partial(_attn_kernel, bk=bk, nk=nk),
        grid=(batch, N_KV_HEADS, nq),
        in_specs=in_specs,
        out_specs=pl.BlockSpec((bq, group_w), lambda b, h, i: (b * nq + i, h)),
        out_shape=jax.ShapeDtypeStruct((batch * seq, ATTN_WIDTH), BF16),
        scratch_shapes=[pltpu.VMEM((Q_PER_KV, 1, bq), F32),
                        pltpu.VMEM((Q_PER_KV, V_ROWS, bq), F32),
                        pltpu.VMEM((2, bk, bq), F32),
                        pltpu.VMEM((2, 1, bq), F32)],
        compiler_params=_params(("parallel", "parallel", "parallel"), 48),
        name="attention",
    )(q_rot, k_rot, vt)


def _convpool_kernel(ch_ref, cb_ref, cc_ref, pu_ref,
                     chp_ref, ccp_ref, pup_ref, chn_ref, ccn_ref, pun_ref,
                     cw_ref, pw_ref, ps_ref, o_ref, *, rows, segments):
    i = pl.program_id(0)
    start = i * rows
    pos0 = start
    seq_len = jnp.int32(0)
    has_prev = jnp.bool_(True)
    has_next = jnp.bool_(True)
    for seg_start, seg_len in segments:
        inside = (start >= seg_start) & (start < seg_start + seg_len)
        pos0 = jnp.where(inside, start - seg_start, pos0)
        seq_len = jnp.where(inside, seg_len, seq_len)
        has_prev = has_prev & (start != seg_start)
        has_next = has_next & (start + rows != seg_start + seg_len)
    prev_on = has_prev.astype(F32)
    next_on = has_next.astype(F32)

    def extended(cur_ref, prev_ref, next_ref):
        return jnp.concatenate([prev_ref[...].astype(F32) * prev_on,
                                cur_ref[...].astype(F32),
                                next_ref[...].astype(F32) * next_on], axis=0)

    ext_rows = rows + 2 * HALO

    def shifted(x, d):
        return pltpu.roll(x, d % ext_rows, axis=0)

    u = extended(cc_ref, ccp_ref, ccn_ref) * extended(ch_ref, chp_ref, chn_ref)
    cw = cw_ref[...]
    y = shifted(u, 1) * cw[0:1, :] + u * cw[1:2, :] + shifted(u, -1) * cw[2:3, :]
    o_ref[:, :CONV_WIDTH] = (cb_ref[...].astype(F32) * y[HALO:HALO + rows, :]).astype(o_ref.dtype)

    x = extended(pu_ref, pup_ref, pun_ref)
    pos = (pos0 + lax.broadcasted_iota(jnp.int32, (rows, 1), 0))
    for gi, w in enumerate(POOL_WINDOWS):
        sl = slice(gi * POOL_GROUP, (gi + 1) * POOL_GROUP)
        xg = x[:, sl]
        tot = xg + shifted(xg, 1)
        half = 1
        while 2 * half < w:
            tot = shifted(tot, half) + shifted(tot, -half)
            half *= 2
        hi = jnp.minimum(pos + w // 2, seq_len)
        lo = jnp.maximum(pos - w // 2, 0)
        cnt = (hi - lo).astype(F32)
        m = tot[HALO:HALO + rows, :] / cnt - xg[HALO:HALO + rows, :]
        yg = jnp.dot(m.astype(BF16), pw_ref[gi], preferred_element_type=F32)
        o_ref[:, CONV_WIDTH + gi * POOL_GROUP:CONV_WIDTH + (gi + 1) * POOL_GROUP] = (
            yg * ps_ref[:, sl]).astype(o_ref.dtype)


def conv_pool(proj, conv_w, pool_w, pool_scale, segments, rows=512):
    t = proj.shape[0]
    rows = min(rows, min(n for _, n in segments))
    assert all(s % rows == 0 and n % rows == 0 for s, n in segments)
    w = CONV_WIDTH
    per = rows // HALO
    last = t // HALO - 1

    def cur(off):
        return pl.BlockSpec((rows, w), lambda i: (i, off // w))

    def prev(off):
        return pl.BlockSpec((HALO, w), lambda i: (jnp.maximum(i * per - 1, 0), off // w))

    def nxt(off):
        return pl.BlockSpec((HALO, w), lambda i: (jnp.minimum((i + 1) * per, last), off // w))

    kernel = functools.partial(_convpool_kernel, rows=rows, segments=segments)
    return pl.pallas_call(
        kernel,
        grid=(t // rows,),
        in_specs=[cur(CH_OFF), cur(CB_OFF), cur(CC_OFF), cur(PU_OFF),
                  prev(CH_OFF), prev(CC_OFF), prev(PU_OFF),
                  nxt(CH_OFF), nxt(CC_OFF), nxt(PU_OFF),
                  pl.BlockSpec((3, w), lambda i: (0, 0)),
                  pl.BlockSpec((len(POOL_WINDOWS), POOL_GROUP, POOL_GROUP), lambda i: (0, 0, 0)),
                  pl.BlockSpec((1, POOL_WIDTH), lambda i: (0, 0))],
        out_specs=pl.BlockSpec((rows, CONV_WIDTH + POOL_WIDTH), lambda i: (i, 0)),
        out_shape=jax.ShapeDtypeStruct((t, CONV_WIDTH + POOL_WIDTH), BF16),
        compiler_params=_params(("parallel",), 48),
        name="conv_pool",
    )(proj, proj, proj, proj, proj, proj, proj, proj, proj, proj,
      conv_w, pool_w, pool_scale.reshape(1, POOL_WIDTH))


def _trunk(x_parts, segments, attn_calls, ln_mix_pre, ln_mix_post, ln_ffn_pre, ln_ffn_post, q_norm, k_norm,
           w_in, conv_w, pool_w, pool_scale, w_out, w_gate, w_up, w_down):
    total = sum(n for _, _, n in x_parts)
    cos, sin = _rope_tables(max(n for _, n in segments))
    key_block = min(ATTN_KEY_BLOCK, min(n for _, n in segments))
    out_parts = [(row0, n) for _, row0, n in x_parts]
    h = norm_cast(x_parts, ln_mix_pre[0], total)
    w_in_bf16 = cast_weight(w_in, 0)
    d_model = w_out.shape[2]
    for l in range(DEPTH):
        proj, (w_out_bf16, w_gate_bf16, w_up_bf16) = matmul(
            h, w_in_bf16, BF16,
            [(w_out, l, w_out.shape[1], d_model), (w_gate, l, d_model, D_FF_PAD), (w_up, l, d_model, D_FF_PAD)])
        q_rot, k_rot, vt = qkv_prep(proj, cos, sin, q_norm[l], k_norm[l], key_block, segments)
        a_parts = [(attention(q_rot, k_rot, vt, row0, batch, seq), row0, batch * seq)
                   for row0, batch, seq in attn_calls]
        cp_out = conv_pool(proj, conv_w[l], pool_w[l].astype(BF16), pool_scale[l], segments)
        mix = matmul_two_lhs(a_parts, cp_out, w_out_bf16)
        x, h = resnorm(mix, x_parts, ln_mix_post[l], ln_ffn_pre[l])
        x_parts = [(x, 0, total)]
        jobs = [(w_down, l, D_FF_PAD, d_model)]
        if l + 1 < DEPTH:
            jobs.append((w_in, l + 1, d_model, w_in.shape[2]))
        act, cast = gate_up(h, w_gate_bf16, w_up_bf16, w_gate.shape[2], jobs)
        w_down_bf16 = cast[0]
        w_in_bf16 = cast[1] if l + 1 < DEPTH else None
        f = matmul_split_k(act, w_down_bf16, D_FF_PAD // 4, w_down.shape[1])
        if l + 1 < DEPTH:
            x, h = resnorm(f, x_parts, ln_ffn_post[l], ln_mix_pre[l + 1])
            x_parts = [(x, 0, total)]
    return resnorm_last_to_parts(f, x, ln_ffn_post[DEPTH - 1], out_parts)


def kernel(x_prompt, x_sample, ln_mix_pre, ln_mix_post, ln_ffn_pre, ln_ffn_post, q_norm, k_norm,
           w_in, conv_w, pool_w, pool_scale, w_out, w_gate, w_up, w_down):
    pb, ps, d = x_prompt.shape
    sb, ss, _ = x_sample.shape
    x_parts = [(x_prompt.reshape(pb * ps, d), 0, pb * ps), (x_sample.reshape(sb * ss, d), pb * ps, sb * ss)]
    segments = tuple((b * ps, ps) for b in range(pb)) + tuple((pb * ps + b * ss, ss) for b in range(sb))
    attn_calls = ((0, pb, ps), (pb * ps, sb, ss))
    y_prompt, y_sample = _trunk(x_parts, segments, attn_calls, ln_mix_pre, ln_mix_post, ln_ffn_pre, ln_ffn_post,
                                q_norm, k_norm, w_in, conv_w, pool_w, pool_scale, w_out, w_gate, w_up, w_down)
    return (y_prompt.reshape(pb, ps, d), y_sample.reshape(sb, ss, d))
```

```python
import functools

import numpy as np
import jax
import jax.numpy as jnp
from jax import lax
from jax.experimental import pallas as pl
from jax.experimental.pallas import tpu as pltpu

D_MODEL = 4096
DEPTH = 2
GRID_W = 64
HEAD_DIM = 128
N_Q_HEADS = 16
N_KV_HEADS = 4
Q_PER_KV = N_Q_HEADS // N_KV_HEADS
ATTN_WIDTH = N_Q_HEADS * HEAD_DIM
KV_WIDTH = N_KV_HEADS * HEAD_DIM
CONV_WIDTH = D_MODEL // 4
POOL_WIDTH = D_MODEL // 4
POOL_WINDOWS = (2, 4, 8, 16)
POOL_GROUP = POOL_WIDTH // len(POOL_WINDOWS)
D_FF = 11008
D_FF_PAD = 11264
ROPE_HALF = HEAD_DIM // 2
ROPE_THETA = 10000.0
EPS = 1e-6

Q_OFF = 0
K_OFF = ATTN_WIDTH
V_OFF = K_OFF + KV_WIDTH
CH_OFF = V_OFF + KV_WIDTH
CB_OFF = CH_OFF + CONV_WIDTH
CC_OFF = CB_OFF + CONV_WIDTH
PU_OFF = CC_OFF + CONV_WIDTH

HALO = 16
V7X_VMEM_BYTES = 64 * 1024 * 1024
MXU_DEPTH = 256

F32 = jnp.float32
BF16 = jnp.bfloat16


def _params(semantics, vmem_mb):
    assert vmem_mb * 1024 * 1024 < V7X_VMEM_BYTES
    return pltpu.CompilerParams(dimension_semantics=semantics,
                                vmem_limit_bytes=vmem_mb * 1024 * 1024)


def _rms(x):
    return x * lax.rsqrt(jnp.mean(x * x, axis=-1, keepdims=True) + EPS)


def _part_specs(parts, block_rows, width, block_of):
    specs = []
    for _, row0, n in parts:
        assert row0 % block_rows == 0 and n % block_rows == 0
        first, count = row0 // block_rows, n // block_rows
        specs.append(pl.BlockSpec(
            (block_rows, width), lambda *g, first=first, count=count: (jnp.clip(block_of(*g) - first, 0, count - 1), 0)))
    return specs


def _part_select(parts, block_rows, block, refs):
    value = refs[0][...]
    for (_, row0, _), ref in zip(parts[1:], refs[1:]):
        value = jnp.where(block >= row0 // block_rows, ref[...], value)
    return value


def _rowwise_call(body, name, slab_row0, n_rows, sources, gains, out_dtypes, rows=256):
    d = sources[0][0][0].shape[1]
    first_block = slab_row0 // rows
    assert slab_row0 % rows == 0 and n_rows % rows == 0
    in_specs, args = [], []
    for parts in sources:
        in_specs += _part_specs(parts, rows, d, lambda i: first_block + i)
        args += [a for a, _, _ in parts]
    in_specs += [pl.BlockSpec((1, d), lambda i: (0, 0)) for _ in gains]
    args += [g.reshape(1, d) for g in gains]

    def kernel(*refs):
        block = first_block + pl.program_id(0)
        values, at = [], 0
        for parts in sources:
            values.append(_part_select(parts, rows, block, refs[at:at + len(parts)]))
            at += len(parts)
        body(*values, *refs[at:])

    return pl.pallas_call(
        kernel,
        grid=(n_rows // rows,),
        in_specs=in_specs,
        out_specs=[pl.BlockSpec((rows, d), lambda i: (i, 0)) for _ in out_dtypes],
        out_shape=[jax.ShapeDtypeStruct((n_rows, d), dt) for dt in out_dtypes],
        compiler_params=_params(("parallel",), 48),
        name=name,
    )(*args)


def _norm_cast_body(x, g_ref, h_ref):
    h_ref[...] = (_rms(x) * g_ref[...]).astype(h_ref.dtype)


def _resnorm_body(f, x, g_post_ref, g_next_ref, xo_ref, h_ref):
    xn = x + _rms(f.astype(F32)) * g_post_ref[...]
    xo_ref[...] = xn
    h_ref[...] = (_rms(xn) * g_next_ref[...]).astype(h_ref.dtype)


def _resnorm_last_body(f, x, g_post_ref, xo_ref):
    xo_ref[...] = x + _rms(f.astype(F32)) * g_post_ref[...]


def norm_cast(x_parts, g, total):
    return _rowwise_call(_norm_cast_body, "norm_cast", 0, total, [x_parts], [g], [BF16])[0]


def resnorm(f, x_parts, g_post, g_next):
    total = f.shape[0]
    return _rowwise_call(_resnorm_body, "resnorm", 0, total, [[(f, 0, total)], x_parts], [g_post, g_next],
                         [F32, BF16])


def resnorm_last_to_parts(f, x, g_post, parts):
    total = f.shape[0]
    return [_rowwise_call(_resnorm_last_body, "resnorm_last", row0, n, [[(f, 0, total)], [(x, 0, total)]],
                          [g_post], [F32])[0] for row0, n in parts]


CAST_JOB_ROW_CHOICES = (32, 64, 128, 256, 512, 1024)


class CastJob:
    def __init__(self, w, layer, out_rows, out_cols, n_steps):
        _, n_rows, n_cols = w.shape
        fits = [r for r in CAST_JOB_ROW_CHOICES
                if n_rows % r == 0 and out_rows % r == 0 and out_rows // r <= n_steps]
        self.feasible = bool(fits)
        if not fits:
            return
        self.w, self.layer, self.rows = w, layer, fits[0]
        self.n_cols, self.out_rows, self.out_cols = n_cols, out_rows, out_cols
        self.in_blocks, self.out_blocks = n_rows // self.rows, out_rows // self.rows

    def specs(self, step_of):
        in_spec = pl.BlockSpec((None, self.rows, self.n_cols),
                               lambda *g: (self.layer, jnp.minimum(step_of(*g), self.in_blocks - 1), 0))
        out_spec = pl.BlockSpec((self.rows, self.out_cols),
                                lambda *g: (jnp.minimum(step_of(*g), self.out_blocks - 1), 0))
        return in_spec, out_spec, jax.ShapeDtypeStruct((self.out_rows, self.out_cols), BF16)

    def run(self, step, w_ref, o_ref):
        @pl.when(step < self.in_blocks)
        def _():
            o_ref[:, :self.n_cols] = w_ref[...].astype(o_ref.dtype)
            if self.out_cols > self.n_cols:
                o_ref[:, self.n_cols:] = jnp.zeros((self.rows, self.out_cols - self.n_cols), o_ref.dtype)

        if self.out_blocks > self.in_blocks:
            @pl.when((step >= self.in_blocks) & (step < self.out_blocks))
            def _():
                o_ref[...] = jnp.zeros(o_ref.shape, o_ref.dtype)


def _cast_weight_kernel(w_ref, o_ref, *, valid_rows, valid_cols):
    br, bc = o_ref.shape
    r = pl.program_id(0) * br + lax.broadcasted_iota(jnp.int32, (br, bc), 0)
    c = pl.program_id(1) * bc + lax.broadcasted_iota(jnp.int32, (br, bc), 1)
    o_ref[...] = jnp.where((r < valid_rows) & (c < valid_cols), w_ref[...], 0.0).astype(o_ref.dtype)


def cast_weight(w, layer, out_rows=None, out_cols=None, br=512, bc=1024):
    _, rows, cols = w.shape
    out_rows = out_rows or rows
    out_cols = out_cols or cols
    return pl.pallas_call(
        functools.partial(_cast_weight_kernel, valid_rows=rows, valid_cols=cols),
        grid=(out_rows // br, out_cols // bc),
        in_specs=[pl.BlockSpec((None, br, bc), lambda i, j: (layer, i, j))],
        out_specs=pl.BlockSpec((br, bc), lambda i, j: (i, j)),
        out_shape=jax.ShapeDtypeStruct((out_rows, out_cols), BF16),
        compiler_params=_params(("parallel", "parallel"), 32),
        name="cast_weight",
    )(w)


def _with_cast_jobs(body, n_in, n_out, jobs, n_j):
    n_jobs = len(jobs)

    def kernel(*refs):
        ins, job_ins = refs[:n_in], refs[n_in:n_in + n_jobs]
        outs = refs[n_in + n_jobs:n_in + n_jobs + n_out]
        job_outs = refs[n_in + n_jobs + n_out:n_in + n_jobs + n_out + n_jobs]
        body(*ins, *outs, *refs[n_in + n_jobs + n_out + n_jobs:])
        step = pl.program_id(0) * n_j + pl.program_id(1)
        for job, w_ref, o_ref in zip(jobs, job_ins, job_outs):
            job.run(step, w_ref, o_ref)

    return kernel


def _matmul_call(body, name, ins, in_specs, out_spec, out_shape, grid, job_requests, vmem_mb):
    n_i, n_j = grid
    jobs = [CastJob(w, layer, r, c, n_i * n_j) for w, layer, r, c in job_requests]
    riding = [job for job in jobs if job.feasible]
    job_specs = [job.specs(lambda i, j: i * n_j + j) for job in riding]
    outs = pl.pallas_call(
        _with_cast_jobs(body, len(ins), 1, riding, n_j),
        grid=grid,
        in_specs=list(in_specs) + [s[0] for s in job_specs],
        out_specs=[out_spec] + [s[1] for s in job_specs],
        out_shape=[out_shape] + [s[2] for s in job_specs],
        compiler_params=_params(("arbitrary", "arbitrary"), vmem_mb),
        name=name,
    )(*ins, *[job.w for job in riding])
    cast = iter(outs[1:])
    weights = [next(cast) if job.feasible else cast_weight(w, layer, r, c)
               for job, (w, layer, r, c) in zip(jobs, job_requests)]
    return outs[0], weights


def _mm_kernel(a_ref, b_ref, o_ref):
    o_ref[...] = jnp.dot(a_ref[...], b_ref[...], preferred_element_type=F32).astype(o_ref.dtype)


def matmul(a, b, out_dtype, job_requests=(), bm=1024, bn=1024):
    m, k = a.shape
    _, n = b.shape
    bm = min(bm, m)
    return _matmul_call(
        _mm_kernel, "matmul_full_k", [a, b],
        [pl.BlockSpec((bm, k), lambda i, j: (i, 0)), pl.BlockSpec((k, bn), lambda i, j: (0, j))],
        pl.BlockSpec((bm, bn), lambda i, j: (i, j)), jax.ShapeDtypeStruct((m, n), out_dtype),
        (m // bm, n // bn), job_requests, 56)


def matmul_two_lhs(a1_parts, a2, b, bm=1024, bn=1024):
    m, k2 = a2.shape
    k1 = a1_parts[0][0].shape[1]
    assert k1 == k2 and b.shape[0] == k1 + k2
    n = b.shape[1]
    bm = min([bm] + [rows for _, _, rows in a1_parts])
    n_parts = len(a1_parts)
    bounds = [row0 // bm for _, row0, _ in a1_parts] + [m // bm]

    def kernel(*refs):
        a2_ref, b1_ref, b2_ref, o_ref = refs[n_parts:]
        i = pl.program_id(0)
        for p in range(n_parts):
            @pl.when((i >= bounds[p]) & (i < bounds[p + 1]))
            def _(a1_ref=refs[p]):
                o_ref[...] = (jnp.dot(a1_ref[...], b1_ref[...], preferred_element_type=F32)
                              + jnp.dot(a2_ref[...], b2_ref[...], preferred_element_type=F32)).astype(o_ref.dtype)

    return pl.pallas_call(
        kernel,
        grid=(m // bm, n // bn),
        in_specs=_part_specs(a1_parts, bm, k1, lambda i, j: i) + [
            pl.BlockSpec((bm, k2), lambda i, j: (i, 0)),
            pl.BlockSpec((k1, bn), lambda i, j: (0, j)),
            pl.BlockSpec((k2, bn), lambda i, j: (1, j))],
        out_specs=pl.BlockSpec((bm, bn), lambda i, j: (i, j)),
        out_shape=jax.ShapeDtypeStruct((m, n), BF16),
        compiler_params=_params(("parallel", "parallel"), 56),
        name="matmul_two_lhs",
    )(*[a for a, _, _ in a1_parts], a2, b, b)


def _mm_acc_kernel(a_ref, b_ref, o_ref, acc_ref, *, nk, last_live):
    k = pl.program_id(2)

    @pl.when(k == 0)
    def _():
        acc_ref[...] = jnp.dot(a_ref[...], b_ref[...], preferred_element_type=F32)

    @pl.when((k > 0) & (k < nk - 1))
    def _():
        acc_ref[...] += jnp.dot(a_ref[...], b_ref[...], preferred_element_type=F32)

    @pl.when(k == nk - 1)
    def _():
        o_ref[...] = (acc_ref[...] + jnp.dot(a_ref[:, :last_live], b_ref[:last_live, :],
                                             preferred_element_type=F32)).astype(o_ref.dtype)


def matmul_split_k(a, b, bk, live_k, bm=1024, bn=1024):
    m, k = a.shape
    _, n = b.shape
    bm = min(bm, m)
    nk = k // bk
    last_live = live_k - (nk - 1) * bk
    assert nk >= 2 and 0 < last_live <= bk and last_live % MXU_DEPTH == 0
    return pl.pallas_call(
        functools.partial(_mm_acc_kernel, nk=nk, last_live=last_live),
        grid=(m // bm, n // bn, nk),
        in_specs=[pl.BlockSpec((bm, bk), lambda i, j, kk: (i, kk)),
                  pl.BlockSpec((bk, bn), lambda i, j, kk: (kk, j))],
        out_specs=pl.BlockSpec((bm, bn), lambda i, j, kk: (i, j)),
        out_shape=jax.ShapeDtypeStruct((m, n), BF16),
        scratch_shapes=[pltpu.VMEM((bm, bn), F32)],
        compiler_params=_params(("parallel", "parallel", "arbitrary"), 48),
        name="matmul_split_k",
    )(a, b)


def _gateup_kernel(h_ref, wg_ref, wu_ref, o_ref, *, n_j, last_live):
    def tile(width):
        h = h_ref[...]
        g = jnp.dot(h, wg_ref[:, :width], preferred_element_type=F32)
        u = jnp.dot(h, wu_ref[:, :width], preferred_element_type=F32)
        o_ref[:, :width] = (g * jax.nn.sigmoid(g) * u).astype(o_ref.dtype)

    j = pl.program_id(1)
    bn = o_ref.shape[1]
    if last_live == bn:
        tile(bn)
        return

    @pl.when(j < n_j - 1)
    def _():
        tile(bn)

    @pl.when(j == n_j - 1)
    def _():
        tile(last_live)
        o_ref[:, last_live:] = jnp.zeros((o_ref.shape[0], bn - last_live), o_ref.dtype)


def gate_up(h, wg, wu, live_cols, job_requests=(), bm=1024, bn=512):
    m, k = h.shape
    _, n = wg.shape
    bm = min(bm, m)
    n_j = n // bn
    last_live = live_cols - (n_j - 1) * bn
    assert 0 < last_live <= bn and last_live % MXU_DEPTH == 0
    w_spec = pl.BlockSpec((k, bn), lambda i, j: (0, j))
    return _matmul_call(
        functools.partial(_gateup_kernel, n_j=n_j, last_live=last_live), "gate_up", [h, wg, wu],
        [pl.BlockSpec((bm, k), lambda i, j: (i, 0)), w_spec, w_spec],
        pl.BlockSpec((bm, bn), lambda i, j: (i, j)), jax.ShapeDtypeStruct((m, n), BF16),
        (m // bm, n // bn), job_requests, 56)


ATTN_KEY_BLOCK = 1024
ATTN_KEYS_PER_LOOP_TRIP = 4096
V_ROWS = HEAD_DIM + 16
LOG2E = 1.4426950408889634


def _rope_tables(max_len):
    pos = np.arange(max_len)
    r = (pos // GRID_W).astype(np.float32)
    c = (pos % GRID_W).astype(np.float32)
    inv = (1.0 / (ROPE_THETA ** (np.arange(0, ROPE_HALF, 2, dtype=np.float32) / ROPE_HALF))).astype(np.float32)
    ang_r = r[:, None] * inv
    ang_c = c[:, None] * inv
    cos = np.concatenate([np.cos(ang_r), np.cos(ang_r), np.cos(ang_c), np.cos(ang_c)], axis=-1)
    sin = np.concatenate([-np.sin(ang_r), np.sin(ang_r), -np.sin(ang_c), np.sin(ang_c)], axis=-1)
    return jnp.asarray(cos, F32), jnp.asarray(sin, F32)


PAIR_W = 2 * HEAD_DIM


def _rope_partner_matrix():
    idx = np.arange(PAIR_W)
    perm = np.zeros((PAIR_W, PAIR_W), np.float32)
    perm[idx ^ (ROPE_HALF // 2), idx] = 1.0
    return jnp.asarray(perm, BF16)


def _norm_rope_pair(x, gain, cos, sin, perm):
    xn = jnp.concatenate([_rms(x[:, :HEAD_DIM]), _rms(x[:, HEAD_DIM:])], axis=1) * gain
    hi = xn.astype(BF16)
    lo = (xn - hi.astype(F32)).astype(BF16)
    partner = (jnp.dot(hi, perm, preferred_element_type=F32) + jnp.dot(lo, perm, preferred_element_type=F32))
    return xn * cos + partner * sin


def _qkv_prep_kernel(q_ref, k_ref, v_ref, cos_ref, sin_ref, qg_ref, kg_ref, perm_ref,
                     qo_ref, ko_ref, vt_ref):
    rows = cos_ref.shape[0]
    cos = jnp.concatenate([cos_ref[...]] * 2, axis=1)
    sin = jnp.concatenate([sin_ref[...]] * 2, axis=1)
    perm = perm_ref[...]
    q_scale = HEAD_DIM ** -0.5 * LOG2E
    qg = jnp.concatenate([qg_ref[...]] * 2, axis=1)
    kg = jnp.concatenate([kg_ref[...]] * 2, axis=1)
    for p in range(N_Q_HEADS // 2):
        sl = slice(p * PAIR_W, (p + 1) * PAIR_W)
        y = _norm_rope_pair(q_ref[:, sl].astype(F32), qg, cos, sin, perm) * q_scale
        qo_ref[:, sl] = y.astype(qo_ref.dtype)
    for p in range(N_KV_HEADS // 2):
        sl = slice(p * PAIR_W, (p + 1) * PAIR_W)
        y = _norm_rope_pair(k_ref[:, sl].astype(F32), kg, cos, sin, perm)
        ko_ref[:, sl] = y.astype(ko_ref.dtype)
    for h in range(N_KV_HEADS):
        sl = slice(h * HEAD_DIM, (h + 1) * HEAD_DIM)
        vt_ref[h, 0, :HEAD_DIM, :] = v_ref[:, sl].astype(F32).T.astype(vt_ref.dtype)
        vt_ref[h, 0, HEAD_DIM:, :] = jnp.ones((V_ROWS - HEAD_DIM, rows), vt_ref.dtype)


def qkv_prep(proj, cos, sin, q_gain, k_gain, key_block, segments, rows=512):
    t = proj.shape[0]
    rows = min(rows, key_block)
    per_key_block = key_block // rows
    assert all(s0 % rows == 0 for s0, _ in segments)

    def position_block(i):
        block = i
        for s0, _ in segments:
            block = jnp.where(i * rows >= s0, i - s0 // rows, block)
        return block

    return pl.pallas_call(
        _qkv_prep_kernel,
        grid=(t // rows,),
        in_specs=[pl.BlockSpec((rows, ATTN_WIDTH), lambda i: (i, Q_OFF // ATTN_WIDTH)),
                  pl.BlockSpec((rows, KV_WIDTH), lambda i: (i, K_OFF // KV_WIDTH)),
                  pl.BlockSpec((rows, KV_WIDTH), lambda i: (i, V_OFF // KV_WIDTH)),
                  pl.BlockSpec((rows, HEAD_DIM), lambda i: (position_block(i), 0)),
                  pl.BlockSpec((rows, HEAD_DIM), lambda i: (position_block(i), 0)),
                  pl.BlockSpec((1, HEAD_DIM), lambda i: (0, 0)),
                  pl.BlockSpec((1, HEAD_DIM), lambda i: (0, 0)),
                  pl.BlockSpec((PAIR_W, PAIR_W), lambda i: (0, 0))],
        out_specs=[pl.BlockSpec((rows, ATTN_WIDTH), lambda i: (i, 0)),
                   pl.BlockSpec((rows, KV_WIDTH), lambda i: (i, 0)),
                   pl.BlockSpec((N_KV_HEADS, 1, V_ROWS, rows),
                                lambda i: (0, i // per_key_block, 0, i % per_key_block))],
        out_shape=[jax.ShapeDtypeStruct((t, ATTN_WIDTH), BF16),
                   jax.ShapeDtypeStruct((t, KV_WIDTH), BF16),
                   jax.ShapeDtypeStruct((N_KV_HEADS, t // key_block, V_ROWS, key_block), BF16)],
        compiler_params=_params(("parallel",), 40),
        name="qkv_prep",
    )(proj, proj, proj, cos, sin, q_gain.reshape(1, HEAD_DIM), k_gain.reshape(1, HEAD_DIM),
      _rope_partner_matrix())


def _attn_kernel(q_ref, k_ref, vt_ref, *rest, bk, nk):
    o_ref, m_scr, acc_scr, s_scr, smax_scr = rest[-5:]
    m_scr[...] = jnp.full(m_scr.shape, -jnp.inf, F32)
    acc_scr[...] = jnp.zeros(acc_scr.shape, F32)

    def scores(kb, h, slot):
        s = lax.dot_general(kb, q_ref[:, h * HEAD_DIM:(h + 1) * HEAD_DIM], (((1,), (1,)), ((), ())),
                            preferred_element_type=F32)
        s_scr[slot] = s
        smax_scr[slot] = jnp.max(s, axis=0, keepdims=True)

    def key_block(j):
        start = j * bk if isinstance(j, int) else pl.multiple_of(j * bk, bk)
        return k_ref[pl.ds(start, bk), :]

    scores(key_block(0), 0, 0)

    def block_units(j, has_next):
        kb = key_block(j)
        vb = vt_ref[j]
        for h in range(Q_PER_KV):
            slot = h % 2
            if h + 1 < Q_PER_KV:
                scores(kb, h + 1, 1 - slot)
            elif has_next:
                scores(key_block(j + 1), 0, 1 - slot)
            m_prev = m_scr[h]
            m_new = jnp.maximum(m_prev, smax_scr[slot])
            alpha = jnp.exp2(m_prev - m_new)
            p = jnp.exp2(s_scr[slot] - m_new).astype(vb.dtype)
            acc_scr[h] = acc_scr[h] * alpha + jnp.dot(vb, p, preferred_element_type=F32)
            m_scr[h] = m_new

    per_trip = max(1, min(ATTN_KEYS_PER_LOOP_TRIP // bk, nk // 3))
    assert nk % per_trip == 0

    def trip(i, carry):
        for u in range(per_trip):
            block_units(i * per_trip + u, True)
        return carry

    lax.fori_loop(0, nk // per_trip - 1, trip, 0)
    for u in range(per_trip):
        block_units(nk - per_trip + u, u + 1 < per_trip)
    for h in range(Q_PER_KV):
        acc = acc_scr[h]
        o = acc[:HEAD_DIM, :] / acc[HEAD_DIM:HEAD_DIM + 1, :]
        o_ref[:, h * HEAD_DIM:(h + 1) * HEAD_DIM] = o.T.astype(o_ref.dtype)


def attention(q_rot, k_rot, vt, row0, batch, seq, bq=512):
    bk = vt.shape[-1]
    bq = min(bq, seq)
    assert row0 % seq == 0 and seq % bq == 0 and seq % bk == 0
    group_w = Q_PER_KV * HEAD_DIM
    nq = seq // bq
    nk = seq // bk
    in_specs = [
        pl.BlockSpec((bq, group_w), lambda b, h, i: (row0 // bq + b * nq + i, h)),
        pl.BlockSpec((seq, HEAD_DIM), lambda b, h, i: (row0 // seq + b, h)),
        pl.BlockSpec((None, nk, V_ROWS, bk), lambda b, h, i: (h, row0 // seq + b, 0, 0)),
    ]
    return pl.pallas_call(
        functools.partial(_attn_kernel, bk=bk, nk=nk),
        grid=(batch, N_KV_HEADS, nq),
        in_specs=in_specs,
        out_specs=pl.BlockSpec((bq, group_w), lambda b, h, i: (b * nq + i, h)),
        out_shape=jax.ShapeDtypeStruct((batch * seq, ATTN_WIDTH), BF16),
        scratch_shapes=[pltpu.VMEM((Q_PER_KV, 1, bq), F32),
                        pltpu.VMEM((Q_PER_KV, V_ROWS, bq), F32),
                        pltpu.VMEM((2, bk, bq), F32),
                        pltpu.VMEM((2, 1, bq), F32)],
        compiler_params=_params(("parallel", "parallel", "parallel"), 48),
        name="attention",
    )(q_rot, k_rot, vt)


def _convpool_kernel(ch_ref, cb_ref, cc_ref, pu_ref,
                     chp_ref, ccp_ref, pup_ref, chn_ref, ccn_ref, pun_ref,
                     cw_ref, pw_ref, ps_ref, o_ref, *, rows, segments):
    i = pl.program_id(0)
    start = i * rows
    pos0 = start
    seq_len = jnp.int32(0)
    has_prev = jnp.bool_(True)
    has_next = jnp.bool_(True)
    for seg_start, seg_len in segments:
        inside = (start >= seg_start) & (start < seg_start + seg_len)
        pos0 = jnp.where(inside, start - seg_start, pos0)
        seq_len = jnp.where(inside, seg_len, seq_len)
        has_prev = has_prev & (start != seg_start)
        has_next = has_next & (start + rows != seg_start + seg_len)
    prev_on = has_prev.astype(F32)
    next_on = has_next.astype(F32)

    def extended(cur_ref, prev_ref, next_ref):
        return jnp.concatenate([prev_ref[...].astype(F32) * prev_on,
                                cur_ref[...].astype(F32),
                                next_ref[...].astype(F32) * next_on], axis=0)

    ext_rows = rows + 2 * HALO

    def shifted(x, d):
        return pltpu.roll(x, d % ext_rows, axis=0)

    u = extended(cc_ref, ccp_ref, ccn_ref) * extended(ch_ref, chp_ref, chn_ref)
    cw = cw_ref[...]
    y = shifted(u, 1) * cw[0:1, :] + u * cw[1:2, :] + shifted(u, -1) * cw[2:3, :]
    o_ref[:, :CONV_WIDTH] = (cb_ref[...].astype(F32) * y[HALO:HALO + rows, :]).astype(o_ref.dtype)

    x = extended(pu_ref, pup_ref, pun_ref)
    pos = (pos0 + lax.broadcasted_iota(jnp.int32, (rows, 1), 0))
    for gi, w in enumerate(POOL_WINDOWS):
        sl = slice(gi * POOL_GROUP, (gi + 1) * POOL_GROUP)
        xg = x[:, sl]
        tot = xg + shifted(xg, 1)
        half = 1
        while 2 * half < w:
            tot = shifted(tot, half) + shifted(tot, -half)
            half *= 2
        hi = jnp.minimum(pos + w // 2, seq_len)
        lo = jnp.maximum(pos - w // 2, 0)
        cnt = (hi - lo).astype(F32)
        m = tot[HALO:HALO + rows, :] / cnt - xg[HALO:HALO + rows, :]
        yg = jnp.dot(m.astype(BF16), pw_ref[gi], preferred_element_type=F32)
        o_ref[:, CONV_WIDTH + gi * POOL_GROUP:CONV_WIDTH + (gi + 1) * POOL_GROUP] = (
            yg * ps_ref[:, sl]).astype(o_ref.dtype)


def conv_pool(proj, conv_w, pool_w, pool_scale, segments, rows=512):
    t = proj.shape[0]
    rows = min(rows, min(n for _, n in segments))
    assert all(s % rows == 0 and n % rows == 0 for s, n in segments)
    w = CONV_WIDTH
    per = rows // HALO
    last = t // HALO - 1

    def cur(off):
        return pl.BlockSpec((rows, w), lambda i: (i, off // w))

    def prev(off):
        return pl.BlockSpec((HALO, w), lambda i: (jnp.maximum(i * per - 1, 0), off // w))

    def nxt(off):
        return pl.BlockSpec((HALO, w), lambda i: (jnp.minimum((i + 1) * per, last), off // w))

    kernel = functools.partial(_convpool_kernel, rows=rows, segments=segments)
    return pl.pallas_call(
        kernel,
        grid=(t // rows,),
        in_specs=[cur(CH_OFF), cur(CB_OFF), cur(CC_OFF), cur(PU_OFF),
                  prev(CH_OFF), prev(CC_OFF), prev(PU_OFF),
                  nxt(CH_OFF), nxt(CC_OFF), nxt(PU_OFF),
                  pl.BlockSpec((3, w), lambda i: (0, 0)),
                  pl.BlockSpec((len(POOL_WINDOWS), POOL_GROUP, POOL_GROUP), lambda i: (0, 0, 0)),
                  pl.BlockSpec((1, POOL_WIDTH), lambda i: (0, 0))],
        out_specs=pl.BlockSpec((rows, CONV_WIDTH + POOL_WIDTH), lambda i: (i, 0)),
        out_shape=jax.ShapeDtypeStruct((t, CONV_WIDTH + POOL_WIDTH), BF16),
        compiler_params=_params(("parallel",), 48),
        name="conv_pool",
    )(proj, proj, proj, proj, proj, proj, proj, proj, proj, proj,
      conv_w, pool_w, pool_scale.reshape(1, POOL_WIDTH))


def _trunk(x_parts, segments, attn_calls, ln_mix_pre, ln_mix_post, ln_ffn_pre, ln_ffn_post, q_norm, k_norm,
           w_in, conv_w, pool_w, pool_scale, w_out, w_gate, w_up, w_down):
    total = sum(n for _, _, n in x_parts)
    cos, sin = _rope_tables(max(n for _, n in segments))
    key_block = min(ATTN_KEY_BLOCK, min(n for _, n in segments))
    out_parts = [(row0, n) for _, row0, n in x_parts]
    h = norm_cast(x_parts, ln_mix_pre[0], total)
    w_in_bf16 = cast_weight(w_in, 0)
    d_model = w_out.shape[2]
    for l in range(DEPTH):
        proj, (w_out_bf16, w_gate_bf16, w_up_bf16) = matmul(
            h, w_in_bf16, BF16,
            [(w_out, l, w_out.shape[1], d_model), (w_gate, l, d_model, D_FF_PAD), (w_up, l, d_model, D_FF_PAD)])
        q_rot, k_rot, vt = qkv_prep(proj, cos, sin, q_norm[l], k_norm[l], key_block, segments)
        a_parts = [(attention(q_rot, k_rot, vt, row0, batch, seq), row0, batch * seq)
                   for row0, batch, seq in attn_calls]
        cp_out = conv_pool(proj, conv_w[l], pool_w[l].astype(BF16), pool_scale[l], segments)
        mix = matmul_two_lhs(a_parts, cp_out, w_out_bf16)
        x, h = resnorm(mix, x_parts, ln_mix_post[l], ln_ffn_pre[l])
        x_parts = [(x, 0, total)]
        jobs = [(w_down, l, D_FF_PAD, d_model)]
        if l + 1 < DEPTH:
            jobs.append((w_in, l + 1, d_model, w_in.shape[2]))
        act, cast = gate_up(h, w_gate_bf16, w_up_bf16, w_gate.shape[2], jobs)
        w_down_bf16 = cast[0]
        w_in_bf16 = cast[1] if l + 1 < DEPTH else None
        f = matmul_split_k(act, w_down_bf16, D_FF_PAD // 4, w_down.shape[1])
        if l + 1 < DEPTH:
            x, h = resnorm(f, x_parts, ln_ffn_post[l], ln_mix_pre[l + 1])
            x_parts = [(x, 0, total)]
    return resnorm_last_to_parts(f, x, ln_ffn_post[DEPTH - 1], out_parts)


def kernel(x_prompt, x_sample, ln_mix_pre, ln_mix_post, ln_ffn_pre, ln_ffn_post, q_norm, k_norm,
           w_in, conv_w, pool_w, pool_scale, w_out, w_gate, w_up, w_down):
    pb, ps, d = x_prompt.shape
    sb, ss, _ = x_sample.shape
    x_parts = [(x_prompt.reshape(pb * ps, d), 0, pb * ps), (x_sample.reshape(sb * ss, d), pb * ps, sb * ss)]
    segments = tuple((b * ps, ps) for b in range(pb)) + tuple((pb * ps + b * ss, ss) for b in range(sb))
    attn_calls = ((0, pb, ps), (pb * ps, sb, ss))
    y_prompt, y_sample = _trunk(x_parts, segments, attn_calls, ln_mix_pre, ln_mix_post, ln_ffn_pre, ln_ffn_post,
                                q_norm, k_norm, w_in, conv_w, pool_w, pool_scale, w_out, w_gate, w_up, w_down)
    return (y_prompt.reshape(pb, ps, d), y_sample.reshape(sb, ss, d))
```

```python
import functools

import numpy as np
import jax
import jax.numpy as jnp
from jax import lax
from jax.experimental import pallas as pl
from jax.experimental.pallas import tpu as pltpu

D_MODEL = 4096
DEPTH = 2
GRID_W = 64
HEAD_DIM = 128
N_Q_HEADS = 16
N_KV_HEADS = 4
Q_PER_KV = N_Q_HEADS // N_KV_HEADS
ATTN_WIDTH = N_Q_HEADS * HEAD_DIM
KV_WIDTH = N_KV_HEADS * HEAD_DIM
CONV_WIDTH = D_MODEL // 4
POOL_WIDTH = D_MODEL // 4
POOL_WINDOWS = (2, 4, 8, 16)
POOL_GROUP = POOL_WIDTH // len(POOL_WINDOWS)
D_FF = 11008
D_FF_PAD = 11264
ROPE_HALF = HEAD_DIM // 2
ROPE_THETA = 10000.0
EPS = 1e-6

Q_OFF = 0
K_OFF = ATTN_WIDTH
V_OFF = K_OFF + KV_WIDTH
CH_OFF = V_OFF + KV_WIDTH
CB_OFF = CH_OFF + CONV_WIDTH
CC_OFF = CB_OFF + CONV_WIDTH
PU_OFF = CC_OFF + CONV_WIDTH

HALO = 16
V7X_VMEM_BYTES = 64 * 1024 * 1024
MXU_DEPTH = 256

F32 = jnp.float32
BF16 = jnp.bfloat16


def _params(semantics, vmem_mb):
    assert vmem_mb * 1024 * 1024 < V7X_VMEM_BYTES
    return pltpu.CompilerParams(dimension_semantics=semantics,
                                vmem_limit_bytes=vmem_mb * 1024 * 1024)


def _rms(x):
    return x * lax.rsqrt(jnp.mean(x * x, axis=-1, keepdims=True) + EPS)


def _part_specs(parts, block_rows, width, block_of):
    specs = []
    for _, row0, n in parts:
        assert row0 % block_rows == 0 and n % block_rows == 0
        first, count = row0 // block_rows, n // block_rows
        specs.append(pl.BlockSpec(
            (block_rows, width), lambda *g, first=first, count=count: (jnp.clip(block_of(*g) - first, 0, count - 1), 0)))
    return specs


def _part_select(parts, block_rows, block, refs):
    value = refs[0][...]
    for (_, row0, _), ref in zip(parts[1:], refs[1:]):
        value = jnp.where(block >= row0 // block_rows, ref[...], value)
    return value


def _rowwise_call(body, name, slab_row0, n_rows, sources, gains, out_dtypes, rows=256):
    d = sources[0][0][0].shape[1]
    first_block = slab_row0 // rows
    assert slab_row0 % rows == 0 and n_rows % rows == 0
    in_specs, args = [], []
    for parts in sources:
        in_specs += _part_specs(parts, rows, d, lambda i: first_block + i)
        args += [a for a, _, _ in parts]
    in_specs += [pl.BlockSpec((1, d), lambda i: (0, 0)) for _ in gains]
    args += [g.reshape(1, d) for g in gains]

    def kernel(*refs):
        block = first_block + pl.program_id(0)
        values, at = [], 0
        for parts in sources:
            values.append(_part_select(parts, rows, block, refs[at:at + len(parts)]))
            at += len(parts)
        body(*values, *refs[at:])

    return pl.pallas_call(
        kernel,
        grid=(n_rows // rows,),
        in_specs=in_specs,
        out_specs=[pl.BlockSpec((rows, d), lambda i: (i, 0)) for _ in out_dtypes],
        out_shape=[jax.ShapeDtypeStruct((n_rows, d), dt) for dt in out_dtypes],
        compiler_params=_params(("parallel",), 48),
        name=name,
    )(*args)


def _norm_cast_body(x, g_ref, h_ref):
    h_ref[...] = (_rms(x) * g_ref[...]).astype(h_ref.dtype)


def _resnorm_body(f, x, g_post_ref, g_next_ref, xo_ref, h_ref):
    xn = x + _rms(f.astype(F32)) * g_post_ref[...]
    xo_ref[...] = xn
    h_ref[...] = (_rms(xn) * g_next_ref[...]).astype(h_ref.dtype)


def _resnorm_last_body(f, x, g_post_ref, xo_ref):
    xo_ref[...] = x + _rms(f.astype(F32)) * g_post_ref[...]


def norm_cast(x_parts, g, total):
    return _rowwise_call(_norm_cast_body, "norm_cast", 0, total, [x_parts], [g], [BF16])[0]


def resnorm(f, x_parts, g_post, g_next):
    total = f.shape[0]
    return _rowwise_call(_resnorm_body, "resnorm", 0, total, [[(f, 0, total)], x_parts], [g_post, g_next],
                         [F32, BF16])


def resnorm_last_to_parts(f, x, g_post, parts):
    total = f.shape[0]
    return [_rowwise_call(_resnorm_last_body, "resnorm_last", row0, n, [[(f, 0, total)], [(x, 0, total)]],
                          [g_post], [F32])[0] for row0, n in parts]


CAST_JOB_ROW_CHOICES = (32, 64, 128, 256, 512, 1024)


class CastJob:
    def __init__(self, w, layer, out_rows, out_cols, n_steps):
        _, n_rows, n_cols = w.shape
        fits = [r for r in CAST_JOB_ROW_CHOICES
                if n_rows % r == 0 and out_rows % r == 0 and out_rows // r <= n_steps]
        self.feasible = bool(fits)
        if not fits:
            return
        self.w, self.layer, self.rows = w, layer, fits[0]
        self.n_cols, self.out_rows, self.out_cols = n_cols, out_rows, out_cols
        self.in_blocks, self.out_blocks = n_rows // self.rows, out_rows // self.rows

    def specs(self, step_of):
        in_spec = pl.BlockSpec((None, self.rows, self.n_cols),
                               lambda *g: (self.layer, jnp.minimum(step_of(*g), self.in_blocks - 1), 0))
        out_spec = pl.BlockSpec((self.rows, self.out_cols),
                                lambda *g: (jnp.minimum(step_of(*g), self.out_blocks - 1), 0))
        return in_spec, out_spec, jax.ShapeDtypeStruct((self.out_rows, self.out_cols), BF16)

    def run(self, step, w_ref, o_ref):
        @pl.when(step < self.in_blocks)
        def _():
            o_ref[:, :self.n_cols] = w_ref[...].astype(o_ref.dtype)
            if self.out_cols > self.n_cols:
                o_ref[:, self.n_cols:] = jnp.zeros((self.rows, self.out_cols - self.n_cols), o_ref.dtype)

        if self.out_blocks > self.in_blocks:
            @pl.when((step >= self.in_blocks) & (step < self.out_blocks))
            def _():
                o_ref[...] = jnp.zeros(o_ref.shape, o_ref.dtype)


def _cast_weight_kernel(w_ref, o_ref, *, valid_rows, valid_cols):
    br, bc = o_ref.shape
    r = pl.program_id(0) * br + lax.broadcasted_iota(jnp.int32, (br, bc), 0)
    c = pl.program_id(1) * bc + lax.broadcasted_iota(jnp.int32, (br, bc), 1)
    o_ref[...] = jnp.where((r < valid_rows) & (c < valid_cols), w_ref[...], 0.0).astype(o_ref.dtype)


def cast_weight(w, layer, out_rows=None, out_cols=None, br=512, bc=1024):
    _, rows, cols = w.shape
    out_rows = out_rows or rows
    out_cols = out_cols or cols
    return pl.pallas_call(
        functools.partial(_cast_weight_kernel, valid_rows=rows, valid_cols=cols),
        grid=(out_rows // br, out_cols // bc),
        in_specs=[pl.BlockSpec((None, br, bc), lambda i, j: (layer, i, j))],
        out_specs=pl.BlockSpec((br, bc), lambda i, j: (i, j)),
        out_shape=jax.ShapeDtypeStruct((out_rows, out_cols), BF16),
        compiler_params=_params(("parallel", "parallel"), 32),
        name="cast_weight",
    )(w)


def _with_cast_jobs(body, n_in, n_out, jobs, n_j):
    n_jobs = len(jobs)

    def kernel(*refs):
        ins, job_ins = refs[:n_in], refs[n_in:n_in + n_jobs]
        outs = refs[n_in + n_jobs:n_in + n_jobs + n_out]
        job_outs = refs[n_in + n_jobs + n_out:n_in + n_jobs + n_out + n_jobs]
        body(*ins, *outs, *refs[n_in + n_jobs + n_out + n_jobs:])
        step = pl.program_id(0) * n_j + pl.program_id(1)
        for job, w_ref, o_ref in zip(jobs, job_ins, job_outs):
            job.run(step, w_ref, o_ref)

    return kernel


def _matmul_call(body, name, ins, in_specs, out_spec, out_shape, grid, job_requests, vmem_mb):
    n_i, n_j = grid
    jobs = [CastJob(w, layer, r, c, n_i * n_j) for w, layer, r, c in job_requests]
    riding = [job for job in jobs if job.feasible]
    job_specs = [job.specs(lambda i, j: i * n_j + j) for job in riding]
    outs = pl.pallas_call(
        _with_cast_jobs(body, len(ins), 1, riding, n_j),
        grid=grid,
        in_specs=list(in_specs) + [s[0] for s in job_specs],
        out_specs=[out_spec] + [s[1] for s in job_specs],
        out_shape=[out_shape] + [s[2] for s in job_specs],
        compiler_params=_params(("arbitrary", "arbitrary"), vmem_mb),
        name=name,
    )(*ins, *[job.w for job in riding])
    cast = iter(outs[1:])
    weights = [next(cast) if job.feasible else cast_weight(w, layer, r, c)
               for job, (w, layer, r, c) in zip(jobs, job_requests)]
    return outs[0], weights


def _mm_kernel(a_ref, b_ref, o_ref):
    o_ref[...] = jnp.dot(a_ref[...], b_ref[...], preferred_element_type=F32).astype(o_ref.dtype)


def matmul(a, b, out_dtype, job_requests=(), bm=1024, bn=1024):
    m, k = a.shape
    _, n = b.shape
    bm = min(bm, m)
    return _matmul_call(
        _mm_kernel, "matmul_full_k", [a, b],
        [pl.BlockSpec((bm, k), lambda i, j: (i, 0)), pl.BlockSpec((k, bn), lambda i, j: (0, j))],
        pl.BlockSpec((bm, bn), lambda i, j: (i, j)), jax.ShapeDtypeStruct((m, n), out_dtype),
        (m // bm, n // bn), job_requests, 56)


def matmul_two_lhs(a1_parts, a2, b, bm=1024, bn=1024):
    m, k2 = a2.shape
    k1 = a1_parts[0][0].shape[1]
    assert k1 == k2 and b.shape[0] == k1 + k2
    n = b.shape[1]
    bm = min([bm] + [rows for _, _, rows in a1_parts])
    n_parts = len(a1_parts)
    bounds = [row0 // bm for _, row0, _ in a1_parts] + [m // bm]

    def kernel(*refs):
        a2_ref, b1_ref, b2_ref, o_ref = refs[n_parts:]
        i = pl.program_id(0)
        for p in range(n_parts):
            @pl.when((i >= bounds[p]) & (i < bounds[p + 1]))
            def _(a1_ref=refs[p]):
                o_ref[...] = (jnp.dot(a1_ref[...], b1_ref[...], preferred_element_type=F32)
                              + jnp.dot(a2_ref[...], b2_ref[...], preferred_element_type=F32)).astype(o_ref.dtype)

    return pl.pallas_call(
        kernel,
        grid=(m // bm, n // bn),
        in_specs=_part_specs(a1_parts, bm, k1, lambda i, j: i) + [
            pl.BlockSpec((bm, k2), lambda i, j: (i, 0)),
            pl.BlockSpec((k1, bn), lambda i, j: (0, j)),
            pl.BlockSpec((k2, bn), lambda i, j: (1, j))],
        out_specs=pl.BlockSpec((bm, bn), lambda i, j: (i, j)),
        out_shape=jax.ShapeDtypeStruct((m, n), BF16),
        compiler_params=_params(("parallel", "parallel"), 56),
        name="matmul_two_lhs",
    )(*[a for a, _, _ in a1_parts], a2, b, b)


def _mm_acc_kernel(a_ref, b_ref, o_ref, acc_ref, *, nk, last_live):
    k = pl.program_id(2)

    @pl.when(k == 0)
    def _():
        acc_ref[...] = jnp.dot(a_ref[...], b_ref[...], preferred_element_type=F32)

    @pl.when((k > 0) & (k < nk - 1))
    def _():
        acc_ref[...] += jnp.dot(a_ref[...], b_ref[...], preferred_element_type=F32)

    @pl.when(k == nk - 1)
    def _():
        o_ref[...] = (acc_ref[...] + jnp.dot(a_ref[:, :last_live], b_ref[:last_live, :],
                                             preferred_element_type=F32)).astype(o_ref.dtype)


def matmul_split_k(a, b, bk, live_k, bm=1024, bn=1024):
    m, k = a.shape
    _, n = b.shape
    bm = min(bm, m)
    nk = k // bk
    last_live = live_k - (nk - 1) * bk
    assert nk >= 2 and 0 < last_live <= bk and last_live % MXU_DEPTH == 0
    return pl.pallas_call(
        functools.partial(_mm_acc_kernel, nk=nk, last_live=last_live),
        grid=(m // bm, n // bn, nk),
        in_specs=[pl.BlockSpec((bm, bk), lambda i, j, kk: (i, kk)),
                  pl.BlockSpec((bk, bn), lambda i, j, kk: (kk, j))],
        out_specs=pl.BlockSpec((bm, bn), lambda i, j, kk: (i, j)),
        out_shape=jax.ShapeDtypeStruct((m, n), BF16),
        scratch_shapes=[pltpu.VMEM((bm, bn), F32)],
        compiler_params=_params(("parallel", "parallel", "arbitrary"), 48),
        name="matmul_split_k",
    )(a, b)


def _gateup_kernel(h_ref, wg_ref, wu_ref, o_ref, *, n_j, last_live):
    def tile(width):
        h = h_ref[...]
        g = jnp.dot(h, wg_ref[:, :width], preferred_element_type=F32)
        u = jnp.dot(h, wu_ref[:, :width], preferred_element_type=F32)
        o_ref[:, :width] = (g * jax.nn.sigmoid(g) * u).astype(o_ref.dtype)

    j = pl.program_id(1)
    bn = o_ref.shape[1]
    if last_live == bn:
        tile(bn)
        return

    @pl.when(j < n_j - 1)
    def _():
        tile(bn)

    @pl.when(j == n_j - 1)
    def _():
        tile(last_live)
        o_ref[:, last_live:] = jnp.zeros((o_ref.shape[0], bn - last_live), o_ref.dtype)


def gate_up(h, wg, wu, live_cols, job_requests=(), bm=1024, bn=512):
    m, k = h.shape
    _, n = wg.shape
    bm = min(bm, m)
    n_j = n // bn
    last_live = live_cols - (n_j - 1) * bn
    assert 0 < last_live <= bn and last_live % MXU_DEPTH == 0
    w_spec = pl.BlockSpec((k, bn), lambda i, j: (0, j))
    return _matmul_call(
        functools.partial(_gateup_kernel, n_j=n_j, last_live=last_live), "gate_up", [h, wg, wu],
        [pl.BlockSpec((bm, k), lambda i, j: (i, 0)), w_spec, w_spec],
        pl.BlockSpec((bm, bn), lambda i, j: (i, j)), jax.ShapeDtypeStruct((m, n), BF16),
        (m // bm, n // bn), job_requests, 56)


ATTN_KEY_BLOCK = 1024
ATTN_KEYS_PER_LOOP_TRIP = 4096
V_ROWS = HEAD_DIM + 16
LOG2E = 1.4426950408889634


def _rope_tables(max_len):
    pos = np.arange(max_len)
    r = (pos // GRID_W).astype(np.float32)
    c = (pos % GRID_W).astype(np.float32)
    inv = (1.0 / (ROPE_THETA ** (np.arange(0, ROPE_HALF, 2, dtype=np.float32) / ROPE_HALF))).astype(np.float32)
    ang_r = r[:, None] * inv
    ang_c = c[:, None] * inv
    cos = np.concatenate([np.cos(ang_r), np.cos(ang_r), np.cos(ang_c), np.cos(ang_c)], axis=-1)
    sin = np.concatenate([-np.sin(ang_r), np.sin(ang_r), -np.sin(ang_c), np.sin(ang_c)], axis=-1)
    return jnp.asarray(cos, F32), jnp.asarray(sin, F32)


PAIR_W = 2 * HEAD_DIM


def _rope_partner_matrix():
    idx = np.arange(PAIR_W)
    perm = np.zeros((PAIR_W, PAIR_W), np.float32)
    perm[idx ^ (ROPE_HALF // 2), idx] = 1.0
    return jnp.asarray(perm, BF16)


def _norm_rope_pair(x, own, other, perm):
    t = jnp.concatenate([_rms(x[:, :HEAD_DIM]), _rms(x[:, HEAD_DIM:])], axis=1)
    hi = t.astype(BF16)
    lo = (t - hi.astype(F32)).astype(BF16)
    partner = (jnp.dot(hi, perm, preferred_element_type=F32) + jnp.dot(lo, perm, preferred_element_type=F32))
    return t * own + partner * other


def _qkv_prep_kernel(q_ref, k_ref, v_ref, cos_ref, sin_ref, qg_ref, kg_ref, perm_ref,
                     qo_ref, ko_ref, vt_ref):
    rows = cos_ref.shape[0]
    cos = jnp.concatenate([cos_ref[...]] * 2, axis=1)
    sin = jnp.concatenate([sin_ref[...]] * 2, axis=1)
    perm = perm_ref[...]
    q_scale = HEAD_DIM ** -0.5 * LOG2E
    q_own = cos * (jnp.concatenate([qg_ref[0:1, :]] * 2, axis=1) * q_scale)
    q_other = sin * (jnp.concatenate([qg_ref[1:2, :]] * 2, axis=1) * q_scale)
    k_own = cos * jnp.concatenate([kg_ref[0:1, :]] * 2, axis=1)
    k_other = sin * jnp.concatenate([kg_ref[1:2, :]] * 2, axis=1)
    for p in range(N_Q_HEADS // 2):
        sl = slice(p * PAIR_W, (p + 1) * PAIR_W)
        y = _norm_rope_pair(q_ref[:, sl].astype(F32), q_own, q_other, perm)
        qo_ref[:, sl] = y.astype(qo_ref.dtype)
    for p in range(N_KV_HEADS // 2):
        sl = slice(p * PAIR_W, (p + 1) * PAIR_W)
        y = _norm_rope_pair(k_ref[:, sl].astype(F32), k_own, k_other, perm)
        ko_ref[:, sl] = y.astype(ko_ref.dtype)
    for h in range(N_KV_HEADS):
        sl = slice(h * HEAD_DIM, (h + 1) * HEAD_DIM)
        vt_ref[h, 0, :HEAD_DIM, :] = v_ref[:, sl].astype(F32).T.astype(vt_ref.dtype)
        vt_ref[h, 0, HEAD_DIM:, :] = jnp.ones((V_ROWS - HEAD_DIM, rows), vt_ref.dtype)


def qkv_prep(proj, cos, sin, q_gain, k_gain, key_block, segments, rows=512):
    t = proj.shape[0]
    rows = min(rows, key_block)
    per_key_block = key_block // rows
    assert all(s0 % rows == 0 for s0, _ in segments)

    def position_block(i):
        block = i
        for s0, _ in segments:
            block = jnp.where(i * rows >= s0, i - s0 // rows, block)
        return block

    def with_partner(gain):
        return jnp.stack([gain, gain[np.arange(HEAD_DIM) ^ (ROPE_HALF // 2)]])

    return pl.pallas_call(
        _qkv_prep_kernel,
        grid=(t // rows,),
        in_specs=[pl.BlockSpec((rows, ATTN_WIDTH), lambda i: (i, Q_OFF // ATTN_WIDTH)),
                  pl.BlockSpec((rows, KV_WIDTH), lambda i: (i, K_OFF // KV_WIDTH)),
                  pl.BlockSpec((rows, KV_WIDTH), lambda i: (i, V_OFF // KV_WIDTH)),
                  pl.BlockSpec((rows, HEAD_DIM), lambda i: (position_block(i), 0)),
                  pl.BlockSpec((rows, HEAD_DIM), lambda i: (position_block(i), 0)),
                  pl.BlockSpec((2, HEAD_DIM), lambda i: (0, 0)),
                  pl.BlockSpec((2, HEAD_DIM), lambda i: (0, 0)),
                  pl.BlockSpec((PAIR_W, PAIR_W), lambda i: (0, 0))],
        out_specs=[pl.BlockSpec((rows, ATTN_WIDTH), lambda i: (i, 0)),
                   pl.BlockSpec((rows, KV_WIDTH), lambda i: (i, 0)),
                   pl.BlockSpec((N_KV_HEADS, 1, V_ROWS, rows),
                                lambda i: (0, i // per_key_block, 0, i % per_key_block))],
        out_shape=[jax.ShapeDtypeStruct((t, ATTN_WIDTH), BF16),
                   jax.ShapeDtypeStruct((t, KV_WIDTH), BF16),
                   jax.ShapeDtypeStruct((N_KV_HEADS, t // key_block, V_ROWS, key_block), BF16)],
        compiler_params=_params(("parallel",), 40),
        name="qkv_prep",
    )(proj, proj, proj, cos, sin, with_partner(q_gain), with_partner(k_gain), _rope_partner_matrix())


def _attn_kernel(q_ref, k_ref, vt_ref, *rest, bk, nk):
    o_ref, m_scr, acc_scr, s_scr, smax_scr = rest[-5:]
    m_scr[...] = jnp.full(m_scr.shape, -jnp.inf, F32)
    acc_scr[...] = jnp.zeros(acc_scr.shape, F32)

    def scores(kb, h, slot):
        s = lax.dot_general(kb, q_ref[:, h * HEAD_DIM:(h + 1) * HEAD_DIM], (((1,), (1,)), ((), ())),
                            preferred_element_type=F32)
        s_scr[slot] = s
        smax_scr[slot] = jnp.max(s, axis=0, keepdims=True)

    def key_block(j):
        start = j * bk if isinstance(j, int) else pl.multiple_of(j * bk, bk)
        return k_ref[pl.ds(start, bk), :]

    scores(key_block(0), 0, 0)

    def block_units(j, has_next):
        kb = key_block(j)
        vb = vt_ref[j]
        for h in range(Q_PER_KV):
            slot = h % 2
            if h + 1 < Q_PER_KV:
                scores(kb, h + 1, 1 - slot)
            elif has_next:
                scores(key_block(j + 1), 0, 1 - slot)
            m_prev = m_scr[h]
            m_new = jnp.maximum(m_prev, smax_scr[slot])
            alpha = jnp.exp2(m_prev - m_new)
            p = jnp.exp2(s_scr[slot] - m_new).astype(vb.dtype)
            acc_scr[h] = acc_scr[h] * alpha + jnp.dot(vb, p, preferred_element_type=F32)
            m_scr[h] = m_new

    per_trip = max(1, min(ATTN_KEYS_PER_LOOP_TRIP // bk, nk // 3))
    assert nk % per_trip == 0

    def trip(i, carry):
        for u in range(per_trip):
            block_units(i * per_trip + u, True)
        return carry

    lax.fori_loop(0, nk // per_trip - 1, trip, 0)
    for u in range(per_trip):
        block_units(nk - per_trip + u, u + 1 < per_trip)
    for h in range(Q_PER_KV):
        acc = acc_scr[h]
        o = acc[:HEAD_DIM, :] / acc[HEAD_DIM:HEAD_DIM + 1, :]
        o_ref[:, h * HEAD_DIM:(h + 1) * HEAD_DIM] = o.T.astype(o_ref.dtype)


def attention(q_rot, k_rot, vt, row0, batch, seq, bq=512):
    bk = vt.shape[-1]
    bq = min(bq, seq)
    assert row0 % seq == 0 and seq % bq == 0 and seq % bk == 0
    group_w = Q_PER_KV * HEAD_DIM
    nq = seq // bq
    nk = seq // bk
    in_specs = [
        pl.BlockSpec((bq, group_w), lambda b, h, i: (row0 // bq + b * nq + i, h)),
        pl.BlockSpec((seq, HEAD_DIM), lambda b, h, i: (row0 // seq + b, h)),
        pl.BlockSpec((None, nk, V_ROWS, bk), lambda b, h, i: (h, row0 // seq + b, 0, 0)),
    ]
    return pl.pallas_call(
        functools.partial(_attn_kernel, bk=bk, nk=nk),
        grid=(batch, N_KV_HEADS, nq),
        in_specs=in_specs,
        out_specs=pl.BlockSpec((bq, group_w), lambda b, h, i: (b * nq + i, h)),
        out_shape=jax.ShapeDtypeStruct((batch * seq, ATTN_WIDTH), BF16),
        scratch_shapes=[pltpu.VMEM((Q_PER_KV, 1, bq), F32),
                        pltpu.VMEM((Q_PER_KV, V_ROWS, bq), F32),
                        pltpu.VMEM((2, bk, bq), F32),
                        pltpu.VMEM((2, 1, bq), F32)],
        compiler_params=_params(("parallel", "parallel", "parallel"), 48),
        name="attention",
    )(q_rot, k_rot, vt)


def _convpool_kernel(ch_ref, cb_ref, cc_ref, pu_ref,
                     chp_ref, ccp_ref, pup_ref, chn_ref, ccn_ref, pun_ref,
                     cw_ref, pw_ref, ps_ref, o_ref, *, rows, segments):
    i = pl.program_id(0)
    start = i * rows
    pos0 = start
    seq_len = jnp.int32(0)
    has_prev = jnp.bool_(True)
    has_next = jnp.bool_(True)
    for seg_start, seg_len in segments:
        inside = (start >= seg_start) & (start < seg_start + seg_len)
        pos0 = jnp.where(inside, start - seg_start, pos0)
        seq_len = jnp.where(inside, seg_len, seq_len)
        has_prev = has_prev & (start != seg_start)
        has_next = has_next & (start + rows != seg_start + seg_len)
    prev_on = has_prev.astype(F32)
    next_on = has_next.astype(F32)

    def extended(cur_ref, prev_ref, next_ref):
        return jnp.concatenate([prev_ref[...].astype(F32) * prev_on,
                                cur_ref[...].astype(F32),
                                next_ref[...].astype(F32) * next_on], axis=0)

    ext_rows = rows + 2 * HALO

    def shifted(x, d):
        return pltpu.roll(x, d % ext_rows, axis=0)

    u = extended(cc_ref, ccp_ref, ccn_ref) * extended(ch_ref, chp_ref, chn_ref)
    cw = cw_ref[...]
    y = shifted(u, 1) * cw[0:1, :] + u * cw[1:2, :] + shifted(u, -1) * cw[2:3, :]
    o_ref[:, :CONV_WIDTH] = (cb_ref[...].astype(F32) * y[HALO:HALO + rows, :]).astype(o_ref.dtype)

    x = extended(pu_ref, pup_ref, pun_ref)
    pos = (pos0 + lax.broadcasted_iota(jnp.int32, (rows, 1), 0))
    for gi, w in enumerate(POOL_WINDOWS):
        sl = slice(gi * POOL_GROUP, (gi + 1) * POOL_GROUP)
        xg = x[:, sl]
        tot = xg + shifted(xg, 1)
        half = 1
        while 2 * half < w:
            tot = shifted(tot, half) + shifted(tot, -half)
            half *= 2
        hi = jnp.minimum(pos + w // 2, seq_len)
        lo = jnp.maximum(pos - w // 2, 0)
        inv_cnt = 1.0 / (hi - lo).astype(F32)
        m = tot[HALO:HALO + rows, :] * inv_cnt - xg[HALO:HALO + rows, :]
        yg = jnp.dot(m.astype(BF16), pw_ref[gi], preferred_element_type=F32)
        o_ref[:, CONV_WIDTH + gi * POOL_GROUP:CONV_WIDTH + (gi + 1) * POOL_GROUP] = (
            yg * ps_ref[:, sl]).astype(o_ref.dtype)


def conv_pool(proj, conv_w, pool_w, pool_scale, segments, rows=512):
    t = proj.shape[0]
    rows = min(rows, min(n for _, n in segments))
    assert all(s % rows == 0 and n % rows == 0 for s, n in segments)
    w = CONV_WIDTH
    per = rows // HALO
    last = t // HALO - 1

    def cur(off):
        return pl.BlockSpec((rows, w), lambda i: (i, off // w))

    def prev(off):
        return pl.BlockSpec((HALO, w), lambda i: (jnp.maximum(i * per - 1, 0), off // w))

    def nxt(off):
        return pl.BlockSpec((HALO, w), lambda i: (jnp.minimum((i + 1) * per, last), off // w))

    kernel = functools.partial(_convpool_kernel, rows=rows, segments=segments)
    return pl.pallas_call(
        kernel,
        grid=(t // rows,),
        in_specs=[cur(CH_OFF), cur(CB_OFF), cur(CC_OFF), cur(PU_OFF),
                  prev(CH_OFF), prev(CC_OFF), prev(PU_OFF),
                  nxt(CH_OFF), nxt(CC_OFF), nxt(PU_OFF),
                  pl.BlockSpec((3, w), lambda i: (0, 0)),
                  pl.BlockSpec((len(POOL_WINDOWS), POOL_GROUP, POOL_GROUP), lambda i: (0, 0, 0)),
                  pl.BlockSpec((1, POOL_WIDTH), lambda i: (0, 0))],
        out_specs=pl.BlockSpec((rows, CONV_WIDTH + POOL_WIDTH), lambda i: (i, 0)),
        out_shape=jax.ShapeDtypeStruct((t, CONV_WIDTH + POOL_WIDTH), BF16),
        compiler_params=_params(("parallel",), 48),
        name="conv_pool",
    )(proj, proj, proj, proj, proj, proj, proj, proj, proj, proj,
      conv_w, pool_w, pool_scale.reshape(1, POOL_WIDTH))


def _trunk(x_parts, segments, attn_calls, ln_mix_pre, ln_mix_post, ln_ffn_pre, ln_ffn_post, q_norm, k_norm,
           w_in, conv_w, pool_w, pool_scale, w_out, w_gate, w_up, w_down):
    total = sum(n for _, _, n in x_parts)
    cos, sin = _rope_tables(max(n for _, n in segments))
    key_block = min(ATTN_KEY_BLOCK, min(n for _, n in segments))
    out_parts = [(row0, n) for _, row0, n in x_parts]
    h = norm_cast(x_parts, ln_mix_pre[0], total)
    w_in_bf16 = cast_weight(w_in, 0)
    d_model = w_out.shape[2]
    for l in range(DEPTH):
        proj, (w_out_bf16, w_gate_bf16, w_up_bf16) = matmul(
            h, w_in_bf16, BF16,
            [(w_out, l, w_out.shape[1], d_model), (w_gate, l, d_model, D_FF_PAD), (w_up, l, d_model, D_FF_PAD)])
        q_rot, k_rot, vt = qkv_prep(proj, cos, sin, q_norm[l], k_norm[l], key_block, segments)
        a_parts = [(attention(q_rot, k_rot, vt, row0, batch, seq), row0, batch * seq)
                   for row0, batch, seq in attn_calls]
        cp_out = conv_pool(proj, conv_w[l], pool_w[l].astype(BF16), pool_scale[l], segments)
        mix = matmul_two_lhs(a_parts, cp_out, w_out_bf16)
        x, h = resnorm(mix, x_parts, ln_mix_post[l], ln_ffn_pre[l])
        x_parts = [(x, 0, total)]
        jobs = [(w_down, l, D_FF_PAD, d_model)]
        if l + 1 < DEPTH:
            jobs.append((w_in, l + 1, d_model, w_in.shape[2]))
        act, cast = gate_up(h, w_gate_bf16, w_up_bf16, w_gate.shape[2], jobs)
        w_down_bf16 = cast[0]
        w_in_bf16 = cast[1] if l + 1 < DEPTH else None
        f = matmul_split_k(act, w_down_bf16, D_FF_PAD // 4, w_down.shape[1])
        if l + 1 < DEPTH:
            x, h = resnorm(f, x_parts, ln_ffn_post[l], ln_mix_pre[l + 1])
            x_parts = [(x, 0, total)]
    return resnorm_last_to_parts(f, x, ln_ffn_post[DEPTH - 1], out_parts)


def kernel(x_prompt, x_sample, ln_mix_pre, ln_mix_post, ln_ffn_pre, ln_ffn_post, q_norm, k_norm,
           w_in, conv_w, pool_w, pool_scale, w_out, w_gate, w_up, w_down):
    pb, ps, d = x_prompt.shape
    sb, ss, _ = x_sample.shape
    x_parts = [(x_prompt.reshape(pb * ps, d), 0, pb * ps), (x_sample.reshape(sb * ss, d), pb * ps, sb * ss)]
    segments = tuple((b * ps, ps) for b in range(pb)) + tuple((pb * ps + b * ss, ss) for b in range(sb))
    attn_calls = ((0, pb, ps), (pb * ps, sb, ss))
    y_prompt, y_sample = _trunk(x_parts, segments, attn_calls, ln_mix_pre, ln_mix_post, ln_ffn_pre, ln_ffn_post,
                                q_norm, k_norm, w_in, conv_w, pool_w, pool_scale, w_out, w_gate, w_up, w_down)
    return (y_prompt.reshape(pb, ps, d), y_sample.reshape(sb, ss, d))
```
